```python
import jax, jax.numpy as jnp
from jax import lax
import numpy as np

D_MODEL = 2048
BATCH = 1
SEQ = 8192
DEPTH = 2
DEC_BATCH = 128
DEC_SEQ = 8
PAST_LEN = 2048
PAGE_SIZE = 128

HEAD_DIM = 128
N_HEADS = D_MODEL // HEAD_DIM
N_KV_HEADS = 4
N_REP = N_HEADS // N_KV_HEADS
IDX_HEADS = 16
IDX_DIM = 64
TOPK_MAX = 256
TOPK_DIV = 4
Q_BLOCK = 128
CONV_W = 3
N_EXPERTS = 32
TOP_K = 4
D_FF = D_MODEL
SWIGLU_LIMIT = 7.0
SWIGLU_ALPHA = 1.702
ROPE_THETA = 10000.0
LN_EPS = 1e-5
MOE_BLOCK = 128
N_MIXERS = 2
N_ATTN_LAYERS = (DEPTH + 1) // 2
N_CONV_LAYERS = DEPTH // 2
DEEPNORM_ALPHA = (2.0 * DEPTH) ** 0.25
DEEPNORM_BETA = (8.0 * DEPTH) ** -0.25
ATTN_SPLITS = (N_HEADS * HEAD_DIM, N_KV_HEADS * HEAD_DIM, N_KV_HEADS * HEAD_DIM, IDX_HEADS * IDX_DIM, IDX_DIM, IDX_HEADS)
ATTN_IN_WIDTH = sum(ATTN_SPLITS)

kernel_name = 'dsa_shortconv_moe_hybrid_step'


def _split_points(sizes):
    pts, acc = [], 0
    for s in sizes[:-1]:
        acc += s
        pts.append(acc)
    return pts


def _layer_norm(x, g, b):
    xf = x.astype(jnp.float32)
    mu = jnp.mean(xf, axis=-1, keepdims=True)
    var = jnp.mean(jnp.square(xf - mu), axis=-1, keepdims=True)
    return ((xf - mu) * lax.rsqrt(var + LN_EPS) * g + b).astype(x.dtype)


def _rope(x, pos):
    d = x.shape[-1]
    inv = ROPE_THETA ** (-jnp.arange(0, d, 2, dtype=jnp.float32) / d)
    ang = pos.astype(jnp.float32)[:, None] * inv[None, :]
    cos = jnp.cos(ang)[:, None, :].astype(x.dtype)
    sin = jnp.sin(ang)[:, None, :].astype(x.dtype)
    x1, x2 = jnp.split(x, 2, axis=-1)
    return jnp.concatenate([x1 * cos - x2 * sin, x2 * cos + x1 * sin], axis=-1)


def _take_rows(rows, idx):
    return jax.vmap(lambda r, i: r[i])(rows, idx)


def _dsa_project(u, pos, w_in, kn_g, kn_b):
    B, T, _ = u.shape
    q, k, v, qi, ki, wh = jnp.split(u @ w_in, _split_points(ATTN_SPLITS), axis=-1)
    q = _rope(q.reshape(B, T, N_HEADS, HEAD_DIM), pos)
    k = _rope(k.reshape(B, T, N_KV_HEADS, HEAD_DIM), pos)
    v = v.reshape(B, T, N_KV_HEADS, HEAD_DIM)
    qi = _rope(qi.reshape(B, T, IDX_HEADS, IDX_DIM), pos)
    ki = _rope(_layer_norm(ki, kn_g, kn_b)[:, :, None, :], pos)[:, :, 0, :]
    wh = wh * (IDX_HEADS ** -0.5)
    return q, k, v, qi, ki, wh


def _indexer_topk(qi, wh, ki_all, q_pos, k_pos, topk):
    dots = jnp.einsum('bqhd,bsd->bqhs', qi, ki_all, preferred_element_type=jnp.float32) * (IDX_DIM ** -0.5)
    score = jnp.einsum('bqhs,bqh->bqs', jax.nn.relu(dots), wh.astype(jnp.float32))
    causal = k_pos[None, :] <= q_pos[:, None]
    score = jnp.where(causal[None], score, -jnp.inf)
    _, idx = lax.top_k(score, topk)
    valid = jnp.take(k_pos, idx) <= q_pos[None, :, None]
    return idx, valid


def _sparse_attend(q, k_sel, v_sel, valid):
    B, Tq = q.shape[:2]
    qg = q.reshape(B, Tq, N_KV_HEADS, N_REP, HEAD_DIM)
    s = jnp.einsum('bqgrd,bqkgd->bqgrk', qg, k_sel, preferred_element_type=jnp.float32) * (HEAD_DIM ** -0.5)
    s = jnp.where(valid[:, :, None, None, :], s, -jnp.inf)
    p = jax.nn.softmax(s, axis=-1).astype(v_sel.dtype)
    o = jnp.einsum('bqgrk,bqkgd->bqgrd', p, v_sel)
    return o.reshape(B, Tq, N_HEADS * HEAD_DIM)


def _dsa_prompt(u, w_in, kn_g, kn_b, w_out):
    B, S, _ = u.shape
    pos = jnp.arange(S, dtype=jnp.int32)
    q, k, v, qi, ki, wh = _dsa_project(u, pos, w_in, kn_g, kn_b)
    topk = min(TOPK_MAX, S // TOPK_DIV)
    nb = S // Q_BLOCK

    def to_blocks(a):
        return a.reshape((B, nb, Q_BLOCK) + a.shape[2:]).swapaxes(0, 1)

    def block(args):
        qb, qib, whb, qpos = args
        idx, valid = _indexer_topk(qib, whb, ki, qpos, pos, topk)
        return _sparse_attend(qb, _take_rows(k, idx), _take_rows(v, idx), valid)

    o = lax.map(block, (to_blocks(q), to_blocks(qi), to_blocks(wh), pos.reshape(nb, Q_BLOCK)))
    o = o.swapaxes(0, 1).reshape(B, S, N_HEADS * HEAD_DIM)
    return o @ w_out, k, v, ki


def _dsa_sample(u, cache_k, cache_v, cache_ki, page_table, w_in, kn_g, kn_b, w_out):
    B, T, _ = u.shape
    page = cache_k.shape[1]
    past = page_table.shape[1] * page
    pos_new = past + jnp.arange(T, dtype=jnp.int32)
    q, k, v, qi, ki, wh = _dsa_project(u, pos_new, w_in, kn_g, kn_b)
    ki_past = cache_ki[page_table].reshape(B, past, IDX_DIM)
    ki_all = jnp.concatenate([ki_past, ki], axis=1)
    k_pos = jnp.arange(past + T, dtype=jnp.int32)
    topk = min(TOPK_MAX, (past + T) // TOPK_DIV)
    idx, valid = _indexer_topk(qi, wh, ki_all, pos_new, k_pos, topk)
    is_new = (idx >= past)[..., None, None]
    s_past = jnp.minimum(idx, past - 1)
    phys = _take_rows(page_table, s_past // page)
    off = s_past % page
    s_new = jnp.maximum(idx - past, 0)
    k_sel = jnp.where(is_new, _take_rows(k, s_new), cache_k[phys, off])
    v_sel = jnp.where(is_new, _take_rows(v, s_new), cache_v[phys, off])
    o = _sparse_attend(q, k_sel, v_sel, valid)
    return o @ w_out, k, v, ki


def _short_conv(u, prev, w_in, kernel, w_out):
    T = u.shape[1]
    b_gate, c_gate, xin = jnp.split(u @ w_in, 3, axis=-1)
    vbuf = jnp.concatenate([prev, c_gate * xin], axis=1)
    conv = kernel[0] * vbuf[:, 0:T]
    for j in range(1, CONV_W):
        conv = conv + kernel[j] * vbuf[:, j:j + T]
    return (b_gate * conv) @ w_out, vbuf[:, vbuf.shape[1] - (CONV_W - 1):]


def _swiglu_clamped(h):
    gate, up = h[..., ::2], h[..., 1::2]
    gate = jnp.minimum(gate, SWIGLU_LIMIT)
    up = jnp.clip(up, -SWIGLU_LIMIT, SWIGLU_LIMIT)
    return gate * jax.nn.sigmoid(SWIGLU_ALPHA * gate) * (up + 1.0)


def _moe(u, w_r, b_r, w_up, b_up, w_down, b_down):
    lead = u.shape[:-1]
    xt = u.reshape(-1, D_MODEL)
    n_tok = xt.shape[0]
    logits = (xt @ w_r + b_r).astype(jnp.float32)
    top_val, top_e = lax.top_k(logits, TOP_K)
    gates = jax.nn.softmax(top_val, axis=-1)
    n_assign = n_tok * TOP_K
    e_flat = top_e.reshape(-1)
    tok_flat = jnp.arange(n_assign, dtype=jnp.int32) // TOP_K
    order = jnp.argsort(e_flat)
    e_sorted = e_flat[order]
    counts = jnp.bincount(e_flat, length=N_EXPERTS)
    starts = jnp.cumsum(counts) - counts
    padded = (counts + MOE_BLOCK - 1) // MOE_BLOCK * MOE_BLOCK
    ends = jnp.cumsum(padded)
    slot = (ends - padded)[e_sorted] + jnp.arange(n_assign, dtype=jnp.int32) - starts[e_sorted]
    n_blocks = -(-n_assign // MOE_BLOCK) + N_EXPERTS
    n_slots = n_blocks * MOE_BLOCK
    slot_tok = jnp.full((n_slots,), n_tok, jnp.int32).at[slot].set(tok_flat[order])
    slot_gate = jnp.zeros((n_slots,), jnp.float32).at[slot].set(gates.reshape(-1)[order])
    block_e = jnp.minimum(jnp.searchsorted(ends, jnp.arange(n_blocks, dtype=jnp.int32) * MOE_BLOCK, side='right'), N_EXPERTS - 1)
    x_pad = jnp.concatenate([xt, jnp.zeros((1, D_MODEL), xt.dtype)], axis=0)
    xs = x_pad[slot_tok].reshape(n_blocks, MOE_BLOCK, D_MODEL)

    def expert_block(args):
        xb, e = args
        h = _swiglu_clamped(xb @ w_up[e] + b_up[e])
        return h @ w_down[e] + b_down[e]

    out = lax.map(expert_block, (xs, block_e)).reshape(n_slots, D_MODEL)
    y = jnp.zeros((n_tok + 1, D_MODEL), out.dtype).at[slot_tok].add(out * slot_gate[:, None].astype(out.dtype))
    return y[:n_tok].reshape(lead + (D_MODEL,))


def _modulate(x, shift, scale):
    return x * (1.0 + scale) + shift


def _post_norm(x, gate, out, g, b):
    return _layer_norm(DEEPNORM_ALPHA * x + gate * out, g, b)


def setup_inputs(seed: int = 0) -> dict:
    key = jax.random.key(seed)
    ks = jax.random.split(key, 28)
    n_pages = PAST_LEN // PAGE_SIZE
    n_used = DEC_BATCH * n_pages
    n_pool = n_used + (n_used + 3) // 4
    page_table = jax.random.permutation(ks[0], n_pool)[:n_used].reshape(DEC_BATCH, n_pages).astype(jnp.int32)

    def nrm(k, shape, s):
        return jax.random.normal(k, shape, jnp.float32) * s

    return {
        'x_prompt': nrm(ks[1], (BATCH, SEQ, D_MODEL), 1.0),
        'x_sample': nrm(ks[2], (DEC_BATCH, DEC_SEQ, D_MODEL), 1.0),
        'cache_k': nrm(ks[3], (N_ATTN_LAYERS, n_pool, PAGE_SIZE, N_KV_HEADS, HEAD_DIM), 1.0),
        'cache_v': nrm(ks[4], (N_ATTN_LAYERS, n_pool, PAGE_SIZE, N_KV_HEADS, HEAD_DIM), 1.0),
        'cache_kidx': nrm(ks[5], (N_ATTN_LAYERS, n_pool, PAGE_SIZE, IDX_DIM), 1.0),
        'state_conv': nrm(ks[6], (N_CONV_LAYERS, DEC_BATCH, CONV_W - 1, D_MODEL), 1.0),
        'page_table': page_table,
        'c_prompt': nrm(ks[7], (BATCH, D_MODEL), 1.0),
        'c_sample': nrm(ks[8], (DEC_BATCH, D_MODEL), 1.0),
        'ada_w': nrm(ks[9], (DEPTH, D_MODEL, 6 * D_MODEL), 0.5 * D_MODEL ** -0.5),
        'ada_b': nrm(ks[10], (DEPTH, 6 * D_MODEL), 0.02),
        'ln_g': 1.0 + nrm(ks[11], (DEPTH, 2, D_MODEL), 0.02),
        'ln_b': nrm(ks[12], (DEPTH, 2, D_MODEL), 0.02),
        'attn_w_in': nrm(ks[13], (N_ATTN_LAYERS, D_MODEL, ATTN_IN_WIDTH), D_MODEL ** -0.5),
        'attn_kidx_g': 1.0 + nrm(ks[14], (N_ATTN_LAYERS, IDX_DIM), 0.02),
        'attn_kidx_b': nrm(ks[15], (N_ATTN_LAYERS, IDX_DIM), 0.02),
        'attn_w_out': nrm(ks[16], (N_ATTN_LAYERS, N_HEADS * HEAD_DIM, D_MODEL), DEEPNORM_BETA * (N_HEADS * HEAD_DIM) ** -0.5),
        'conv_w_in': nrm(ks[17], (N_CONV_LAYERS, D_MODEL, 3 * D_MODEL), D_MODEL ** -0.5),
        'conv_kernel': nrm(ks[18], (N_CONV_LAYERS, CONV_W, D_MODEL), CONV_W ** -0.5),
        'conv_w_out': nrm(ks[19], (N_CONV_LAYERS, D_MODEL, D_MODEL), DEEPNORM_BETA * D_MODEL ** -0.5),
        'router_w': nrm(ks[20], (DEPTH, D_MODEL, N_EXPERTS), D_MODEL ** -0.5),
        'router_b': nrm(ks[21], (DEPTH, N_EXPERTS), 0.01),
        'expert_w_up': nrm(ks[22], (DEPTH, N_EXPERTS, D_MODEL, 2 * D_FF), D_MODEL ** -0.5),
        'expert_b_up': nrm(ks[23], (DEPTH, N_EXPERTS, 2 * D_FF), 0.02),
        'expert_w_down': nrm(ks[24], (DEPTH, N_EXPERTS, D_FF, D_MODEL), DEEPNORM_BETA * D_FF ** -0.5),
        'expert_b_down': nrm(ks[25], (DEPTH, N_EXPERTS, D_MODEL), 0.02),
    }


def reference(x_prompt, x_sample, cache_k, cache_v, cache_kidx, state_conv, page_table, c_prompt, c_sample,
              ada_w, ada_b, ln_g, ln_b, attn_w_in, attn_kidx_g, attn_kidx_b, attn_w_out,
              conv_w_in, conv_kernel, conv_w_out, router_w, router_b,
              expert_w_up, expert_b_up, expert_w_down, expert_b_down):
    xp, xs = x_prompt, x_sample
    kp_l, vp_l, kip_l, ks_l, vs_l, kis_l, cp_l, cs_l = [], [], [], [], [], [], [], []
    for i in range(DEPTH):
        mod_p = jax.nn.silu(c_prompt) @ ada_w[i] + ada_b[i]
        mod_s = jax.nn.silu(c_sample) @ ada_w[i] + ada_b[i]
        sh1p, sc1p, g1p, sh2p, sc2p, g2p = jnp.split(mod_p[:, None, :], 6, axis=-1)
        sh1s, sc1s, g1s, sh2s, sc2s, g2s = jnp.split(mod_s[:, None, :], 6, axis=-1)
        up = _modulate(xp, sh1p, sc1p)
        us = _modulate(xs, sh1s, sc1s)
        if i % N_MIXERS == 0:
            a = i // N_MIXERS
            mix_p, kp, vp, kip = _dsa_prompt(up, attn_w_in[a], attn_kidx_g[a], attn_kidx_b[a], attn_w_out[a])
            mix_s, ksn, vsn, kisn = _dsa_sample(us, cache_k[a], cache_v[a], cache_kidx[a], page_table,
                                                attn_w_in[a], attn_kidx_g[a], attn_kidx_b[a], attn_w_out[a])
            kp_l.append(kp); vp_l.append(vp); kip_l.append(kip)
            ks_l.append(ksn); vs_l.append(vsn); kis_l.append(kisn)
        else:
            cidx = i // N_MIXERS
            zero_prev = jnp.zeros((up.shape[0], CONV_W - 1, D_MODEL), up.dtype)
            mix_p, stp = _short_conv(up, zero_prev, conv_w_in[cidx], conv_kernel[cidx], conv_w_out[cidx])
            mix_s, sts = _short_conv(us, state_conv[cidx], conv_w_in[cidx], conv_kernel[cidx], conv_w_out[cidx])
            cp_l.append(stp); cs_l.append(sts)
        xp = _post_norm(xp, g1p, mix_p, ln_g[i, 0], ln_b[i, 0])
        xs = _post_norm(xs, g1s, mix_s, ln_g[i, 0], ln_b[i, 0])
        moe_p = _moe(_modulate(xp, sh2p, sc2p), router_w[i], router_b[i], expert_w_up[i], expert_b_up[i], expert_w_down[i], expert_b_down[i])
        moe_s = _moe(_modulate(xs, sh2s, sc2s), router_w[i], router_b[i], expert_w_up[i], expert_b_up[i], expert_w_down[i], expert_b_down[i])
        xp = _post_norm(xp, g2p, moe_p, ln_g[i, 1], ln_b[i, 1])
        xs = _post_norm(xs, g2s, moe_s, ln_g[i, 1], ln_b[i, 1])
    new_k_prompt = jnp.stack(kp_l)
    new_v_prompt = jnp.stack(vp_l)
    new_kidx_prompt = jnp.stack(kip_l)
    new_k_sample = jnp.stack(ks_l)
    new_v_sample = jnp.stack(vs_l)
    new_kidx_sample = jnp.stack(kis_l)
    new_conv_prompt = jnp.stack(cp_l)
    new_conv_sample = jnp.stack(cs_l)
    return (xp, xs, new_k_prompt, new_v_prompt, new_kidx_prompt, new_k_sample, new_v_sample, new_kidx_sample, new_conv_prompt, new_conv_sample)
```

```python
import functools

import jax
import jax.numpy as jnp
from jax import lax
from jax.experimental import pallas as pl
from jax.experimental.pallas import tpu as pltpu

F32 = jnp.float32
BF16 = jnp.bfloat16
I32 = jnp.int32

HEAD_DIM = 128
IDX_HEADS = 16
IDX_DIM = 64
TOPK_MAX = 256
TOPK_DIV = 4
CONV_W = 3
TOP_K = 4
SWIGLU_LIMIT = 7.0
SWIGLU_ALPHA = 1.702
ROPE_THETA = 10000.0
LN_EPS = 1e-5

LANES = 128
SUBLANES = 8
V7X_VMEM_BYTES = 64 * 1024 * 1024
VMEM_CAP_BYTES = V7X_VMEM_BYTES - 8 * 1024 * 1024

TOKEN_BLOCK = 256
Q_TILE = 128
KEY_CHUNK = 512
SEL_ROWS = 256
MOE_ROWS = 128
MOE_SUB = 8
MOE_FF_TILE = 256
GATHER_WINDOW = 32
ADA_N_TILE = 1024

INT_MIN = -2 ** 31
KEY_NEG_INF = -2139095041
NEG_BIAS = -2e30
M_FLOOR = -1e30


def _cparams(sem, est_bytes):
    limit = int(min(max(est_bytes, 16 * 1024 * 1024), VMEM_CAP_BYTES))
    return pltpu.CompilerParams(dimension_semantics=sem, vmem_limit_bytes=limit)


def _dot(a, b):
    return jnp.dot(a, b, preferred_element_type=F32)


def _dot_nt(a, b):
    return lax.dot_general(a, b, (((1,), (1,)), ((), ())), preferred_element_type=F32)


def _rep_lanes(x, n):
    return x if n == 1 else jnp.concatenate([x] * n, axis=1)


def _layer_norm_rows(y, g, b):
    mu = jnp.mean(y, axis=-1, keepdims=True)
    d = y - mu
    var = jnp.mean(d * d, axis=-1, keepdims=True)
    return d * lax.rsqrt(var + LN_EPS) * g + b


def _sort_key(s):
    bits = lax.bitcast_convert_type(s, I32)
    return bits ^ (jnp.right_shift(bits, 31) & 0x7FFFFFFF)


def _ada_kernel(c_ref, w_ref, b_ref, o_ref):
    c = c_ref[...]
    a = (c * jax.nn.sigmoid(c)).astype(BF16)
    o_ref[0] = _dot(a, w_ref[0].astype(BF16)) + b_ref[0]


def _ada_mod(c_all, ada_w, ada_b):
    depth, d, n6 = ada_w.shape
    mc = c_all.shape[0]
    tn = ADA_N_TILE
    est = 2 * (d * tn * 4 + mc * tn * 4) + mc * d * 4 * 2 + d * tn * 2
    return pl.pallas_call(
        _ada_kernel,
        grid=(depth, n6 // tn),
        in_specs=[
            pl.BlockSpec((mc, d), lambda l, j: (0, 0)),
            pl.BlockSpec((1, d, tn), lambda l, j: (l, 0, j)),
            pl.BlockSpec((1, 1, tn), lambda l, j: (l, 0, j)),
        ],
        out_specs=pl.BlockSpec((1, mc, tn), lambda l, j: (l, 0, j)),
        out_shape=jax.ShapeDtypeStruct((depth, mc, n6), F32),
        compiler_params=_cparams(("parallel", "parallel"), est),
        name="ada_mod",
    )(c_all, ada_w, ada_b.reshape(depth, 1, n6))


def _proj_attn_kernel(x_ref, sh_ref, sc_ref, w_ref, c128_ref, s128_ref, c64_ref, s64_ref, kng_ref, knb_ref,
                      q_ref, k_ref, v_ref, kb_ref, vb_ref, qa_ref, qb_ref, kw_ref, ke_ref, *, d_model, n_kv):
    xb = (x_ref[...] * (1.0 + sc_ref[...]) + sh_ref[...]).astype(BF16)
    c128, s128 = c128_ref[...], s128_ref[...]
    c64, s64 = c64_ref[...], s64_ref[...]
    tm = xb.shape[0]
    lane = lax.broadcasted_iota(I32, (tm, LANES), 1)
    low_half = (lane % IDX_DIM) < (IDX_DIM // 2)

    def rope128(y):
        return y * c128 + pltpu.roll(y, HEAD_DIM // 2, 1) * s128

    def rope64(y):
        rot = jnp.where(low_half, pltpu.roll(y, LANES - IDX_DIM // 2, 1), pltpu.roll(y, IDX_DIM // 2, 1))
        return y * c64 + rot * s64

    kvw = n_kv * HEAD_DIM
    col = 0
    for c0 in range(0, d_model, 512):
        y = _dot(xb, w_ref[:, col + c0:col + c0 + 512])
        for t in range(4):
            q_ref[:, c0 + t * LANES:c0 + (t + 1) * LANES] = rope128(y[:, t * LANES:(t + 1) * LANES]).astype(BF16)
    col += d_model
    y = _dot(xb, w_ref[:, col:col + kvw])
    for t in range(n_kv):
        r = rope128(y[:, t * LANES:(t + 1) * LANES])
        k_ref[:, t * LANES:(t + 1) * LANES] = r
        kb_ref[:, t * LANES:(t + 1) * LANES] = r.astype(BF16)
    col += kvw
    y = _dot(xb, w_ref[:, col:col + kvw])
    v_ref[...] = y
    vb_ref[...] = y.astype(BF16)
    col += kvw
    iw = IDX_HEADS * IDX_DIM
    for c0 in range(0, iw, 512):
        y = _dot(xb, w_ref[:, col + c0:col + c0 + 512])
        for t in range(4):
            r = rope64(y[:, t * LANES:(t + 1) * LANES])
            qa_ref[:, c0 + t * LANES:c0 + (t + 1) * LANES] = r.astype(BF16)
            qb_ref[:, c0 + t * LANES:c0 + (t + 1) * LANES] = pltpu.roll(r, IDX_DIM, 1).astype(BF16)
    col += iw
    y = _dot(xb, w_ref[:, col:col + LANES])
    is_key = lane < IDX_DIM
    mu = jnp.sum(jnp.where(is_key, y, 0.0), axis=-1, keepdims=True) * (1.0 / IDX_DIM)
    dlt = jnp.where(is_key, y - mu, 0.0)
    var = jnp.sum(dlt * dlt, axis=-1, keepdims=True) * (1.0 / IDX_DIM)
    kn = dlt * lax.rsqrt(var + LN_EPS) * kng_ref[...] + knb_ref[...]
    ki = rope64(kn)
    wscale = (IDX_HEADS ** -0.5) * (IDX_DIM ** -0.5)
    wh = jnp.where((lane >= IDX_DIM) & (lane < IDX_DIM + IDX_HEADS), y * wscale, 0.0)
    kw_ref[...] = ki + wh
    ke_ref[...] = ki.astype(BF16)


def _mod_spec(tm, d, col, npb):
    return pl.BlockSpec((tm, d), lambda i: (jnp.where(i < npb, 0, i - npb + 1), col))


def _proj_attn(x_all, mb, w_pad, tabs, kng, knb, *, npb, n_kv):
    nt, d = x_all.shape
    tm = TOKEN_BLOCK
    kvw = n_kv * HEAD_DIM
    iw = IDX_HEADS * IDX_DIM
    row = lambda w: pl.BlockSpec((tm, w), lambda i: (i, 0))
    tab = pl.BlockSpec((tm, LANES), lambda i: (i, 0))
    vec = pl.BlockSpec((1, LANES), lambda i: (0, 0))
    out_shapes = (
        jax.ShapeDtypeStruct((nt, d), BF16),
        jax.ShapeDtypeStruct((nt, kvw), F32),
        jax.ShapeDtypeStruct((nt, kvw), F32),
        jax.ShapeDtypeStruct((nt, kvw), BF16),
        jax.ShapeDtypeStruct((nt, kvw), BF16),
        jax.ShapeDtypeStruct((nt, iw), BF16),
        jax.ShapeDtypeStruct((nt, iw), BF16),
        jax.ShapeDtypeStruct((nt, LANES), F32),
        jax.ShapeDtypeStruct((nt, LANES), BF16),
    )
    out_specs = (row(d), row(kvw), row(kvw), row(kvw), row(kvw), row(iw), row(iw), row(LANES), row(LANES))
    est = w_pad.size * 2 + 2 * tm * (3 * d * 4 + 4 * LANES * 4) + 2 * tm * (d * 2 + kvw * 12 + iw * 4 + LANES * 6) \
        + 8 * tm * 512 * 4
    return pl.pallas_call(
        functools.partial(_proj_attn_kernel, d_model=d, n_kv=n_kv),
        grid=(nt // tm,),
        in_specs=[row(d), _mod_spec(tm, d, 0, npb), _mod_spec(tm, d, 1, npb),
                  pl.BlockSpec(memory_space=pltpu.VMEM), tab, tab, tab, tab, vec, vec],
        out_specs=out_specs,
        out_shape=out_shapes,
        compiler_params=_cparams(("parallel",), est),
        name="proj_attn",
    )(x_all, mb, mb, w_pad, *tabs, kng, knb)


def _kth_largest_key(key_ref, n_chunks, chunk, kk):
    rows = key_ref.shape[0]

    def count_ge(cand):
        candb = jnp.broadcast_to(cand, (rows, LANES))

        def body(c, acc):
            base = pl.multiple_of(c * chunk, chunk)
            for j in range(chunk // LANES):
                blk = key_ref[:, pl.ds(base + j * LANES, LANES)]
                acc = acc + jnp.where(blk >= candb, 1.0, 0.0)
            return acc

        acc = lax.fori_loop(0, n_chunks, body, jnp.zeros((rows, LANES), F32))
        return jnp.sum(acc, axis=1, keepdims=True)

    kkf = float(kk)
    prefix = jnp.where(count_ge(jnp.zeros((rows, 1), I32)) >= kkf, 0, INT_MIN).astype(I32)

    def bit_body(b, prefix):
        cand = prefix | jnp.left_shift(jnp.int32(1), 30 - b)
        return jnp.where(count_ge(cand) >= kkf, cand, prefix)

    return lax.fori_loop(0, 31, bit_body, prefix)


def _dsa_prompt_kernel(q_ref, qa_ref, qb_ref, kw_ref, ke_ref, k_ref, v_ref, o_ref,
                       key_ref, whb_ref, m_ref, l_ref, acc_ref, *, tq, tc, topk, n_kv, n_rep):
    i = pl.program_id(0)
    q0 = i * tq
    n_ch = (q0 + tq + tc - 1) // tc
    kw = kw_ref[...]
    for h in range(IDX_HEADS):
        whb_ref[h] = jnp.broadcast_to(kw[:, IDX_DIM + h:IDX_DIM + h + 1], (tq, LANES))
    qpos = q0 + lax.broadcasted_iota(I32, (tq, LANES), 0)
    lane = lax.broadcasted_iota(I32, (tq, LANES), 1)
    n_sub = tc // LANES

    def score_body(c, carry):
        base = pl.multiple_of(c * tc, tc)
        ke = ke_ref[pl.ds(base, tc), :]
        acc = [jnp.zeros((tq, LANES), F32) for _ in range(n_sub)]
        for p in range(IDX_HEADS // 2):
            d_even = _dot_nt(qa_ref[:, p * LANES:(p + 1) * LANES], ke)
            d_odd = _dot_nt(qb_ref[:, p * LANES:(p + 1) * LANES], ke)
            w_even, w_odd = whb_ref[2 * p], whb_ref[2 * p + 1]
            for j in range(n_sub):
                sl = slice(j * LANES, (j + 1) * LANES)
                acc[j] = acc[j] + jnp.maximum(d_even[:, sl], 0.0) * w_even + jnp.maximum(d_odd[:, sl], 0.0) * w_odd
        for j in range(n_sub):
            kpos = base + j * LANES + lane
            key_ref[:, pl.ds(base + j * LANES, LANES)] = jnp.where(kpos <= qpos, _sort_key(acc[j]), INT_MIN)
        return carry

    lax.fori_loop(0, n_ch, score_body, 0)

    thr = jnp.maximum(_kth_largest_key(key_ref, n_ch, tc, topk), KEY_NEG_INF + 1)
    thrb = jnp.broadcast_to(thr, (tq, LANES))
    neg_bits = lax.bitcast_convert_type(jnp.float32(NEG_BIAS), I32)

    def bias_body(c, carry):
        base = pl.multiple_of(c * tc, tc)
        for j in range(n_sub):
            sl = pl.ds(base + j * LANES, LANES)
            key_ref[:, sl] = jnp.where(key_ref[:, sl] >= thrb, 0, neg_bits)
        return carry

    lax.fori_loop(0, n_ch, bias_body, 0)

    scale = HEAD_DIM ** -0.5
    for g in range(n_kv):
        qg = jnp.concatenate([q_ref[:, (g * n_rep + r) * LANES:(g * n_rep + r + 1) * LANES] for r in range(n_rep)], axis=0)
        m_ref[...] = jnp.full(m_ref.shape, M_FLOOR, F32)
        l_ref[...] = jnp.zeros(l_ref.shape, F32)
        acc_ref[...] = jnp.zeros(acc_ref.shape, F32)

        def att_body(c, carry, g=g, qg=qg):
            base = pl.multiple_of(c * tc, tc)
            kc = k_ref[pl.ds(base, tc), g * LANES:(g + 1) * LANES]
            vc = v_ref[pl.ds(base, tc), g * LANES:(g + 1) * LANES]
            bias = lax.bitcast_convert_type(key_ref[:, pl.ds(base, tc)], F32)
            s = _dot_nt(qg, kc) * scale + jnp.concatenate([bias] * n_rep, axis=0)
            m_old = m_ref[...]
            m_new = jnp.maximum(m_old, jnp.max(s, axis=1, keepdims=True))
            alpha = jnp.exp(m_old - m_new)
            p = jnp.exp(s - _rep_lanes(m_new, n_sub))
            l_ref[...] = alpha * l_ref[...] + jnp.sum(p, axis=1, keepdims=True)
            acc_ref[...] = alpha * acc_ref[...] + _dot(p.astype(BF16), vc)
            m_ref[...] = m_new
            return carry

        lax.fori_loop(0, n_ch, att_body, 0)
        o = acc_ref[...] / l_ref[...]
        for r in range(n_rep):
            h = g * n_rep + r
            o_ref[:, h * LANES:(h + 1) * LANES] = o[r * tq:(r + 1) * tq].astype(BF16)


def _dsa_prompt(q, qa, qb, kw, ke, kb, vb, *, n_prompt, topk, n_kv):
    d = q.shape[1]
    n_rep = d // HEAD_DIM // n_kv
    tq, tc = Q_TILE, min(KEY_CHUNK, n_prompt)
    iw = IDX_HEADS * IDX_DIM
    kvw = n_kv * HEAD_DIM
    whole = pl.BlockSpec(memory_space=pltpu.VMEM)
    row = lambda w: pl.BlockSpec((tq, w), lambda i: (i, 0))
    est = n_prompt * (LANES * 2 + kvw * 4) + tq * n_prompt * 4 + IDX_HEADS * tq * LANES * 4 \
        + 3 * n_rep * tq * LANES * 4 + 2 * tq * (2 * d * 2 + 2 * iw * 2 + LANES * 4) + 10 * n_rep * tq * tc * 4
    return pl.pallas_call(
        functools.partial(_dsa_prompt_kernel, tq=tq, tc=tc, topk=topk, n_kv=n_kv, n_rep=n_rep),
        grid=(n_prompt // tq,),
        in_specs=[row(d), row(iw), row(iw), row(LANES), whole, whole, whole],
        out_specs=row(d),
        out_shape=jax.ShapeDtypeStruct((n_prompt, d), BF16),
        scratch_shapes=[
            pltpu.VMEM((tq, n_prompt), I32),
            pltpu.VMEM((IDX_HEADS, tq, LANES), F32),
            pltpu.VMEM((n_rep * tq, LANES), F32),
            pltpu.VMEM((n_rep * tq, LANES), F32),
            pltpu.VMEM((n_rep * tq, LANES), F32),
        ],
        compiler_params=_cparams(("parallel",), est),
        name="dsa_prompt",
    )(q, qa, qb, kw, ke, kb, vb)


def _dsa_sample_score_kernel(pt_ref, qa_ref, qb_ref, kw_ref, ken_ref, *refs, n_pages, page, t_new):
    page_refs, key_ref = refs[:n_pages], refs[n_pages]
    qa = qa_ref[0]
    qb = qb_ref[0]
    kw = kw_ref[0]
    n_pair = IDX_HEADS // 2
    q_even = jnp.concatenate([qa[:, p * LANES:(p + 1) * LANES] for p in range(n_pair)], axis=0).astype(BF16)
    q_odd = jnp.concatenate([qb[:, p * LANES:(p + 1) * LANES] for p in range(n_pair)], axis=0).astype(BF16)
    w_even = [jnp.broadcast_to(kw[:, IDX_DIM + 2 * p:IDX_DIM + 2 * p + 1], (t_new, LANES)) for p in range(n_pair)]
    w_odd = [jnp.broadcast_to(kw[:, IDX_DIM + 2 * p + 1:IDX_DIM + 2 * p + 2], (t_new, LANES)) for p in range(n_pair)]

    def scores(ke):
        d_even = _dot_nt(q_even, ke)
        d_odd = _dot_nt(q_odd, ke)
        s = jnp.zeros((t_new, LANES), F32)
        for p in range(n_pair):
            s = s + jnp.maximum(d_even[p * t_new:(p + 1) * t_new], 0.0) * w_even[p] \
                  + jnp.maximum(d_odd[p * t_new:(p + 1) * t_new], 0.0) * w_odd[p]
        return s

    zeros = jnp.zeros((page, LANES - IDX_DIM), F32)
    for pg in range(n_pages):
        ke = jnp.concatenate([page_refs[pg][0], zeros], axis=1).astype(BF16)
        key_ref[0, :, pg * page:(pg + 1) * page] = _sort_key(scores(ke))
    ke_new = jnp.concatenate([ken_ref[0], jnp.zeros((LANES - t_new, LANES), F32)], axis=0).astype(BF16)
    s_new = scores(ke_new)
    qi = lax.broadcasted_iota(I32, (t_new, LANES), 0)
    kj = lax.broadcasted_iota(I32, (t_new, LANES), 1)
    key_ref[0, :, n_pages * page:n_pages * page + LANES] = jnp.where(kj <= qi, _sort_key(s_new), INT_MIN)


def _dsa_sample_scores(pt_flat, qa_s, qb_s, kw_s, ke_new, cache_ki, *, n_pages):
    b, t_new, iw = qa_s.shape
    page = cache_ki.shape[1]
    width = n_pages * page + LANES
    seq = lambda w: pl.BlockSpec((1, t_new, w), lambda s, pt: (s, 0, 0))
    page_specs = [pl.BlockSpec((1, page, IDX_DIM), lambda s, pt, pg=pg: (pt[s * n_pages + pg], 0, 0))
                  for pg in range(n_pages)]
    est = 2 * (n_pages * page * LANES * 4 + t_new * (2 * iw + 2 * LANES + width) * 4) + 64 * page * LANES * 4
    return pl.pallas_call(
        functools.partial(_dsa_sample_score_kernel, n_pages=n_pages, page=page, t_new=t_new),
        grid_spec=pltpu.PrefetchScalarGridSpec(
            num_scalar_prefetch=1,
            grid=(b,),
            in_specs=[seq(iw), seq(iw), seq(LANES), seq(LANES)] + page_specs,
            out_specs=seq(width),
        ),
        out_shape=jax.ShapeDtypeStruct((b, t_new, width), I32),
        compiler_params=_cparams(("parallel",), est),
        name="dsa_sample_scores",
    )(pt_flat, qa_s, qb_s, kw_s, ke_new, *([cache_ki] * n_pages))


def _threshold_kernel(key_ref, thr_ref, *, topk, n_tiles):
    thr = jnp.maximum(_kth_largest_key(key_ref, n_tiles, LANES, topk), KEY_NEG_INF + 1)
    thr_ref[...] = jnp.broadcast_to(thr, thr_ref.shape)


def _thresholds(keys, *, topk):
    n, width = keys.shape
    tr = min(SEL_ROWS, n)
    est = 2 * tr * (width + LANES) * 4 + 8 * tr * LANES * 4
    return pl.pallas_call(
        functools.partial(_threshold_kernel, topk=topk, n_tiles=width // LANES),
        grid=(n // tr,),
        in_specs=[pl.BlockSpec((tr, width), lambda i: (i, 0))],
        out_specs=pl.BlockSpec((tr, LANES), lambda i: (i, 0)),
        out_shape=jax.ShapeDtypeStruct((n, LANES), I32),
        compiler_params=_cparams(("parallel",), est),
        name="topk_threshold",
    )(keys)


def _dsa_sample_attn_kernel(pt_ref, q_ref, key_ref, thr_ref, kn_ref, vn_ref, *refs,
                            n_pages, page, t_new, n_kv, n_rep):
    k_refs, v_refs = refs[:n_pages], refs[n_pages:2 * n_pages]
    o_ref, kall_ref, vall_ref = refs[2 * n_pages:]
    n_heads = n_kv * n_rep
    kvw = n_kv * HEAD_DIM
    past = n_pages * page
    for pg in range(n_pages):
        kall_ref[pg * page:(pg + 1) * page, :] = k_refs[pg][0].astype(BF16)
        vall_ref[pg * page:(pg + 1) * page, :] = v_refs[pg][0].astype(BF16)
    pad = jnp.zeros((LANES - t_new, kvw), F32)
    kall_ref[past:past + LANES, :] = jnp.concatenate([kn_ref[0], pad], axis=0).astype(BF16)
    vall_ref[past:past + LANES, :] = jnp.concatenate([vn_ref[0], pad], axis=0).astype(BF16)
    q = q_ref[0]
    zero = jnp.zeros((t_new, LANES), F32)
    rows = []
    for h in range(n_heads):
        g = h // n_rep
        rows.append(jnp.concatenate([q[:, h * LANES:(h + 1) * LANES] if gg == g else zero for gg in range(n_kv)], axis=1))
    qbd = jnp.concatenate(rows, axis=0).astype(BF16)
    thr = thr_ref[0]
    width = past + LANES
    sel = key_ref[0] >= _rep_lanes(thr, width // LANES)
    bias = jnp.where(sel, 0.0, NEG_BIAS)
    s = _dot_nt(qbd, kall_ref[...]) * (HEAD_DIM ** -0.5) + jnp.concatenate([bias] * n_heads, axis=0)
    m = jnp.maximum(jnp.max(s, axis=1, keepdims=True), M_FLOOR)
    p = jnp.exp(s - m)
    l = jnp.sum(p, axis=1, keepdims=True)
    o = _dot(p.astype(BF16), vall_ref[...]) / l
    for h in range(n_heads):
        g = h // n_rep
        o_ref[0, :, h * LANES:(h + 1) * LANES] = o[h * t_new:(h + 1) * t_new, g * LANES:(g + 1) * LANES]


def _dsa_sample_attn(pt_flat, q_s, keys, thr, k_new, v_new, cache_k, cache_v, *, n_pages, n_kv):
    b, t_new, d = q_s.shape
    page = cache_k.shape[1]
    kvw = n_kv * HEAD_DIM
    n_rep = d // HEAD_DIM // n_kv
    width = n_pages * page + LANES
    seq = lambda w: pl.BlockSpec((1, t_new, w), lambda s, pt: (s, 0, 0))
    page_specs = [pl.BlockSpec((1, page, kvw), lambda s, pt, pg=pg: (pt[s * n_pages + pg], 0, 0))
                  for pg in range(n_pages)]
    est = 2 * (2 * n_pages * page * kvw * 4 + t_new * (2 * d + width + LANES + 2 * kvw) * 4) \
        + 2 * width * kvw * 2 + 6 * (d // HEAD_DIM) * t_new * width * 4
    return pl.pallas_call(
        functools.partial(_dsa_sample_attn_kernel, n_pages=n_pages, page=page, t_new=t_new, n_kv=n_kv, n_rep=n_rep),
        grid_spec=pltpu.PrefetchScalarGridSpec(
            num_scalar_prefetch=1,
            grid=(b,),
            in_specs=[seq(d), seq(width), seq(LANES), seq(kvw), seq(kvw)] + page_specs + page_specs,
            out_specs=seq(d),
            scratch_shapes=[pltpu.VMEM((width, kvw), BF16), pltpu.VMEM((width, kvw), BF16)],
        ),
        out_shape=jax.ShapeDtypeStruct((b, t_new, d), F32),
        compiler_params=_cparams(("parallel",), est),
        name="dsa_sample_attn",
    )(pt_flat, q_s, keys, thr, k_new, v_new, *([cache_k] * n_pages), *([cache_v] * n_pages))


def _proj_conv_kernel(x_ref, sh_ref, sc_ref, w_ref, b_ref, v_ref, *, d_model):
    xb = (x_ref[...] * (1.0 + sc_ref[...]) + sh_ref[...]).astype(BF16)
    for c0 in range(0, d_model, 512):
        sl = slice(c0, c0 + 512)
        b_ref[:, sl] = _dot(xb, w_ref[:, c0:c0 + 512])
        cg = _dot(xb, w_ref[:, d_model + c0:d_model + c0 + 512])
        xi = _dot(xb, w_ref[:, 2 * d_model + c0:2 * d_model + c0 + 512])
        v_ref[:, sl] = cg * xi


def _proj_conv(x_all, mb, w_bf, *, npb):
    nt, d = x_all.shape
    tm = TOKEN_BLOCK
    row = pl.BlockSpec((tm, d), lambda i: (i, 0))
    est = w_bf.size * 2 + 2 * tm * d * 4 * 5 + 8 * tm * 512 * 4
    return pl.pallas_call(
        functools.partial(_proj_conv_kernel, d_model=d),
        grid=(nt // tm,),
        in_specs=[row, _mod_spec(tm, d, 0, npb), _mod_spec(tm, d, 1, npb), pl.BlockSpec(memory_space=pltpu.VMEM)],
        out_specs=(row, row),
        out_shape=(jax.ShapeDtypeStruct((nt, d), F32), jax.ShapeDtypeStruct((nt, d), F32)),
        compiler_params=_cparams(("parallel",), est),
        name="proj_conv",
    )(x_all, mb, mb, w_bf)


def _conv_kernel(v_ref, b_ref, p1_ref, p2_ref, k_ref, o_ref, *, period):
    v = v_ref[...]
    tm = v.shape[0]
    t = lax.broadcasted_iota(I32, v.shape, 0) % period
    s1 = jnp.where(t == 0, p1_ref[...], pltpu.roll(v, 1, 0))
    s2 = jnp.where(t == 0, p2_ref[...], jnp.where(t == 1, p1_ref[...], pltpu.roll(v, 2, 0)))
    conv = k_ref[0:1, :] * s2 + k_ref[1:2, :] * s1 + k_ref[2:3, :] * v
    o_ref[...] = (b_ref[...] * conv).astype(BF16)


def _conv_mix(v, bgate, p1, p2, kern, *, period, row0_blocks, n_rows, prev_per_block):
    d = v.shape[1]
    tm = TOKEN_BLOCK
    row = pl.BlockSpec((tm, d), lambda i: (i + row0_blocks, 0))
    if prev_per_block:
        prev = pl.BlockSpec((SUBLANES, d), lambda i: (i, 0))
    else:
        prev = pl.BlockSpec((tm, d), lambda i: (i, 0))
    kpad = jnp.zeros((SUBLANES, d), F32).at[:CONV_W].set(kern)
    est = 2 * tm * d * (4 * 4 + 2) + 8 * tm * d * 4
    return pl.pallas_call(
        functools.partial(_conv_kernel if not prev_per_block else _conv_kernel_blockprev, period=period),
        grid=(n_rows // tm,),
        in_specs=[row, row, prev, prev, pl.BlockSpec((SUBLANES, d), lambda i: (0, 0))],
        out_specs=pl.BlockSpec((tm, d), lambda i: (i, 0)),
        out_shape=jax.ShapeDtypeStruct((n_rows, d), BF16),
        compiler_params=_cparams(("parallel",), est),
        name="conv_mix",
    )(v, bgate, p1, p2, kpad)


def _conv_kernel_blockprev(v_ref, b_ref, p1_ref, p2_ref, k_ref, o_ref, *, period):
    v = v_ref[...]
    t = lax.broadcasted_iota(I32, v.shape, 0) % period
    p1 = jnp.broadcast_to(p1_ref[0:1, :], v.shape)
    p2 = jnp.broadcast_to(p2_ref[0:1, :], v.shape)
    s1 = jnp.where(t == 0, p1, pltpu.roll(v, 1, 0))
    s2 = jnp.where(t == 0, p2, jnp.where(t == 1, p1, pltpu.roll(v, 2, 0)))
    conv = k_ref[0:1, :] * s2 + k_ref[1:2, :] * s1 + k_ref[2:3, :] * v
    o_ref[...] = (b_ref[...] * conv).astype(BF16)


def _post_mix_kernel(a_ref, x_ref, g_ref, sh_ref, sc_ref, w_ref, lng_ref, lnb_ref, rwh_ref, rwl_ref, rb_ref,
                     x1_ref, u_ref, ti_ref, tg_ref, *, alpha):
    mix = _dot(a_ref[...], w_ref[...])
    x1 = _layer_norm_rows(alpha * x_ref[...] + g_ref[...] * mix, lng_ref[...], lnb_ref[...])
    x1_ref[...] = x1
    u = x1 * (1.0 + sc_ref[...]) + sh_ref[...]
    u_hi = u.astype(BF16)
    u_ref[...] = u_hi
    u_lo = (u - u_hi.astype(F32)).astype(BF16)
    logits = _dot(u_hi, rwh_ref[...]) + _dot(u_lo, rwh_ref[...]) + _dot(u_hi, rwl_ref[...]) + rb_ref[...]
    tm = logits.shape[0]
    lane = lax.broadcasted_iota(I32, (tm, LANES), 1)
    lane_f = lane.astype(F32)
    ti = jnp.zeros((tm, LANES), I32)
    tv = jnp.full((tm, LANES), -jnp.inf, F32)
    for r in range(TOP_K):
        m = jnp.max(logits, axis=-1, keepdims=True)
        idx = jnp.min(jnp.where(logits == m, lane_f, float(LANES)), axis=-1, keepdims=True)
        hit = lane_f == idx
        ti = jnp.where(lane == r, idx.astype(I32), ti)
        tv = jnp.where(lane == r, m, tv)
        logits = jnp.where(hit, -jnp.inf, logits)
    e = jnp.exp(tv - jnp.max(tv, axis=-1, keepdims=True))
    tg_ref[...] = e / jnp.sum(e, axis=-1, keepdims=True)
    ti_ref[...] = ti


def _post_mix(a, x_all, mb, w_bf, lng, lnb, rwh, rwl, rb, *, npb, alpha):
    nt, d = x_all.shape
    tm = TOKEN_BLOCK
    row = lambda w: pl.BlockSpec((tm, w), lambda i: (i, 0))
    vec = lambda w: pl.BlockSpec((1, w), lambda i: (0, 0))
    whole = pl.BlockSpec(memory_space=pltpu.VMEM)
    est = w_bf.size * 2 + 2 * d * LANES * 2 + 2 * tm * d * (2 + 4 * 4 + 4 + 2) + 4 * tm * LANES * 4 + 8 * tm * d * 4
    return pl.pallas_call(
        functools.partial(_post_mix_kernel, alpha=alpha),
        grid=(nt // tm,),
        in_specs=[row(d), row(d), _mod_spec(tm, d, 2, npb), _mod_spec(tm, d, 3, npb), _mod_spec(tm, d, 4, npb),
                  whole, vec(d), vec(d), whole, whole, vec(LANES)],
        out_specs=(row(d), row(d), row(LANES), row(LANES)),
        out_shape=(jax.ShapeDtypeStruct((nt, d), F32), jax.ShapeDtypeStruct((nt, d), BF16),
                   jax.ShapeDtypeStruct((nt, LANES), I32), jax.ShapeDtypeStruct((nt, LANES), F32)),
        compiler_params=_cparams(("parallel",), est),
        name="post_mix",
    )(a, x_all, mb, mb, mb, w_bf, lng, lnb, rwh, rwl, rb)


def _gather_rows_kernel(idx_ref, src_ref, dst_ref, sem, *, n_rows, window):
    def row_copy(i):
        return pltpu.make_async_copy(src_ref.at[idx_ref[i]], dst_ref.at[i], sem)

    for i in range(window):
        row_copy(i).start()

    def body(i, carry):
        row_copy(i - window).wait()
        row_copy(i).start()
        return carry

    lax.fori_loop(window, n_rows, body, 0)
    for i in range(window):
        row_copy(n_rows - window + i).wait()


def _gather_rows(src, idx):
    n, d = src.shape
    m = idx.shape[0]
    src3 = src.reshape(n, d // LANES, LANES)
    out = pl.pallas_call(
        functools.partial(_gather_rows_kernel, n_rows=m, window=GATHER_WINDOW),
        grid_spec=pltpu.PrefetchScalarGridSpec(
            num_scalar_prefetch=1,
            grid=(1,),
            in_specs=[pl.BlockSpec(memory_space=pl.ANY)],
            out_specs=pl.BlockSpec(memory_space=pl.ANY),
            scratch_shapes=[pltpu.SemaphoreType.DMA(())],
        ),
        out_shape=jax.ShapeDtypeStruct((m, d // LANES, LANES), src.dtype),
        compiler_params=pltpu.CompilerParams(dimension_semantics=("arbitrary",), has_side_effects=True),
        name="gather_rows",
    )(idx, src3)
    return out.reshape(m, d)


def _expert_kernel(ie_ref, ib_ref, nv_ref, ob_ref, *refs, n_sub, tf):
    x_refs = refs[:n_sub]
    wu_ref, bu_ref, wd_ref, bd_ref, o_ref, wub_ref, wdp_ref, wdb_ref = refs[n_sub:]
    i, j = pl.program_id(0), pl.program_id(1)
    nv = nv_ref[i]
    half = LANES // 2
    d_model = o_ref.shape[1]

    @pl.when((j == 0) & (nv > 0))
    def _():
        for r in range(n_sub):
            @pl.when(r >= nv)
            def _(r=r):
                o_ref[r * MOE_ROWS:(r + 1) * MOE_ROWS, :] = jnp.zeros((MOE_ROWS, d_model), F32)

    @pl.when(nv > 0)
    def _():
        wub_ref[...] = wu_ref[0, 0].astype(BF16)
        for c in range(d_model // LANES):
            cs = slice(c * LANES, (c + 1) * LANES)
            for qd in range(tf // LANES):
                wdp_ref[c, pl.ds(qd * LANES, half, stride=2), :] = wd_ref[0, 0, qd * LANES:qd * LANES + half, cs]
                wdp_ref[c, pl.ds(qd * LANES + 1, half, stride=2), :] = wd_ref[0, 0, qd * LANES + half:(qd + 1) * LANES, cs]
            wdb_ref[:, cs] = wdp_ref[c].astype(BF16)
        bu = bu_ref[0, 0]
        bd = bd_ref[0, 0]
        for gp in range(n_sub // 2):
            @pl.when(2 * gp < nv)
            def _(gp=gp):
                x = jnp.concatenate([x_refs[2 * gp][...], x_refs[2 * gp + 1][...]], axis=0)
                h = _dot(x, wub_ref[...]) + bu
                rows = h.shape[0]
                even = (lax.broadcasted_iota(I32, (rows, LANES), 1) % 2) == 0
                prods = []
                for t in range(2 * tf // LANES):
                    ht = h[:, t * LANES:(t + 1) * LANES]
                    gate = jnp.minimum(ht, SWIGLU_LIMIT)
                    up = jnp.clip(ht, -SWIGLU_LIMIT, SWIGLU_LIMIT)
                    act = jnp.where(even, gate * jax.nn.sigmoid(SWIGLU_ALPHA * gate), up + 1.0)
                    prods.append(act * pltpu.roll(act, LANES - 1, 1))
                comp = [jnp.where(even, prods[2 * t], pltpu.roll(prods[2 * t + 1], 1, 1))
                        for t in range(tf // LANES)]
                hc = jnp.concatenate(comp, axis=1).astype(BF16)
                y = _dot(hc, wdb_ref[...])
                sl = slice(2 * gp * MOE_ROWS, (2 * gp + 2) * MOE_ROWS)

                @pl.when(j == 0)
                def _():
                    o_ref[sl, :] = y + bd

                @pl.when(j > 0)
                def _():
                    o_ref[sl, :] = o_ref[sl, :] + y


def _experts(xs, plan, w_up, b_up, w_down, b_down, *, layer):
    item_e, item_b0, item_nv, item_ob = plan
    n_items = item_e.shape[0]
    d = xs.shape[1]
    f = w_down.shape[2]
    tf = min(MOE_FF_TILE, f)
    nj = f // tf
    n_sub = MOE_SUB
    sb = n_sub * MOE_ROWS

    def x_map(r):
        return lambda i, j, ie, ib, nv, ob: (jnp.where(r < nv[i], ib[i] + r, ib[i]), 0)

    def jj(i, j, nv):
        return jnp.where(nv[i] > 0, j, nj - 1)

    in_specs = [pl.BlockSpec((MOE_ROWS, d), x_map(r)) for r in range(n_sub)] + [
        pl.BlockSpec((1, 1, d, 2 * tf), lambda i, j, ie, ib, nv, ob: (layer, ie[i], 0, jj(i, j, nv))),
        pl.BlockSpec((1, 1, 1, 2 * tf), lambda i, j, ie, ib, nv, ob: (layer, ie[i], 0, jj(i, j, nv))),
        pl.BlockSpec((1, 1, tf, d), lambda i, j, ie, ib, nv, ob: (layer, ie[i], jj(i, j, nv), 0)),
        pl.BlockSpec((1, 1, 1, d), lambda i, j, ie, ib, nv, ob: (layer, ie[i], 0, 0)),
    ]
    est = 2 * sb * d * (2 + 4) + 2 * (d * 2 * tf + tf * d) * 4 + (d * 2 * tf + tf * d) * 2 + tf * d * 4 \
        + 12 * 2 * MOE_ROWS * 2 * tf * 4
    depth, n_exp = w_up.shape[0], w_up.shape[1]
    return pl.pallas_call(
        functools.partial(_expert_kernel, n_sub=n_sub, tf=tf),
        grid_spec=pltpu.PrefetchScalarGridSpec(
            num_scalar_prefetch=4,
            grid=(n_items, nj),
            in_specs=in_specs,
            out_specs=pl.BlockSpec((sb, d), lambda i, j, ie, ib, nv, ob: (ob[i], 0)),
            scratch_shapes=[pltpu.VMEM((d, 2 * tf), BF16), pltpu.VMEM((d // LANES, tf, LANES), F32),
                            pltpu.VMEM((tf, d), BF16)],
        ),
        out_shape=jax.ShapeDtypeStruct((n_items * sb, d), F32),
        compiler_params=_cparams(("arbitrary", "arbitrary"), est),
        name="experts",
    )(item_e, item_b0, item_nv, item_ob, *([xs] * n_sub), w_up,
      b_up.reshape(depth, n_exp, 1, 2 * f), w_down, b_down.reshape(depth, n_exp, 1, d))


def _combine_kernel(y_ref, tg_ref, x_ref, g_ref, lng_ref, lnb_ref, o_ref, *, alpha, d_model):
    tg = tg_ref[...]
    moe = jnp.zeros(x_ref.shape, F32)
    for r in range(TOP_K):
        moe = moe + tg[:, r:r + 1] * y_ref[:, r * d_model:(r + 1) * d_model]
    o_ref[...] = _layer_norm_rows(alpha * x_ref[...] + g_ref[...] * moe, lng_ref[...], lnb_ref[...])


def _combine(y4, tg, x1, mb, lng, lnb, *, npb, alpha):
    nt, d = x1.shape
    tm = TOKEN_BLOCK
    row = lambda w: pl.BlockSpec((tm, w), lambda i: (i, 0))
    vec = pl.BlockSpec((1, d), lambda i: (0, 0))
    est = 2 * tm * (TOP_K * d + LANES + 3 * d) * 4 + 6 * tm * d * 4
    return pl.pallas_call(
        functools.partial(_combine_kernel, alpha=alpha, d_model=d),
        grid=(nt // tm,),
        in_specs=[row(TOP_K * d), row(LANES), row(d), _mod_spec(tm, d, 5, npb), vec, vec],
        out_specs=row(d),
        out_shape=jax.ShapeDtypeStruct((nt, d), F32),
        compiler_params=_cparams(("parallel",), est),
        name="moe_combine",
    )(y4, tg, x1, mb, lng, lnb)


def _moe_plan(top_e, n_exp):
    nt = top_e.shape[0]
    na = nt * TOP_K
    sb = MOE_SUB * MOE_ROWS
    e_flat = top_e.reshape(-1)
    order = jnp.argsort(e_flat).astype(I32)
    e_sorted = e_flat[order]
    counts = jnp.bincount(e_flat, length=n_exp).astype(I32)
    starts = jnp.cumsum(counts) - counts
    padded = (counts + MOE_ROWS - 1) // MOE_ROWS * MOE_ROWS
    gstart = jnp.cumsum(padded) - padded
    rank = jnp.arange(na, dtype=I32) - starts[e_sorted]
    slot_sorted = gstart[e_sorted] + rank
    n_slot_blocks = na // MOE_ROWS + n_exp
    slot_tok = jnp.zeros((n_slot_blocks * MOE_ROWS,), I32).at[slot_sorted].set(order // TOP_K)
    items_e = (counts + sb - 1) // sb
    item_end = jnp.cumsum(items_e)
    item_first = item_end - items_e
    out_row_sorted = (item_first[e_sorted] + rank // sb) * sb + rank % sb
    out_row_of_assign = jnp.zeros((na,), I32).at[order].set(out_row_sorted)
    n_items = na // sb + n_exp
    ids = jnp.arange(n_items, dtype=I32)
    n_real = item_end[-1]
    is_real = ids < n_real
    ids_c = jnp.minimum(ids, n_real - 1).astype(I32)
    e_of = jnp.minimum(jnp.searchsorted(item_end, ids_c, side='right'), n_exp - 1).astype(I32)
    s_in_e = ids_c - item_first[e_of]
    rows_left = counts[e_of] - s_in_e * sb
    nvb = jnp.where(is_real, (jnp.minimum(rows_left, sb) + MOE_ROWS - 1) // MOE_ROWS, 0).astype(I32)
    blk0 = ((gstart[e_of] + s_in_e * sb) // MOE_ROWS).astype(I32)
    return slot_tok, out_row_of_assign, (e_of, blk0, nvb, ids_c)


def _rope_tables(pos, width):
    inv = ROPE_THETA ** (-jnp.arange(0, width, 2, dtype=F32) / width)
    ang = pos.astype(F32)[:, None] * inv[None, :]
    c, s = jnp.cos(ang), jnp.sin(ang)
    reps = LANES // width
    return jnp.tile(jnp.concatenate([c, c], axis=-1), (1, reps)), jnp.tile(jnp.concatenate([-s, s], axis=-1), (1, reps))


def _pad_lanes(v, fill=0.0):
    return jnp.full((1, LANES), fill, F32).at[0, :v.shape[0]].set(v)


def kernel(x_prompt, x_sample, cache_k, cache_v, cache_kidx, state_conv, page_table, c_prompt, c_sample, ada_w, ada_b, ln_g, ln_b, attn_w_in, attn_kidx_g, attn_kidx_b, attn_w_out, conv_w_in, conv_kernel, conv_w_out, router_w, router_b, expert_w_up, expert_b_up, expert_w_down, expert_b_down):
    bp, n_prompt, d = x_prompt.shape
    b, t_new, _ = x_sample.shape
    depth = ada_w.shape[0]
    n_kv = cache_k.shape[3]
    page = cache_k.shape[2]
    n_pages = page_table.shape[1]
    past = n_pages * page
    n_exp = router_w.shape[-1]
    n_sample = b * t_new
    nt = n_prompt + n_sample
    tm = TOKEN_BLOCK
    assert bp == 1 and t_new == SUBLANES and n_prompt % tm == 0 and n_sample % tm == 0 and n_exp <= LANES
    assert d % 512 == 0 and (IDX_HEADS * IDX_DIM) % 512 == 0 and n_prompt % min(KEY_CHUNK, n_prompt) == 0
    npb = n_prompt // tm
    alpha = (2.0 * depth) ** 0.25
    kvw = n_kv * HEAD_DIM
    iw = IDX_HEADS * IDX_DIM

    x_all = jnp.concatenate([x_prompt.reshape(n_prompt, d), x_sample.reshape(n_sample, d)], axis=0)
    c_all = jnp.concatenate([c_prompt, c_sample], axis=0)
    mc = -(-c_all.shape[0] // SUBLANES) * SUBLANES
    c_all = jnp.pad(c_all, ((0, mc - c_all.shape[0]), (0, 0)))
    mod = _ada_mod(c_all, ada_w, ada_b)

    pos = jnp.concatenate([jnp.arange(n_prompt, dtype=I32), jnp.tile(past + jnp.arange(t_new, dtype=I32), b)])
    tabs = _rope_tables(pos, HEAD_DIM) + _rope_tables(pos, IDX_DIM)
    pt_flat = page_table.reshape(-1).astype(I32)
    n_pool = cache_k.shape[1]

    outs = dict(kp=[], vp=[], kip=[], ks=[], vs=[], kis=[], cp=[], cs=[])
    for i in range(depth):
        mb = jnp.concatenate([jnp.broadcast_to(mod[i, 0:1], (tm, 6 * d)),
                              jnp.repeat(mod[i, 1:1 + b], t_new, axis=0)], axis=0)
        if i % 2 == 0:
            a = i // 2
            w_pad = jnp.pad(attn_w_in[a], ((0, 0), (0, LANES - IDX_DIM - IDX_HEADS))).astype(BF16)
            q, k32, v32, kb, vb, qa, qb, kw, ke = _proj_attn(
                x_all, mb, w_pad, tabs, _pad_lanes(attn_kidx_g[a]), _pad_lanes(attn_kidx_b[a]), npb=npb, n_kv=n_kv)
            topk_p = min(TOPK_MAX, n_prompt // TOPK_DIV)
            o_p = _dsa_prompt(q, qa, qb, kw, ke[:n_prompt], kb[:n_prompt], vb[:n_prompt],
                              n_prompt=n_prompt, topk=topk_p, n_kv=n_kv)
            seq3 = lambda z: z[n_prompt:].astype(F32).reshape(b, t_new, z.shape[1])
            keys = _dsa_sample_scores(pt_flat, seq3(qa), seq3(qb), seq3(kw), seq3(ke),
                                      cache_kidx[a], n_pages=n_pages)
            topk_s = min(TOPK_MAX, (past + t_new) // TOPK_DIV)
            thr = _thresholds(keys.reshape(n_sample, -1), topk=topk_s)
            o_s = _dsa_sample_attn(pt_flat, seq3(q), keys, thr.reshape(b, t_new, LANES), seq3(k32), seq3(v32),
                                   cache_k[a].reshape(n_pool, page, kvw), cache_v[a].reshape(n_pool, page, kvw),
                                   n_pages=n_pages, n_kv=n_kv)
            mix_in = jnp.concatenate([o_p, o_s.reshape(n_sample, d).astype(BF16)], axis=0)
            w_out = attn_w_out[a].astype(BF16)
            outs['kp'].append(k32[:n_prompt].reshape(1, n_prompt, n_kv, HEAD_DIM))
            outs['vp'].append(v32[:n_prompt].reshape(1, n_prompt, n_kv, HEAD_DIM))
            outs['kip'].append(kw[:n_prompt, :IDX_DIM].reshape(1, n_prompt, IDX_DIM))
            outs['ks'].append(k32[n_prompt:].reshape(b, t_new, n_kv, HEAD_DIM))
            outs['vs'].append(v32[n_prompt:].reshape(b, t_new, n_kv, HEAD_DIM))
            outs['kis'].append(kw[n_prompt:, :IDX_DIM].reshape(b, t_new, IDX_DIM))
        else:
            ci = i // 2
            bgate, v = _proj_conv(x_all, mb, conv_w_in[ci].astype(BF16), npb=npb)
            zrow = jnp.zeros((1, d), F32)
            prev1 = jnp.concatenate([zrow, v[tm - 1:n_prompt - 1:tm]], axis=0)
            prev2 = jnp.concatenate([zrow, v[tm - 2:n_prompt - 2:tm]], axis=0)
            mix_p = _conv_mix(v, bgate, jnp.repeat(prev1, SUBLANES, axis=0), jnp.repeat(prev2, SUBLANES, axis=0),
                              conv_kernel[ci], period=tm, row0_blocks=0, n_rows=n_prompt, prev_per_block=True)
            st = state_conv[ci]
            mix_s = _conv_mix(v, bgate, jnp.repeat(st[:, 1], t_new, axis=0), jnp.repeat(st[:, 0], t_new, axis=0),
                              conv_kernel[ci], period=t_new, row0_blocks=npb, n_rows=n_sample, prev_per_block=False)
            mix_in = jnp.concatenate([mix_p, mix_s], axis=0)
            w_out = conv_w_out[ci].astype(BF16)
            outs['cp'].append(v[n_prompt - (CONV_W - 1):n_prompt].reshape(1, CONV_W - 1, d))
            outs['cs'].append(v[n_prompt:].reshape(b, t_new, d)[:, t_new - (CONV_W - 1):])

        rw = jnp.pad(router_w[i], ((0, 0), (0, LANES - n_exp)))
        rwh = rw.astype(BF16)
        rwl = (rw - rwh.astype(F32)).astype(BF16)
        rb = _pad_lanes(router_b[i], fill=-1e30)
        x1, u2, ti, tg = _post_mix(mix_in, x_all, mb, w_out, ln_g[i, 0:1], ln_b[i, 0:1], rwh, rwl, rb,
                                   npb=npb, alpha=alpha)
        slot_tok, out_row_of_assign, plan = _moe_plan(ti[:, :TOP_K], n_exp)
        xs = _gather_rows(u2, slot_tok)
        ys = _experts(xs, plan, expert_w_up, expert_b_up, expert_w_down, expert_b_down, layer=i)
        y4 = _gather_rows(ys, out_row_of_assign).reshape(nt, TOP_K * d)
        x_all = _combine(y4, tg, x1, mb, ln_g[i, 1:2], ln_b[i, 1:2], npb=npb, alpha=alpha)

    y_prompt = x_all[:n_prompt].reshape(1, n_prompt, d)
    y_sample = x_all[n_prompt:].reshape(b, t_new, d)
    st = lambda name: jnp.stack(outs[name])
    return (y_prompt, y_sample, st('kp'), st('vp'), st('kip'), st('ks'), st('vs'), st('kis'), st('cp'), st('cs'))
```

```python
import functools

import jax
import jax.numpy as jnp
from jax import lax
from jax.experimental import pallas as pl
from jax.experimental.pallas import tpu as pltpu
from jax.experimental.pallas import tpu_sc as plsc

F32 = jnp.float32
BF16 = jnp.bfloat16
I32 = jnp.int32

HEAD_DIM = 128
IDX_HEADS = 16
IDX_DIM = 64
TOPK_MAX = 256
TOPK_DIV = 4
CONV_W = 3
TOP_K = 4
SWIGLU_LIMIT = 7.0
SWIGLU_ALPHA = 1.702
ROPE_THETA = 10000.0
LN_EPS = 1e-5

LANES = 128
SUBLANES = 8
SC_CORES = 2
SC_SUBCORES = 16
V7X_VMEM_BYTES = 64 * 1024 * 1024
VMEM_CAP_BYTES = V7X_VMEM_BYTES - 8 * 1024 * 1024

TOKEN_BLOCK = 256
Q_TILE = 128
KEY_CHUNK = 512
SEL_ROWS = 256
MOE_ROWS = 128
MOE_SUB = 6
MOE_FF_TILE = 256
GATHER_ROWS = 16
ADA_N_TILE = 1024

INT_MIN = -2 ** 31
KEY_NEG_INF = -2139095041
NEG_BIAS = -2e30
M_FLOOR = -1e30


def _cparams(sem, est_bytes):
    limit = int(min(max(est_bytes, 16 * 1024 * 1024), VMEM_CAP_BYTES))
    return pltpu.CompilerParams(dimension_semantics=sem, vmem_limit_bytes=limit)


def _dot(a, b):
    return jnp.dot(a, b, preferred_element_type=F32)


def _dot_nt(a, b):
    return lax.dot_general(a, b, (((1,), (1,)), ((), ())), preferred_element_type=F32)


def _rep_lanes(x, n):
    return x if n == 1 else jnp.concatenate([x] * n, axis=1)


def _layer_norm_rows(y, g, b):
    mu = jnp.mean(y, axis=-1, keepdims=True)
    d = y - mu
    var = jnp.mean(d * d, axis=-1, keepdims=True)
    return d * lax.rsqrt(var + LN_EPS) * g + b


def _sort_key(s):
    bits = lax.bitcast_convert_type(s, I32)
    return bits ^ (jnp.right_shift(bits, 31) & 0x7FFFFFFF)


def _ada_kernel(c_ref, w_ref, b_ref, o_ref):
    c = c_ref[...]
    a = (c * jax.nn.sigmoid(c)).astype(BF16)
    o_ref[0] = _dot(a, w_ref[0].astype(BF16)) + b_ref[0]


def _ada_mod(c_all, ada_w, ada_b):
    depth, d, n6 = ada_w.shape
    mc = c_all.shape[0]
    tn = ADA_N_TILE
    est = 2 * (d * tn * 4 + mc * tn * 4) + mc * d * 4 * 2 + d * tn * 2
    return pl.pallas_call(
        _ada_kernel,
        grid=(depth, n6 // tn),
        in_specs=[
            pl.BlockSpec((mc, d), lambda l, j: (0, 0)),
            pl.BlockSpec((1, d, tn), lambda l, j: (l, 0, j)),
            pl.BlockSpec((1, 1, tn), lambda l, j: (l, 0, j)),
        ],
        out_specs=pl.BlockSpec((1, mc, tn), lambda l, j: (l, 0, j)),
        out_shape=jax.ShapeDtypeStruct((depth, mc, n6), F32),
        compiler_params=_cparams(("parallel", "parallel"), est),
        name="ada_mod",
    )(c_all, ada_w, ada_b.reshape(depth, 1, n6))


def _proj_attn_kernel(x_ref, sh_ref, sc_ref, w_ref, c128_ref, s128_ref, c64_ref, s64_ref, kng_ref, knb_ref,
                      q_ref, k_ref, v_ref, kb_ref, vb_ref, qa_ref, qb_ref, kw_ref, ke_ref, *, d_model, n_kv):
    xb = (x_ref[...] * (1.0 + sc_ref[...]) + sh_ref[...]).astype(BF16)
    c128, s128 = c128_ref[...], s128_ref[...]
    c64, s64 = c64_ref[...], s64_ref[...]
    tm = xb.shape[0]
    lane = lax.broadcasted_iota(I32, (tm, LANES), 1)
    low_half = (lane % IDX_DIM) < (IDX_DIM // 2)

    def rope128(y):
        return y * c128 + pltpu.roll(y, HEAD_DIM // 2, 1) * s128

    def rope64(y):
        rot = jnp.where(low_half, pltpu.roll(y, LANES - IDX_DIM // 2, 1), pltpu.roll(y, IDX_DIM // 2, 1))
        return y * c64 + rot * s64

    kvw = n_kv * HEAD_DIM
    col = 0
    for c0 in range(0, d_model, 512):
        y = _dot(xb, w_ref[:, col + c0:col + c0 + 512])
        for t in range(4):
            q_ref[:, c0 + t * LANES:c0 + (t + 1) * LANES] = rope128(y[:, t * LANES:(t + 1) * LANES]).astype(BF16)
    col += d_model
    y = _dot(xb, w_ref[:, col:col + kvw])
    for t in range(n_kv):
        r = rope128(y[:, t * LANES:(t + 1) * LANES])
        k_ref[:, t * LANES:(t + 1) * LANES] = r
        kb_ref[:, t * LANES:(t + 1) * LANES] = r.astype(BF16)
    col += kvw
    y = _dot(xb, w_ref[:, col:col + kvw])
    v_ref[...] = y
    vb_ref[...] = y.astype(BF16)
    col += kvw
    iw = IDX_HEADS * IDX_DIM
    for c0 in range(0, iw, 512):
        y = _dot(xb, w_ref[:, col + c0:col + c0 + 512])
        for t in range(4):
            r = rope64(y[:, t * LANES:(t + 1) * LANES])
            qa_ref[:, c0 + t * LANES:c0 + (t + 1) * LANES] = r.astype(BF16)
            qb_ref[:, c0 + t * LANES:c0 + (t + 1) * LANES] = pltpu.roll(r, IDX_DIM, 1).astype(BF16)
    col += iw
    y = _dot(xb, w_ref[:, col:col + LANES])
    is_key = lane < IDX_DIM
    mu = jnp.sum(jnp.where(is_key, y, 0.0), axis=-1, keepdims=True) * (1.0 / IDX_DIM)
    dlt = jnp.where(is_key, y - mu, 0.0)
    var = jnp.sum(dlt * dlt, axis=-1, keepdims=True) * (1.0 / IDX_DIM)
    kn = dlt * lax.rsqrt(var + LN_EPS) * kng_ref[...] + knb_ref[...]
    ki = rope64(kn)
    wscale = (IDX_HEADS ** -0.5) * (IDX_DIM ** -0.5)
    wh = jnp.where((lane >= IDX_DIM) & (lane < IDX_DIM + IDX_HEADS), y * wscale, 0.0)
    kw_ref[...] = ki + wh
    ke_ref[...] = ki.astype(BF16)


def _mod_spec(tm, d, col, npb):
    return pl.BlockSpec((tm, d), lambda i: (jnp.where(i < npb, 0, i - npb + 1), col))


def _proj_attn(x_all, mb, w_pad, tabs, kng, knb, *, npb, n_kv):
    nt, d = x_all.shape
    tm = TOKEN_BLOCK
    kvw = n_kv * HEAD_DIM
    iw = IDX_HEADS * IDX_DIM
    row = lambda w: pl.BlockSpec((tm, w), lambda i: (i, 0))
    tab = pl.BlockSpec((tm, LANES), lambda i: (i, 0))
    vec = pl.BlockSpec((1, LANES), lambda i: (0, 0))
    out_shapes = (
        jax.ShapeDtypeStruct((nt, d), BF16),
        jax.ShapeDtypeStruct((nt, kvw), F32),
        jax.ShapeDtypeStruct((nt, kvw), F32),
        jax.ShapeDtypeStruct((nt, kvw), BF16),
        jax.ShapeDtypeStruct((nt, kvw), BF16),
        jax.ShapeDtypeStruct((nt, iw), BF16),
        jax.ShapeDtypeStruct((nt, iw), BF16),
        jax.ShapeDtypeStruct((nt, LANES), F32),
        jax.ShapeDtypeStruct((nt, LANES), BF16),
    )
    out_specs = (row(d), row(kvw), row(kvw), row(kvw), row(kvw), row(iw), row(iw), row(LANES), row(LANES))
    est = w_pad.size * 2 + 2 * tm * (3 * d * 4 + 4 * LANES * 4) + 2 * tm * (d * 2 + kvw * 12 + iw * 4 + LANES * 6) \
        + 8 * tm * 512 * 4
    return pl.pallas_call(
        functools.partial(_proj_attn_kernel, d_model=d, n_kv=n_kv),
        grid=(nt // tm,),
        in_specs=[row(d), _mod_spec(tm, d, 0, npb), _mod_spec(tm, d, 1, npb),
                  pl.BlockSpec(memory_space=pltpu.VMEM), tab, tab, tab, tab, vec, vec],
        out_specs=out_specs,
        out_shape=out_shapes,
        compiler_params=_cparams(("parallel",), est),
        name="proj_attn",
    )(x_all, mb, mb, w_pad, *tabs, kng, knb)


def _kth_largest_key(key_ref, n_chunks, chunk, kk):
    rows = key_ref.shape[0]

    def count_ge(cand):
        candb = jnp.broadcast_to(cand, (rows, LANES))

        def body(c, acc):
            base = pl.multiple_of(c * chunk, chunk)
            for j in range(chunk // LANES):
                blk = key_ref[:, pl.ds(base + j * LANES, LANES)]
                acc = acc + jnp.where(blk >= candb, 1.0, 0.0)
            return acc

        acc = lax.fori_loop(0, n_chunks, body, jnp.zeros((rows, LANES), F32))
        return jnp.sum(acc, axis=1, keepdims=True)

    kkf = float(kk)
    prefix = jnp.where(count_ge(jnp.zeros((rows, 1), I32)) >= kkf, 0, INT_MIN).astype(I32)

    def bit_body(b, prefix):
        cand = prefix | jnp.left_shift(jnp.int32(1), 30 - b)
        return jnp.where(count_ge(cand) >= kkf, cand, prefix)

    return lax.fori_loop(0, 31, bit_body, prefix)


def _dsa_prompt_kernel(q_ref, qa_ref, qb_ref, kw_ref, ke_ref, k_ref, v_ref, o_ref,
                       key_ref, whb_ref, m_ref, l_ref, acc_ref, *, tq, tc, topk, n_kv, n_rep):
    i = pl.program_id(0)
    q0 = i * tq
    n_ch = (q0 + tq + tc - 1) // tc
    kw = kw_ref[...]
    for h in range(IDX_HEADS):
        whb_ref[h] = jnp.broadcast_to(kw[:, IDX_DIM + h:IDX_DIM + h + 1], (tq, LANES))
    qpos = q0 + lax.broadcasted_iota(I32, (tq, LANES), 0)
    lane = lax.broadcasted_iota(I32, (tq, LANES), 1)
    n_sub = tc // LANES

    def score_body(c, carry):
        base = pl.multiple_of(c * tc, tc)
        ke = ke_ref[pl.ds(base, tc), :]
        acc = [jnp.zeros((tq, LANES), F32) for _ in range(n_sub)]
        for p in range(IDX_HEADS // 2):
            d_even = _dot_nt(qa_ref[:, p * LANES:(p + 1) * LANES], ke)
            d_odd = _dot_nt(qb_ref[:, p * LANES:(p + 1) * LANES], ke)
            w_even, w_odd = whb_ref[2 * p], whb_ref[2 * p + 1]
            for j in range(n_sub):
                sl = slice(j * LANES, (j + 1) * LANES)
                acc[j] = acc[j] + jnp.maximum(d_even[:, sl], 0.0) * w_even + jnp.maximum(d_odd[:, sl], 0.0) * w_odd
        for j in range(n_sub):
            kpos = base + j * LANES + lane
            key_ref[:, pl.ds(base + j * LANES, LANES)] = jnp.where(kpos <= qpos, _sort_key(acc[j]), INT_MIN)
        return carry

    lax.fori_loop(0, n_ch, score_body, 0)

    thr = jnp.maximum(_kth_largest_key(key_ref, n_ch, tc, topk), KEY_NEG_INF + 1)
    thrb = jnp.broadcast_to(thr, (tq, LANES))
    neg_bits = lax.bitcast_convert_type(jnp.float32(NEG_BIAS), I32)

    def bias_body(c, carry):
        base = pl.multiple_of(c * tc, tc)
        for j in range(n_sub):
            sl = pl.ds(base + j * LANES, LANES)
            key_ref[:, sl] = jnp.where(key_ref[:, sl] >= thrb, 0, neg_bits)
        return carry

    lax.fori_loop(0, n_ch, bias_body, 0)

    scale = HEAD_DIM ** -0.5
    for g in range(n_kv):
        qg = jnp.concatenate([q_ref[:, (g * n_rep + r) * LANES:(g * n_rep + r + 1) * LANES] for r in range(n_rep)], axis=0)
        m_ref[...] = jnp.full(m_ref.shape, M_FLOOR, F32)
        l_ref[...] = jnp.zeros(l_ref.shape, F32)
        acc_ref[...] = jnp.zeros(acc_ref.shape, F32)

        def att_body(c, carry, g=g, qg=qg):
            base = pl.multiple_of(c * tc, tc)
            kc = k_ref[pl.ds(base, tc), g * LANES:(g + 1) * LANES]
            vc = v_ref[pl.ds(base, tc), g * LANES:(g + 1) * LANES]
            bias = lax.bitcast_convert_type(key_ref[:, pl.ds(base, tc)], F32)
            s = _dot_nt(qg, kc) * scale + jnp.concatenate([bias] * n_rep, axis=0)
            m_old = m_ref[...]
            m_new = jnp.maximum(m_old, jnp.max(s, axis=1, keepdims=True))
            alpha = jnp.exp(m_old - m_new)
            p = jnp.exp(s - _rep_lanes(m_new, n_sub))
            l_ref[...] = alpha * l_ref[...] + jnp.sum(p, axis=1, keepdims=True)
            acc_ref[...] = alpha * acc_ref[...] + _dot(p.astype(BF16), vc)
            m_ref[...] = m_new
            return carry

        lax.fori_loop(0, n_ch, att_body, 0)
        o = acc_ref[...] / l_ref[...]
        for r in range(n_rep):
            h = g * n_rep + r
            o_ref[:, h * LANES:(h + 1) * LANES] = o[r * tq:(r + 1) * tq].astype(BF16)


def _dsa_prompt(q, qa, qb, kw, ke, kb, vb, *, n_prompt, topk, n_kv):
    d = q.shape[1]
    n_rep = d // HEAD_DIM // n_kv
    tq, tc = Q_TILE, min(KEY_CHUNK, n_prompt)
    iw = IDX_HEADS * IDX_DIM
    kvw = n_kv * HEAD_DIM
    whole = pl.BlockSpec(memory_space=pltpu.VMEM)
    row = lambda w: pl.BlockSpec((tq, w), lambda i: (i, 0))
    est = n_prompt * (LANES * 2 + kvw * 4) + tq * n_prompt * 4 + IDX_HEADS * tq * LANES * 4 \
        + 3 * n_rep * tq * LANES * 4 + 2 * tq * (2 * d * 2 + 2 * iw * 2 + LANES * 4) + 10 * n_rep * tq * tc * 4
    return pl.pallas_call(
        functools.partial(_dsa_prompt_kernel, tq=tq, tc=tc, topk=topk, n_kv=n_kv, n_rep=n_rep),
        grid=(n_prompt // tq,),
        in_specs=[row(d), row(iw), row(iw), row(LANES), whole, whole, whole],
        out_specs=row(d),
        out_shape=jax.ShapeDtypeStruct((n_prompt, d), BF16),
        scratch_shapes=[
            pltpu.VMEM((tq, n_prompt), I32),
            pltpu.VMEM((IDX_HEADS, tq, LANES), F32),
            pltpu.VMEM((n_rep * tq, LANES), F32),
            pltpu.VMEM((n_rep * tq, LANES), F32),
            pltpu.VMEM((n_rep * tq, LANES), F32),
        ],
        compiler_params=_cparams(("parallel",), est),
        name="dsa_prompt",
    )(q, qa, qb, kw, ke, kb, vb)


def _dsa_sample_score_kernel(pt_ref, qa_ref, qb_ref, kw_ref, ken_ref, *refs, n_pages, page, t_new):
    page_refs, key_ref = refs[:n_pages], refs[n_pages]
    qa = qa_ref[0]
    qb = qb_ref[0]
    kw = kw_ref[0]
    n_pair = IDX_HEADS // 2
    q_even = jnp.concatenate([qa[:, p * LANES:(p + 1) * LANES] for p in range(n_pair)], axis=0).astype(BF16)
    q_odd = jnp.concatenate([qb[:, p * LANES:(p + 1) * LANES] for p in range(n_pair)], axis=0).astype(BF16)
    w_even = [jnp.broadcast_to(kw[:, IDX_DIM + 2 * p:IDX_DIM + 2 * p + 1], (t_new, LANES)) for p in range(n_pair)]
    w_odd = [jnp.broadcast_to(kw[:, IDX_DIM + 2 * p + 1:IDX_DIM + 2 * p + 2], (t_new, LANES)) for p in range(n_pair)]

    def scores(d_even, d_odd):
        s = jnp.zeros((t_new, LANES), F32)
        for p in range(n_pair):
            s = s + jnp.maximum(d_even[p * t_new:(p + 1) * t_new], 0.0) * w_even[p] \
                  + jnp.maximum(d_odd[p * t_new:(p + 1) * t_new], 0.0) * w_odd[p]
        return s

    zeros = jnp.zeros((LANES - IDX_DIM, page), F32)
    for pg in range(n_pages):
        ket = jnp.concatenate([page_refs[pg][0], zeros], axis=0).astype(BF16)
        key_ref[0, :, pg * page:(pg + 1) * page] = _sort_key(scores(_dot(q_even, ket), _dot(q_odd, ket)))
    ke_new = jnp.concatenate([ken_ref[0], jnp.zeros((LANES - t_new, LANES), F32)], axis=0).astype(BF16)
    s_new = scores(_dot_nt(q_even, ke_new), _dot_nt(q_odd, ke_new))
    qi = lax.broadcasted_iota(I32, (t_new, LANES), 0)
    kj = lax.broadcasted_iota(I32, (t_new, LANES), 1)
    key_ref[0, :, n_pages * page:n_pages * page + LANES] = jnp.where(kj <= qi, _sort_key(s_new), INT_MIN)


def _dsa_sample_scores(pt_flat, qa_s, qb_s, kw_s, ke_new, cache_kit, *, n_pages, page0):
    b, t_new, iw = qa_s.shape
    page = cache_kit.shape[2]
    width = n_pages * page + LANES
    seq = lambda w: pl.BlockSpec((1, t_new, w), lambda s, pt: (s, 0, 0))
    page_specs = [pl.BlockSpec((1, IDX_DIM, page), lambda s, pt, pg=pg: (page0 + pt[s * n_pages + pg], 0, 0))
                  for pg in range(n_pages)]
    est = 2 * (n_pages * page * LANES * 4 + t_new * (2 * iw + 2 * LANES + width) * 4) + 64 * page * LANES * 4
    return pl.pallas_call(
        functools.partial(_dsa_sample_score_kernel, n_pages=n_pages, page=page, t_new=t_new),
        grid_spec=pltpu.PrefetchScalarGridSpec(
            num_scalar_prefetch=1,
            grid=(b,),
            in_specs=[seq(iw), seq(iw), seq(LANES), seq(LANES)] + page_specs,
            out_specs=seq(width),
        ),
        out_shape=jax.ShapeDtypeStruct((b, t_new, width), I32),
        compiler_params=_cparams(("parallel",), est),
        name="dsa_sample_scores",
    )(pt_flat, qa_s, qb_s, kw_s, ke_new, *([cache_kit] * n_pages))


def _threshold_kernel(key_ref, thr_ref, *, topk, n_tiles):
    thr = jnp.maximum(_kth_largest_key(key_ref, n_tiles, LANES, topk), KEY_NEG_INF + 1)
    thr_ref[...] = jnp.broadcast_to(thr, thr_ref.shape)


def _thresholds(keys, *, topk):
    n, width = keys.shape
    tr = min(SEL_ROWS, n)
    est = 2 * tr * (width + LANES) * 4 + 8 * tr * LANES * 4
    return pl.pallas_call(
        functools.partial(_threshold_kernel, topk=topk, n_tiles=width // LANES),
        grid=(n // tr,),
        in_specs=[pl.BlockSpec((tr, width), lambda i: (i, 0))],
        out_specs=pl.BlockSpec((tr, LANES), lambda i: (i, 0)),
        out_shape=jax.ShapeDtypeStruct((n, LANES), I32),
        compiler_params=_cparams(("parallel",), est),
        name="topk_threshold",
    )(keys)


def _dsa_sample_attn_kernel(pt_ref, q_ref, key_ref, thr_ref, kn_ref, vn_ref, *refs,
                            n_pages, page, t_new, n_kv, n_rep):
    k_refs, v_refs = refs[:n_pages], refs[n_pages:2 * n_pages]
    o_ref, kall_ref, vall_ref = refs[2 * n_pages:]
    n_heads = n_kv * n_rep
    kvw = n_kv * HEAD_DIM
    past = n_pages * page
    for pg in range(n_pages):
        for g in range(n_kv):
            rows_g = pl.ds(g, page, stride=n_kv)
            kall_ref[pg * page:(pg + 1) * page, g * LANES:(g + 1) * LANES] = k_refs[pg][0, rows_g, :].astype(BF16)
            vall_ref[pg * page:(pg + 1) * page, g * LANES:(g + 1) * LANES] = v_refs[pg][0, rows_g, :].astype(BF16)
    pad = jnp.zeros((LANES - t_new, kvw), F32)
    kall_ref[past:past + LANES, :] = jnp.concatenate([kn_ref[0], pad], axis=0).astype(BF16)
    vall_ref[past:past + LANES, :] = jnp.concatenate([vn_ref[0], pad], axis=0).astype(BF16)
    q = q_ref[0]
    zero = jnp.zeros((t_new, LANES), F32)
    rows = []
    for h in range(n_heads):
        g = h // n_rep
        rows.append(jnp.concatenate([q[:, h * LANES:(h + 1) * LANES] if gg == g else zero for gg in range(n_kv)], axis=1))
    qbd = jnp.concatenate(rows, axis=0).astype(BF16)
    thr = thr_ref[0]
    width = past + LANES
    sel = key_ref[0] >= _rep_lanes(thr, width // LANES)
    bias = jnp.where(sel, 0.0, NEG_BIAS)
    s = _dot_nt(qbd, kall_ref[...]) * (HEAD_DIM ** -0.5) + jnp.concatenate([bias] * n_heads, axis=0)
    m = jnp.maximum(jnp.max(s, axis=1, keepdims=True), M_FLOOR)
    p = jnp.exp(s - m)
    l = jnp.sum(p, axis=1, keepdims=True)
    o = _dot(p.astype(BF16), vall_ref[...]) / l
    for h in range(n_heads):
        g = h // n_rep
        o_ref[0, :, h * LANES:(h + 1) * LANES] = o[h * t_new:(h + 1) * t_new, g * LANES:(g + 1) * LANES]


def _dsa_sample_attn(pt_flat, q_s, keys, thr, k_new, v_new, cache_k, cache_v, *, n_pages, n_kv, page0):
    b, t_new, d = q_s.shape
    page = cache_k.shape[1] // n_kv
    kvw = n_kv * HEAD_DIM
    n_rep = d // HEAD_DIM // n_kv
    width = n_pages * page + LANES
    seq = lambda w: pl.BlockSpec((1, t_new, w), lambda s, pt: (s, 0, 0))
    page_specs = [pl.BlockSpec((1, page * n_kv, HEAD_DIM), lambda s, pt, pg=pg: (page0 + pt[s * n_pages + pg], 0, 0))
                  for pg in range(n_pages)]
    est = 2 * (2 * n_pages * page * kvw * 4 + t_new * (2 * d + width + LANES + 2 * kvw) * 4) \
        + 2 * width * kvw * 2 + 6 * (d // HEAD_DIM) * t_new * width * 4
    return pl.pallas_call(
        functools.partial(_dsa_sample_attn_kernel, n_pages=n_pages, page=page, t_new=t_new, n_kv=n_kv, n_rep=n_rep),
        grid_spec=pltpu.PrefetchScalarGridSpec(
            num_scalar_prefetch=1,
            grid=(b,),
            in_specs=[seq(d), seq(width), seq(LANES), seq(kvw), seq(kvw)] + page_specs + page_specs,
            out_specs=seq(d),
            scratch_shapes=[pltpu.VMEM((width, kvw), BF16), pltpu.VMEM((width, kvw), BF16)],
        ),
        out_shape=jax.ShapeDtypeStruct((b, t_new, d), F32),
        compiler_params=_cparams(("parallel",), est),
        name="dsa_sample_attn",
    )(pt_flat, q_s, keys, thr, k_new, v_new, *([cache_k] * n_pages), *([cache_v] * n_pages))


def _proj_conv_kernel(x_ref, sh_ref, sc_ref, w_ref, b_ref, v_ref, *, d_model):
    xb = (x_ref[...] * (1.0 + sc_ref[...]) + sh_ref[...]).astype(BF16)
    for c0 in range(0, d_model, 512):
        sl = slice(c0, c0 + 512)
        b_ref[:, sl] = _dot(xb, w_ref[:, c0:c0 + 512])
        cg = _dot(xb, w_ref[:, d_model + c0:d_model + c0 + 512])
        xi = _dot(xb, w_ref[:, 2 * d_model + c0:2 * d_model + c0 + 512])
        v_ref[:, sl] = cg * xi


def _proj_conv(x_all, mb, w_bf, *, npb):
    nt, d = x_all.shape
    tm = TOKEN_BLOCK
    row = pl.BlockSpec((tm, d), lambda i: (i, 0))
    est = w_bf.size * 2 + 2 * tm * d * 4 * 5 + 8 * tm * 512 * 4
    return pl.pallas_call(
        functools.partial(_proj_conv_kernel, d_model=d),
        grid=(nt // tm,),
        in_specs=[row, _mod_spec(tm, d, 0, npb), _mod_spec(tm, d, 1, npb), pl.BlockSpec(memory_space=pltpu.VMEM)],
        out_specs=(row, row),
        out_shape=(jax.ShapeDtypeStruct((nt, d), F32), jax.ShapeDtypeStruct((nt, d), F32)),
        compiler_params=_cparams(("parallel",), est),
        name="proj_conv",
    )(x_all, mb, mb, w_bf)


def _conv_kernel(v_ref, b_ref, p1_ref, p2_ref, k_ref, o_ref, *, period):
    v = v_ref[...]
    tm = v.shape[0]
    t = lax.broadcasted_iota(I32, v.shape, 0) % period
    s1 = jnp.where(t == 0, p1_ref[...], pltpu.roll(v, 1, 0))
    s2 = jnp.where(t == 0, p2_ref[...], jnp.where(t == 1, p1_ref[...], pltpu.roll(v, 2, 0)))
    conv = k_ref[0:1, :] * s2 + k_ref[1:2, :] * s1 + k_ref[2:3, :] * v
    o_ref[...] = (b_ref[...] * conv).astype(BF16)


def _conv_mix(v, bgate, p1, p2, kern, *, period, row0_blocks, n_rows, prev_per_block):
    d = v.shape[1]
    tm = TOKEN_BLOCK
    row = pl.BlockSpec((tm, d), lambda i: (i + row0_blocks, 0))
    if prev_per_block:
        prev = pl.BlockSpec((SUBLANES, d), lambda i: (i, 0))
    else:
        prev = pl.BlockSpec((tm, d), lambda i: (i, 0))
    kpad = jnp.zeros((SUBLANES, d), F32).at[:CONV_W].set(kern)
    est = 2 * tm * d * (4 * 4 + 2) + 8 * tm * d * 4
    return pl.pallas_call(
        functools.partial(_conv_kernel if not prev_per_block else _conv_kernel_blockprev, period=period),
        grid=(n_rows // tm,),
        in_specs=[row, row, prev, prev, pl.BlockSpec((SUBLANES, d), lambda i: (0, 0))],
        out_specs=pl.BlockSpec((tm, d), lambda i: (i, 0)),
        out_shape=jax.ShapeDtypeStruct((n_rows, d), BF16),
        compiler_params=_cparams(("parallel",), est),
        name="conv_mix",
    )(v, bgate, p1, p2, kpad)


def _conv_kernel_blockprev(v_ref, b_ref, p1_ref, p2_ref, k_ref, o_ref, *, period):
    v = v_ref[...]
    t = lax.broadcasted_iota(I32, v.shape, 0) % period
    p1 = jnp.broadcast_to(p1_ref[0:1, :], v.shape)
    p2 = jnp.broadcast_to(p2_ref[0:1, :], v.shape)
    s1 = jnp.where(t == 0, p1, pltpu.roll(v, 1, 0))
    s2 = jnp.where(t == 0, p2, jnp.where(t == 1, p1, pltpu.roll(v, 2, 0)))
    conv = k_ref[0:1, :] * s2 + k_ref[1:2, :] * s1 + k_ref[2:3, :] * v
    o_ref[...] = (b_ref[...] * conv).astype(BF16)


def _post_mix_kernel(a_ref, x_ref, g_ref, sh_ref, sc_ref, w_ref, lng_ref, lnb_ref, rwh_ref, rwl_ref, rb_ref,
                     x1_ref, u_ref, ti_ref, tg_ref, *, alpha):
    mix = _dot(a_ref[...], w_ref[...])
    x1 = _layer_norm_rows(alpha * x_ref[...] + g_ref[...] * mix, lng_ref[...], lnb_ref[...])
    x1_ref[...] = x1
    u = x1 * (1.0 + sc_ref[...]) + sh_ref[...]
    u_ref[...] = u
    u_hi = u.astype(BF16)
    u_lo = (u - u_hi.astype(F32)).astype(BF16)
    logits = _dot(u_hi, rwh_ref[...]) + _dot(u_lo, rwh_ref[...]) + _dot(u_hi, rwl_ref[...]) + rb_ref[...]
    tm = logits.shape[0]
    lane = lax.broadcasted_iota(I32, (tm, LANES), 1)
    lane_f = lane.astype(F32)
    ti = jnp.zeros((tm, LANES), I32)
    tv = jnp.full((tm, LANES), -jnp.inf, F32)
    for r in range(TOP_K):
        m = jnp.max(logits, axis=-1, keepdims=True)
        idx = jnp.min(jnp.where(logits == m, lane_f, float(LANES)), axis=-1, keepdims=True)
        hit = lane_f == idx
        ti = jnp.where(lane == r, idx.astype(I32), ti)
        tv = jnp.where(lane == r, m, tv)
        logits = jnp.where(hit, -jnp.inf, logits)
    e = jnp.exp(tv - jnp.max(tv, axis=-1, keepdims=True))
    tg_ref[...] = e / jnp.sum(e, axis=-1, keepdims=True)
    ti_ref[...] = ti


def _post_mix(a, x_all, mb, w_bf, lng, lnb, rwh, rwl, rb, *, npb, alpha):
    nt, d = x_all.shape
    tm = TOKEN_BLOCK
    row = lambda w: pl.BlockSpec((tm, w), lambda i: (i, 0))
    vec = lambda w: pl.BlockSpec((1, w), lambda i: (0, 0))
    whole = pl.BlockSpec(memory_space=pltpu.VMEM)
    est = w_bf.size * 2 + 2 * d * LANES * 2 + 2 * tm * d * (2 + 4 * 4 + 4 + 2) + 4 * tm * LANES * 4 + 8 * tm * d * 4
    return pl.pallas_call(
        functools.partial(_post_mix_kernel, alpha=alpha),
        grid=(nt // tm,),
        in_specs=[row(d), row(d), _mod_spec(tm, d, 2, npb), _mod_spec(tm, d, 3, npb), _mod_spec(tm, d, 4, npb),
                  whole, vec(d), vec(d), whole, whole, vec(LANES)],
        out_specs=(row(d), row(d), row(LANES), row(LANES)),
        out_shape=(jax.ShapeDtypeStruct((nt, d), F32), jax.ShapeDtypeStruct((nt, d), F32),
                   jax.ShapeDtypeStruct((nt, LANES), I32), jax.ShapeDtypeStruct((nt, LANES), F32)),
        compiler_params=_cparams(("parallel",), est),
        name="post_mix",
    )(a, x_all, mb, mb, mb, w_bf, lng, lnb, rwh, rwl, rb)


def _gather_rows(src, idx):
    n, d = src.shape
    m = idx.shape[0]
    n_workers = SC_CORES * SC_SUBCORES
    per_w = m // n_workers
    rows = GATHER_ROWS
    assert m % n_workers == 0 and per_w % (2 * rows) == 0
    mesh = plsc.VectorSubcoreMesh(core_axis_name="core", subcore_axis_name="subcore")

    @functools.partial(
        pl.kernel, out_type=jax.ShapeDtypeStruct((m, d), src.dtype), mesh=mesh, name="gather_rows",
        scratch_types=[pltpu.VMEM((per_w,), I32), pltpu.VMEM((rows, d), src.dtype), pltpu.VMEM((rows, d), src.dtype),
                       pltpu.SemaphoreType.DMA, pltpu.SemaphoreType.DMA])
    def gather(src_hbm, idx_hbm, dst_hbm, idx_v, buf0, buf1, sem0, sem1):
        base = (lax.axis_index("subcore") * SC_CORES + lax.axis_index("core")) * per_w
        pltpu.sync_copy(idx_hbm.at[pl.ds(base, per_w)], idx_v)

        @pl.loop(0, per_w, step=2 * rows)
        def _(off):
            g0 = pltpu.async_copy(src_hbm.at[idx_v.at[pl.ds(off, rows)]], buf0, sem0)
            g1 = pltpu.async_copy(src_hbm.at[idx_v.at[pl.ds(off + rows, rows)]], buf1, sem1)
            g0.wait()
            w0 = pltpu.async_copy(buf0, dst_hbm.at[pl.ds(base + off, rows)], sem0)
            g1.wait()
            w1 = pltpu.async_copy(buf1, dst_hbm.at[pl.ds(base + off + rows, rows)], sem1)
            w0.wait()
            w1.wait()

    return gather(src, idx)


def _expert_kernel(ie_ref, ib_ref, nv_ref, ob_ref, *refs, n_sub, tf):
    x_refs = refs[:n_sub]
    wu_ref, bu_ref, wd_ref, bd_ref, o_ref, xb_ref, wub_ref, wdp_ref, wdb_ref = refs[n_sub:]
    i, j = pl.program_id(0), pl.program_id(1)
    nv = nv_ref[i]
    half = LANES // 2
    d_model = o_ref.shape[1]

    @pl.when((j == 0) & (nv > 0))
    def _():
        for r in range(n_sub):
            rs = slice(r * MOE_ROWS, (r + 1) * MOE_ROWS)

            @pl.when(r < nv)
            def _(r=r, rs=rs):
                xb_ref[rs, :] = x_refs[r][...].astype(BF16)

            @pl.when(r >= nv)
            def _(rs=rs):
                xb_ref[rs, :] = jnp.zeros((MOE_ROWS, d_model), BF16)
                o_ref[rs, :] = jnp.zeros((MOE_ROWS, d_model), F32)

    @pl.when(nv > 0)
    def _():
        wub_ref[...] = wu_ref[0, 0].astype(BF16)
        for c in range(d_model // LANES):
            cs = slice(c * LANES, (c + 1) * LANES)
            for qd in range(tf // LANES):
                wdp_ref[c, pl.ds(qd * LANES, half, stride=2), :] = wd_ref[0, 0, qd * LANES:qd * LANES + half, cs]
                wdp_ref[c, pl.ds(qd * LANES + 1, half, stride=2), :] = wd_ref[0, 0, qd * LANES + half:(qd + 1) * LANES, cs]
            wdb_ref[:, cs] = wdp_ref[c].astype(BF16)
        bu = bu_ref[0, 0]
        bd = bd_ref[0, 0]
        for gp in range(n_sub // 2):
            @pl.when(2 * gp < nv)
            def _(gp=gp):
                sl = slice(2 * gp * MOE_ROWS, (2 * gp + 2) * MOE_ROWS)
                h = _dot(xb_ref[sl, :], wub_ref[...]) + bu
                rows = h.shape[0]
                even = (lax.broadcasted_iota(I32, (rows, LANES), 1) % 2) == 0
                prods = []
                for t in range(2 * tf // LANES):
                    ht = h[:, t * LANES:(t + 1) * LANES]
                    gate = jnp.minimum(ht, SWIGLU_LIMIT)
                    up = jnp.clip(ht, -SWIGLU_LIMIT, SWIGLU_LIMIT)
                    act = jnp.where(even, gate * jax.nn.sigmoid(SWIGLU_ALPHA * gate), up + 1.0)
                    prods.append(act * pltpu.roll(act, LANES - 1, 1))
                comp = [jnp.where(even, prods[2 * t], pltpu.roll(prods[2 * t + 1], 1, 1))
                        for t in range(tf // LANES)]
                hc = jnp.concatenate(comp, axis=1).astype(BF16)
                y = _dot(hc, wdb_ref[...])

                @pl.when(j == 0)
                def _():
                    o_ref[sl, :] = y + bd

                @pl.when(j > 0)
                def _():
                    o_ref[sl, :] = o_ref[sl, :] + y


def _experts(xs, plan, w_up, b_up, w_down, b_down, *, layer):
    item_e, item_b0, item_nv, item_ob = plan
    n_items = item_e.shape[0]
    d = xs.shape[1]
    f = w_down.shape[2]
    tf = min(MOE_FF_TILE, f)
    nj = f // tf
    n_sub = MOE_SUB
    sb = n_sub * MOE_ROWS

    def x_map(r):
        return lambda i, j, ie, ib, nv, ob: (jnp.where(r < nv[i], ib[i] + r, ib[i]), 0)

    def jj(i, j, nv):
        return jnp.where(nv[i] > 0, j, nj - 1)

    in_specs = [pl.BlockSpec((MOE_ROWS, d), x_map(r)) for r in range(n_sub)] + [
        pl.BlockSpec((1, 1, d, 2 * tf), lambda i, j, ie, ib, nv, ob: (layer, ie[i], 0, jj(i, j, nv))),
        pl.BlockSpec((1, 1, 1, 2 * tf), lambda i, j, ie, ib, nv, ob: (layer, ie[i], 0, jj(i, j, nv))),
        pl.BlockSpec((1, 1, tf, d), lambda i, j, ie, ib, nv, ob: (layer, ie[i], jj(i, j, nv), 0)),
        pl.BlockSpec((1, 1, 1, d), lambda i, j, ie, ib, nv, ob: (layer, ie[i], 0, 0)),
    ]
    est = sb * d * (2 * 4 + 2 + 2 * 4) + 2 * (d * 2 * tf + tf * d) * 4 + (d * 2 * tf + tf * d) * 2 + tf * d * 4 \
        + 12 * 2 * MOE_ROWS * 2 * tf * 4
    depth, n_exp = w_up.shape[0], w_up.shape[1]
    return pl.pallas_call(
        functools.partial(_expert_kernel, n_sub=n_sub, tf=tf),
        grid_spec=pltpu.PrefetchScalarGridSpec(
            num_scalar_prefetch=4,
            grid=(n_items, nj),
            in_specs=in_specs,
            out_specs=pl.BlockSpec((sb, d), lambda i, j, ie, ib, nv, ob: (ob[i], 0)),
            scratch_shapes=[pltpu.VMEM((sb, d), BF16), pltpu.VMEM((d, 2 * tf), BF16),
                            pltpu.VMEM((d // LANES, tf, LANES), F32), pltpu.VMEM((tf, d), BF16)],
        ),
        out_shape=jax.ShapeDtypeStruct((n_items * sb, d), F32),
        compiler_params=_cparams(("arbitrary", "arbitrary"), est),
        name="experts",
    )(item_e, item_b0, item_nv, item_ob, *([xs] * n_sub), w_up,
      b_up.reshape(depth, n_exp, 1, 2 * f), w_down, b_down.reshape(depth, n_exp, 1, d))


def _combine_kernel(*refs, alpha):
    y_refs = refs[:TOP_K]
    tg_ref, x_ref, g_ref, lng_ref, lnb_ref, o_ref = refs[TOP_K:]
    tg = tg_ref[...]
    moe = jnp.zeros(x_ref.shape, F32)
    for r in range(TOP_K):
        moe = moe + tg[:, r:r + 1] * y_refs[r][...]
    o_ref[...] = _layer_norm_rows(alpha * x_ref[...] + g_ref[...] * moe, lng_ref[...], lnb_ref[...])


def _combine(y4, tg, x1, mb, lng, lnb, *, npb, alpha):
    nt, d = x1.shape
    tm = TOKEN_BLOCK
    nbt = nt // tm
    row = lambda w: pl.BlockSpec((tm, w), lambda i: (i, 0))
    vec = pl.BlockSpec((1, d), lambda i: (0, 0))
    y_specs = [pl.BlockSpec((tm, d), lambda i, r=r: (r * nbt + i, 0)) for r in range(TOP_K)]
    est = 2 * tm * (TOP_K * d + LANES + 3 * d) * 4 + 6 * tm * d * 4
    return pl.pallas_call(
        functools.partial(_combine_kernel, alpha=alpha),
        grid=(nbt,),
        in_specs=y_specs + [row(LANES), row(d), _mod_spec(tm, d, 5, npb), vec, vec],
        out_specs=row(d),
        out_shape=jax.ShapeDtypeStruct((nt, d), F32),
        compiler_params=_cparams(("parallel",), est),
        name="moe_combine",
    )(*([y4] * TOP_K), tg, x1, mb, lng, lnb)


def _moe_plan(top_e, n_exp):
    nt = top_e.shape[0]
    na = nt * TOP_K
    sb = MOE_SUB * MOE_ROWS
    e_flat = top_e.reshape(-1)
    order = jnp.argsort(e_flat).astype(I32)
    e_sorted = e_flat[order]
    counts = jnp.bincount(e_flat, length=n_exp).astype(I32)
    starts = jnp.cumsum(counts) - counts
    padded = (counts + MOE_ROWS - 1) // MOE_ROWS * MOE_ROWS
    gstart = jnp.cumsum(padded) - padded
    rank = jnp.arange(na, dtype=I32) - starts[e_sorted]
    slot_sorted = gstart[e_sorted] + rank
    n_slot_blocks = na // MOE_ROWS + n_exp
    slot_tok = jnp.zeros((n_slot_blocks * MOE_ROWS,), I32).at[slot_sorted].set(order // TOP_K)
    items_e = (counts + sb - 1) // sb
    item_end = jnp.cumsum(items_e)
    item_first = item_end - items_e
    out_row_sorted = (item_first[e_sorted] + rank // sb) * sb + rank % sb
    out_row_of_assign = jnp.zeros((na,), I32).at[order].set(out_row_sorted)
    n_items = na // sb + n_exp
    ids = jnp.arange(n_items, dtype=I32)
    n_real = item_end[-1]
    is_real = ids < n_real
    ids_c = jnp.minimum(ids, n_real - 1).astype(I32)
    e_of = jnp.minimum(jnp.searchsorted(item_end, ids_c, side='right'), n_exp - 1).astype(I32)
    s_in_e = ids_c - item_first[e_of]
    rows_left = counts[e_of] - s_in_e * sb
    nvb = jnp.where(is_real, (jnp.minimum(rows_left, sb) + MOE_ROWS - 1) // MOE_ROWS, 0).astype(I32)
    blk0 = ((gstart[e_of] + s_in_e * sb) // MOE_ROWS).astype(I32)
    return slot_tok, out_row_of_assign, (e_of, blk0, nvb, ids_c)


def _rope_tables(pos, width):
    inv = ROPE_THETA ** (-jnp.arange(0, width, 2, dtype=F32) / width)
    ang = pos.astype(F32)[:, None] * inv[None, :]
    c, s = jnp.cos(ang), jnp.sin(ang)
    reps = LANES // width
    return jnp.tile(jnp.concatenate([c, c], axis=-1), (1, reps)), jnp.tile(jnp.concatenate([-s, s], axis=-1), (1, reps))


def _pad_lanes(v, fill=0.0):
    return jnp.full((1, LANES), fill, F32).at[0, :v.shape[0]].set(v)


def kernel(x_prompt, x_sample, cache_k, cache_v, cache_kidx, state_conv, page_table, c_prompt, c_sample, ada_w, ada_b, ln_g, ln_b, attn_w_in, attn_kidx_g, attn_kidx_b, attn_w_out, conv_w_in, conv_kernel, conv_w_out, router_w, router_b, expert_w_up, expert_b_up, expert_w_down, expert_b_down):
    bp, n_prompt, d = x_prompt.shape
    b, t_new, _ = x_sample.shape
    depth = ada_w.shape[0]
    n_kv = cache_k.shape[3]
    page = cache_k.shape[2]
    n_pages = page_table.shape[1]
    past = n_pages * page
    n_exp = router_w.shape[-1]
    n_sample = b * t_new
    nt = n_prompt + n_sample
    tm = TOKEN_BLOCK
    assert bp == 1 and t_new == SUBLANES and n_prompt % tm == 0 and n_sample % tm == 0 and n_exp <= LANES
    assert d % 512 == 0 and (IDX_HEADS * IDX_DIM) % 512 == 0 and n_prompt % min(KEY_CHUNK, n_prompt) == 0
    npb = n_prompt // tm
    alpha = (2.0 * depth) ** 0.25
    kvw = n_kv * HEAD_DIM
    iw = IDX_HEADS * IDX_DIM

    x_all = jnp.concatenate([x_prompt.reshape(n_prompt, d), x_sample.reshape(n_sample, d)], axis=0)
    c_all = jnp.concatenate([c_prompt, c_sample], axis=0)
    mc = -(-c_all.shape[0] // SUBLANES) * SUBLANES
    c_all = jnp.pad(c_all, ((0, mc - c_all.shape[0]), (0, 0)))
    mod = _ada_mod(c_all, ada_w, ada_b)

    pos = jnp.concatenate([jnp.arange(n_prompt, dtype=I32), jnp.tile(past + jnp.arange(t_new, dtype=I32), b)])
    tabs = _rope_tables(pos, HEAD_DIM) + _rope_tables(pos, IDX_DIM)
    pt_flat = page_table.reshape(-1).astype(I32)
    n_pool = cache_k.shape[1]
    n_cache = cache_k.shape[0] * n_pool
    cache_k2 = cache_k.reshape(n_cache, page * n_kv, HEAD_DIM)
    cache_v2 = cache_v.reshape(n_cache, page * n_kv, HEAD_DIM)
    cache_kit = jnp.swapaxes(cache_kidx.reshape(n_cache, page, IDX_DIM), 1, 2)

    outs = dict(kp=[], vp=[], kip=[], ks=[], vs=[], kis=[], cp=[], cs=[])
    for i in range(depth):
        mb = jnp.concatenate([jnp.broadcast_to(mod[i, 0:1], (tm, 6 * d)),
                              jnp.repeat(mod[i, 1:1 + b], t_new, axis=0)], axis=0)
        if i % 2 == 0:
            a = i // 2
            w_pad = jnp.pad(attn_w_in[a], ((0, 0), (0, LANES - IDX_DIM - IDX_HEADS))).astype(BF16)
            q, k32, v32, kb, vb, qa, qb, kw, ke = _proj_attn(
                x_all, mb, w_pad, tabs, _pad_lanes(attn_kidx_g[a]), _pad_lanes(attn_kidx_b[a]), npb=npb, n_kv=n_kv)
            topk_p = min(TOPK_MAX, n_prompt // TOPK_DIV)
            o_p = _dsa_prompt(q, qa, qb, kw, ke[:n_prompt], kb[:n_prompt], vb[:n_prompt],
                              n_prompt=n_prompt, topk=topk_p, n_kv=n_kv)
            seq3 = lambda z: z[n_prompt:].astype(F32).reshape(b, t_new, z.shape[1])
            keys = _dsa_sample_scores(pt_flat, seq3(qa), seq3(qb), seq3(kw), seq3(ke),
                                      cache_kit, n_pages=n_pages, page0=a * n_pool)
            topk_s = min(TOPK_MAX, (past + t_new) // TOPK_DIV)
            thr = _thresholds(keys.reshape(n_sample, -1), topk=topk_s)
            o_s = _dsa_sample_attn(pt_flat, seq3(q), keys, thr.reshape(b, t_new, LANES), seq3(k32), seq3(v32),
                                   cache_k2, cache_v2, n_pages=n_pages, n_kv=n_kv, page0=a * n_pool)
            mix_in = jnp.concatenate([o_p, o_s.reshape(n_sample, d).astype(BF16)], axis=0)
            w_out = attn_w_out[a].astype(BF16)
            outs['kp'].append(k32[:n_prompt].reshape(1, n_prompt, n_kv, HEAD_DIM))
            outs['vp'].append(v32[:n_prompt].reshape(1, n_prompt, n_kv, HEAD_DIM))
            outs['kip'].append(kw[:n_prompt, :IDX_DIM].reshape(1, n_prompt, IDX_DIM))
            outs['ks'].append(k32[n_prompt:].reshape(b, t_new, n_kv, HEAD_DIM))
            outs['vs'].append(v32[n_prompt:].reshape(b, t_new, n_kv, HEAD_DIM))
            outs['kis'].append(kw[n_prompt:, :IDX_DIM].reshape(b, t_new, IDX_DIM))
        else:
            ci = i // 2
            bgate, v = _proj_conv(x_all, mb, conv_w_in[ci].astype(BF16), npb=npb)
            zrow = jnp.zeros((1, d), F32)
            prev1 = jnp.concatenate([zrow, v[tm - 1:n_prompt - 1:tm]], axis=0)
            prev2 = jnp.concatenate([zrow, v[tm - 2:n_prompt - 2:tm]], axis=0)
            mix_p = _conv_mix(v, bgate, jnp.repeat(prev1, SUBLANES, axis=0), jnp.repeat(prev2, SUBLANES, axis=0),
                              conv_kernel[ci], period=tm, row0_blocks=0, n_rows=n_prompt, prev_per_block=True)
            st = state_conv[ci]
            mix_s = _conv_mix(v, bgate, jnp.repeat(st[:, 1], t_new, axis=0), jnp.repeat(st[:, 0], t_new, axis=0),
                              conv_kernel[ci], period=t_new, row0_blocks=npb, n_rows=n_sample, prev_per_block=False)
            mix_in = jnp.concatenate([mix_p, mix_s], axis=0)
            w_out = conv_w_out[ci].astype(BF16)
            outs['cp'].append(v[n_prompt - (CONV_W - 1):n_prompt].reshape(1, CONV_W - 1, d))
            outs['cs'].append(v[n_prompt:].reshape(b, t_new, d)[:, t_new - (CONV_W - 1):])

        rw = jnp.pad(router_w[i], ((0, 0), (0, LANES - n_exp)))
        rwh = rw.astype(BF16)
        rwl = (rw - rwh.astype(F32)).astype(BF16)
        rb = _pad_lanes(router_b[i], fill=-1e30)
        x1, u2, ti, tg = _post_mix(mix_in, x_all, mb, w_out, ln_g[i, 0:1], ln_b[i, 0:1], rwh, rwl, rb,
                                   npb=npb, alpha=alpha)
        slot_tok, out_row_of_assign, plan = _moe_plan(ti[:, :TOP_K], n_exp)
        xs = _gather_rows(u2, slot_tok)
        ys = _experts(xs, plan, expert_w_up, expert_b_up, expert_w_down, expert_b_down, layer=i)
        y4 = _gather_rows(ys, out_row_of_assign.reshape(nt, TOP_K).T.reshape(-1))
        x_all = _combine(y4, tg, x1, mb, ln_g[i, 1:2], ln_b[i, 1:2], npb=npb, alpha=alpha)

    y_prompt = x_all[:n_prompt].reshape(1, n_prompt, d)
    y_sample = x_all[n_prompt:].reshape(b, t_new, d)
    st = lambda name: jnp.stack(outs[name])
    return (y_prompt, y_sample, st('kp'), st('vp'), st('kip'), st('ks'), st('vs'), st('kis'), st('cp'), st('cs'))
```

```python
import functools

import jax
import jax.numpy as jnp
from jax import lax
from jax.experimental import pallas as pl
from jax.experimental.pallas import tpu as pltpu
from jax.experimental.pallas import tpu_sc as plsc

F32 = jnp.float32
BF16 = jnp.bfloat16
I32 = jnp.int32

HEAD_DIM = 128
IDX_HEADS = 16
IDX_DIM = 64
TOPK_MAX = 256
TOPK_DIV = 4
CONV_W = 3
TOP_K = 4
SWIGLU_LIMIT = 7.0
SWIGLU_ALPHA = 1.702
ROPE_THETA = 10000.0
LN_EPS = 1e-5

LANES = 128
SUBLANES = 8
SC_CORES = 2
SC_SUBCORES = 16
V7X_VMEM_BYTES = 64 * 1024 * 1024
VMEM_CAP_BYTES = V7X_VMEM_BYTES - 8 * 1024 * 1024

TOKEN_BLOCK = 256
Q_TILE = 128
KEY_CHUNK = 512
SEL_ROWS = 256
MOE_ROWS = 128
MOE_SUB = 10
MOE_FF_TILE = 256
MOE_OUT_TILE = 512
GATHER_BUF_BYTES = 128 * 1024
ADA_N_TILE = 1024

INT_MIN = -2 ** 31
HI16_MASK = -65536
QK_SCALE_LOG2 = (HEAD_DIM ** -0.5) * 1.4426950408889634
KEY_NEG_INF = -2139095041
NEG_BIAS = -2e30
M_FLOOR = -1e30


def _cparams(sem, est_bytes):
    limit = int(min(max(est_bytes, 16 * 1024 * 1024), VMEM_CAP_BYTES))
    return pltpu.CompilerParams(dimension_semantics=sem, vmem_limit_bytes=limit)


def _dot(a, b):
    return jnp.dot(a, b, preferred_element_type=F32)


def _dot_nt(a, b):
    return lax.dot_general(a, b, (((1,), (1,)), ((), ())), preferred_element_type=F32)


def _rep_lanes(x, n):
    return x if n == 1 else jnp.concatenate([x] * n, axis=1)


def _layer_norm_rows(y, g, b):
    mu = jnp.mean(y, axis=-1, keepdims=True)
    d = y - mu
    var = jnp.mean(d * d, axis=-1, keepdims=True)
    return d * lax.rsqrt(var + LN_EPS) * g + b


def _sort_key(s):
    bits = lax.bitcast_convert_type(s, I32)
    return bits ^ (jnp.right_shift(bits, 31) & 0x7FFFFFFF)


def _ada_kernel(c_ref, w_ref, b_ref, o_ref):
    c = c_ref[...]
    a = (c * jax.nn.sigmoid(c)).astype(BF16)
    o_ref[0] = _dot(a, w_ref[0].astype(BF16)) + b_ref[0]


def _ada_mod(c_all, ada_w, ada_b):
    depth, d, n6 = ada_w.shape
    mc = c_all.shape[0]
    tn = ADA_N_TILE
    est = 2 * (d * tn * 4 + mc * tn * 4) + mc * d * 4 * 2 + d * tn * 2
    return pl.pallas_call(
        _ada_kernel,
        grid=(depth, n6 // tn),
        in_specs=[
            pl.BlockSpec((mc, d), lambda l, j: (0, 0)),
            pl.BlockSpec((1, d, tn), lambda l, j: (l, 0, j)),
            pl.BlockSpec((1, 1, tn), lambda l, j: (l, 0, j)),
        ],
        out_specs=pl.BlockSpec((1, mc, tn), lambda l, j: (l, 0, j)),
        out_shape=jax.ShapeDtypeStruct((depth, mc, n6), F32),
        compiler_params=_cparams(("parallel", "parallel"), est),
        name="ada_mod",
    )(c_all, ada_w, ada_b.reshape(depth, 1, n6))


def _proj_attn_kernel(x_ref, sh_ref, sc_ref, w_ref, c128_ref, s128_ref, c64_ref, s64_ref, kng_ref, knb_ref,
                      q_ref, k_ref, v_ref, kb_ref, vb_ref, qa_ref, qb_ref, kw_ref, ke_ref, *, d_model, n_kv):
    xb = (x_ref[...] * (1.0 + sc_ref[...]) + sh_ref[...]).astype(BF16)
    c128, s128 = c128_ref[...], s128_ref[...]
    c64, s64 = c64_ref[...], s64_ref[...]
    tm = xb.shape[0]
    lane = lax.broadcasted_iota(I32, (tm, LANES), 1)
    low_half = (lane % IDX_DIM) < (IDX_DIM // 2)

    def rope128(y):
        return y * c128 + pltpu.roll(y, HEAD_DIM // 2, 1) * s128

    def rope64(y):
        rot = jnp.where(low_half, pltpu.roll(y, LANES - IDX_DIM // 2, 1), pltpu.roll(y, IDX_DIM // 2, 1))
        return y * c64 + rot * s64

    kvw = n_kv * HEAD_DIM
    col = 0
    for c0 in range(0, d_model, 512):
        y = _dot(xb, w_ref[:, col + c0:col + c0 + 512])
        for t in range(4):
            r = rope128(y[:, t * LANES:(t + 1) * LANES]) * QK_SCALE_LOG2
            q_ref[:, c0 + t * LANES:c0 + (t + 1) * LANES] = r.astype(BF16)
    col += d_model
    y = _dot(xb, w_ref[:, col:col + kvw])
    for t in range(n_kv):
        r = rope128(y[:, t * LANES:(t + 1) * LANES])
        k_ref[:, t * LANES:(t + 1) * LANES] = r
        kb_ref[:, t * LANES:(t + 1) * LANES] = r.astype(BF16)
    col += kvw
    y = _dot(xb, w_ref[:, col:col + kvw])
    v_ref[...] = y
    vb_ref[...] = y.astype(BF16)
    col += kvw
    iw = IDX_HEADS * IDX_DIM
    for c0 in range(0, iw, 512):
        y = _dot(xb, w_ref[:, col + c0:col + c0 + 512])
        for t in range(4):
            r = rope64(y[:, t * LANES:(t + 1) * LANES])
            qa_ref[:, c0 + t * LANES:c0 + (t + 1) * LANES] = r.astype(BF16)
            qb_ref[:, c0 + t * LANES:c0 + (t + 1) * LANES] = pltpu.roll(r, IDX_DIM, 1).astype(BF16)
    col += iw
    y = _dot(xb, w_ref[:, col:col + LANES])
    is_key = lane < IDX_DIM
    mu = jnp.sum(jnp.where(is_key, y, 0.0), axis=-1, keepdims=True) * (1.0 / IDX_DIM)
    dlt = jnp.where(is_key, y - mu, 0.0)
    var = jnp.sum(dlt * dlt, axis=-1, keepdims=True) * (1.0 / IDX_DIM)
    kn = dlt * lax.rsqrt(var + LN_EPS) * kng_ref[...] + knb_ref[...]
    ki = rope64(kn)
    wscale = (IDX_HEADS ** -0.5) * (IDX_DIM ** -0.5)
    wh = jnp.where((lane >= IDX_DIM) & (lane < IDX_DIM + IDX_HEADS), y * wscale, 0.0)
    kw_ref[...] = ki + wh
    ke_ref[...] = ki.astype(BF16)


def _mod_spec(tm, d, col, npb):
    return pl.BlockSpec((tm, d), lambda i: (jnp.where(i < npb, 0, i - npb + 1), col))


def _proj_attn(x_all, mb, w_pad, tabs, kng, knb, *, npb, n_kv):
    nt, d = x_all.shape
    tm = TOKEN_BLOCK
    kvw = n_kv * HEAD_DIM
    iw = IDX_HEADS * IDX_DIM
    row = lambda w: pl.BlockSpec((tm, w), lambda i: (i, 0))
    tab = pl.BlockSpec((tm, LANES), lambda i: (i, 0))
    vec = pl.BlockSpec((1, LANES), lambda i: (0, 0))
    out_shapes = (
        jax.ShapeDtypeStruct((nt, d), BF16),
        jax.ShapeDtypeStruct((nt, kvw), F32),
        jax.ShapeDtypeStruct((nt, kvw), F32),
        jax.ShapeDtypeStruct((nt, kvw), BF16),
        jax.ShapeDtypeStruct((nt, kvw), BF16),
        jax.ShapeDtypeStruct((nt, iw), BF16),
        jax.ShapeDtypeStruct((nt, iw), BF16),
        jax.ShapeDtypeStruct((nt, LANES), F32),
        jax.ShapeDtypeStruct((nt, LANES), BF16),
    )
    out_specs = (row(d), row(kvw), row(kvw), row(kvw), row(kvw), row(iw), row(iw), row(LANES), row(LANES))
    est = w_pad.size * 2 + 2 * tm * (3 * d * 4 + 4 * LANES * 4) + 2 * tm * (d * 2 + kvw * 12 + iw * 4 + LANES * 6) \
        + 8 * tm * 512 * 4
    return pl.pallas_call(
        functools.partial(_proj_attn_kernel, d_model=d, n_kv=n_kv),
        grid=(nt // tm,),
        in_specs=[row(d), _mod_spec(tm, d, 0, npb), _mod_spec(tm, d, 1, npb),
                  pl.BlockSpec(memory_space=pltpu.VMEM), tab, tab, tab, tab, vec, vec],
        out_specs=out_specs,
        out_shape=out_shapes,
        compiler_params=_cparams(("parallel",), est),
        name="proj_attn",
    )(x_all, mb, mb, w_pad, *tabs, kng, knb)


def _kth_largest_key(key_ref, n_chunks, chunk, kk):
    rows = key_ref.shape[0]

    def count_ge(cand):
        candb = jnp.broadcast_to(cand, (rows, LANES))

        def body(c, acc):
            base = pl.multiple_of(c * chunk, chunk)
            for j in range(chunk // LANES):
                blk = key_ref[:, pl.ds(base + j * LANES, LANES)]
                acc = acc + jnp.where(blk >= candb, 1.0, 0.0)
            return acc

        acc = lax.fori_loop(0, n_chunks, body, jnp.zeros((rows, LANES), F32))
        return jnp.sum(acc, axis=1, keepdims=True)

    kkf = float(kk)
    prefix = jnp.where(count_ge(jnp.zeros((rows, 1), I32)) >= kkf, 0, INT_MIN).astype(I32)

    def bit_body(b, prefix):
        cand = prefix | jnp.left_shift(jnp.int32(1), 30 - b)
        return jnp.where(count_ge(cand) >= kkf, cand, prefix)

    return lax.fori_loop(0, 31, bit_body, prefix)


def _dsa_prompt_kernel(q_ref, qa_ref, qb_ref, kw_ref, ke_ref, k_ref, v_ref, o_ref,
                       key_ref, whb_ref, m_ref, l_ref, acc_ref, *, tq, tc, topk, n_kv, n_rep):
    i = pl.program_id(0)
    q0 = i * tq
    n_ch = (q0 + tq + tc - 1) // tc
    kw = kw_ref[...]
    for h in range(IDX_HEADS):
        whb_ref[h] = jnp.broadcast_to(kw[:, IDX_DIM + h:IDX_DIM + h + 1], (tq, LANES))
    qpos = q0 + lax.broadcasted_iota(I32, (tq, LANES), 0)
    lane = lax.broadcasted_iota(I32, (tq, LANES), 1)
    n_sub = tc // LANES

    def score_body(c, carry):
        base = pl.multiple_of(c * tc, tc)
        ke = ke_ref[pl.ds(base, tc), :]
        acc = [jnp.zeros((tq, LANES), F32) for _ in range(n_sub)]
        for p in range(IDX_HEADS // 2):
            d_even = _dot_nt(qa_ref[:, p * LANES:(p + 1) * LANES], ke)
            d_odd = _dot_nt(qb_ref[:, p * LANES:(p + 1) * LANES], ke)
            w_even, w_odd = whb_ref[2 * p], whb_ref[2 * p + 1]
            for j in range(n_sub):
                sl = slice(j * LANES, (j + 1) * LANES)
                acc[j] = acc[j] + jnp.maximum(d_even[:, sl], 0.0) * w_even + jnp.maximum(d_odd[:, sl], 0.0) * w_odd
        for j in range(n_sub):
            kpos = base + j * LANES + lane
            key_ref[:, pl.ds(base + j * LANES, LANES)] = jnp.where(kpos <= qpos, _sort_key(acc[j]), INT_MIN)
        return carry

    lax.fori_loop(0, n_ch, score_body, 0)

    thr = jnp.maximum(_kth_largest_key(key_ref, n_ch, tc, topk), KEY_NEG_INF + 1)
    thrb = jnp.broadcast_to(thr, (tq, LANES))
    neg_bits = lax.bitcast_convert_type(jnp.float32(NEG_BIAS), I32)

    def bias_body(c, carry):
        base = pl.multiple_of(c * tc, tc)
        for j in range(n_sub):
            sl = pl.ds(base + j * LANES, LANES)
            key_ref[:, sl] = jnp.where(key_ref[:, sl] >= thrb, 0, neg_bits)
        return carry

    lax.fori_loop(0, n_ch, bias_body, 0)

    for g in range(n_kv):
        qg = jnp.concatenate([q_ref[:, (g * n_rep + r) * LANES:(g * n_rep + r + 1) * LANES] for r in range(n_rep)], axis=0)
        m_ref[...] = jnp.full(m_ref.shape, M_FLOOR, F32)
        l_ref[...] = jnp.zeros(l_ref.shape, F32)
        acc_ref[...] = jnp.zeros(acc_ref.shape, F32)

        def att_body(c, carry, g=g, qg=qg):
            base = pl.multiple_of(c * tc, tc)
            kc = k_ref[pl.ds(base, tc), g * LANES:(g + 1) * LANES]
            vc = v_ref[pl.ds(base, tc), g * LANES:(g + 1) * LANES]
            bias = lax.bitcast_convert_type(key_ref[:, pl.ds(base, tc)], F32)
            s = _dot_nt(qg, kc) + jnp.concatenate([bias] * n_rep, axis=0)
            m_old = m_ref[...]
            m_new = jnp.maximum(m_old, jnp.max(s, axis=1, keepdims=True))
            alpha = jnp.exp2(m_old - m_new)
            p = jnp.exp2(s - _rep_lanes(m_new, n_sub))
            l_ref[...] = alpha * l_ref[...] + jnp.sum(p, axis=1, keepdims=True)
            acc_ref[...] = alpha * acc_ref[...] + _dot(p.astype(BF16), vc)
            m_ref[...] = m_new
            return carry

        lax.fori_loop(0, n_ch, att_body, 0)
        o = acc_ref[...] / l_ref[...]
        for r in range(n_rep):
            h = g * n_rep + r
            o_ref[:, h * LANES:(h + 1) * LANES] = o[r * tq:(r + 1) * tq].astype(BF16)


def _dsa_prompt(q, qa, qb, kw, ke, kb, vb, *, n_prompt, topk, n_kv):
    d = q.shape[1]
    n_rep = d // HEAD_DIM // n_kv
    tq, tc = Q_TILE, min(KEY_CHUNK, n_prompt)
    iw = IDX_HEADS * IDX_DIM
    kvw = n_kv * HEAD_DIM
    whole = pl.BlockSpec(memory_space=pltpu.VMEM)
    row = lambda w: pl.BlockSpec((tq, w), lambda i: (i, 0))
    est = n_prompt * (LANES * 2 + kvw * 4) + tq * n_prompt * 4 + IDX_HEADS * tq * LANES * 4 \
        + 3 * n_rep * tq * LANES * 4 + 2 * tq * (2 * d * 2 + 2 * iw * 2 + LANES * 4) + 10 * n_rep * tq * tc * 4
    return pl.pallas_call(
        functools.partial(_dsa_prompt_kernel, tq=tq, tc=tc, topk=topk, n_kv=n_kv, n_rep=n_rep),
        grid=(n_prompt // tq,),
        in_specs=[row(d), row(iw), row(iw), row(LANES), whole, whole, whole],
        out_specs=row(d),
        out_shape=jax.ShapeDtypeStruct((n_prompt, d), BF16),
        scratch_shapes=[
            pltpu.VMEM((tq, n_prompt), I32),
            pltpu.VMEM((IDX_HEADS, tq, LANES), F32),
            pltpu.VMEM((n_rep * tq, LANES), F32),
            pltpu.VMEM((n_rep * tq, LANES), F32),
            pltpu.VMEM((n_rep * tq, LANES), F32),
        ],
        compiler_params=_cparams(("parallel",), est),
        name="dsa_prompt",
    )(q, qa, qb, kw, ke, kb, vb)


def _dsa_sample_score_kernel(pt_ref, qa_ref, qb_ref, kw_ref, ken_ref, *refs, n_pages, page, t_new):
    page_refs, key_ref = refs[:n_pages], refs[n_pages]
    qa = qa_ref[0]
    qb = qb_ref[0]
    kw = kw_ref[0]
    n_pair = IDX_HEADS // 2
    q_even = jnp.concatenate([qa[:, p * LANES:(p + 1) * LANES] for p in range(n_pair)], axis=0).astype(BF16)
    q_odd = jnp.concatenate([qb[:, p * LANES:(p + 1) * LANES] for p in range(n_pair)], axis=0).astype(BF16)
    w_even = [jnp.broadcast_to(kw[:, IDX_DIM + 2 * p:IDX_DIM + 2 * p + 1], (t_new, LANES)) for p in range(n_pair)]
    w_odd = [jnp.broadcast_to(kw[:, IDX_DIM + 2 * p + 1:IDX_DIM + 2 * p + 2], (t_new, LANES)) for p in range(n_pair)]

    def scores(d_even, d_odd):
        s = jnp.zeros((t_new, LANES), F32)
        for p in range(n_pair):
            s = s + jnp.maximum(d_even[p * t_new:(p + 1) * t_new], 0.0) * w_even[p] \
                  + jnp.maximum(d_odd[p * t_new:(p + 1) * t_new], 0.0) * w_odd[p]
        return s

    zeros = jnp.zeros((LANES - IDX_DIM, page), F32)
    for pg in range(n_pages):
        ket = jnp.concatenate([page_refs[pg][0], zeros], axis=0).astype(BF16)
        key_ref[0, :, pg * page:(pg + 1) * page] = _sort_key(scores(_dot(q_even, ket), _dot(q_odd, ket)))
    ke_new = jnp.concatenate([ken_ref[0], jnp.zeros((LANES - t_new, LANES), F32)], axis=0).astype(BF16)
    s_new = scores(_dot_nt(q_even, ke_new), _dot_nt(q_odd, ke_new))
    qi = lax.broadcasted_iota(I32, (t_new, LANES), 0)
    kj = lax.broadcasted_iota(I32, (t_new, LANES), 1)
    key_ref[0, :, n_pages * page:n_pages * page + LANES] = jnp.where(kj <= qi, _sort_key(s_new), INT_MIN)


def _dsa_sample_scores(pt_flat, qa_s, qb_s, kw_s, ke_new, cache_kit, *, n_pages, page0):
    b, t_new, iw = qa_s.shape
    page = cache_kit.shape[2]
    width = n_pages * page + LANES
    seq = lambda w: pl.BlockSpec((1, t_new, w), lambda s, pt: (s, 0, 0))
    page_specs = [pl.BlockSpec((1, IDX_DIM, page), lambda s, pt, pg=pg: (page0 + pt[s * n_pages + pg], 0, 0))
                  for pg in range(n_pages)]
    est = 2 * (n_pages * page * LANES * 4 + t_new * (2 * iw + 2 * LANES + width) * 4) + 64 * page * LANES * 4
    return pl.pallas_call(
        functools.partial(_dsa_sample_score_kernel, n_pages=n_pages, page=page, t_new=t_new),
        grid_spec=pltpu.PrefetchScalarGridSpec(
            num_scalar_prefetch=1,
            grid=(b,),
            in_specs=[seq(iw), seq(iw), seq(LANES), seq(LANES)] + page_specs,
            out_specs=seq(width),
        ),
        out_shape=jax.ShapeDtypeStruct((b, t_new, width), I32),
        compiler_params=_cparams(("parallel",), est),
        name="dsa_sample_scores",
    )(pt_flat, qa_s, qb_s, kw_s, ke_new, *([cache_kit] * n_pages))


def _threshold_kernel(key_ref, thr_ref, *, topk, n_tiles):
    thr = jnp.maximum(_kth_largest_key(key_ref, n_tiles, LANES, topk), KEY_NEG_INF + 1)
    thr_ref[...] = jnp.broadcast_to(thr, thr_ref.shape)


def _thresholds(keys, *, topk):
    n, width = keys.shape
    tr = min(SEL_ROWS, n)
    est = 2 * tr * (width + LANES) * 4 + 8 * tr * LANES * 4
    return pl.pallas_call(
        functools.partial(_threshold_kernel, topk=topk, n_tiles=width // LANES),
        grid=(n // tr,),
        in_specs=[pl.BlockSpec((tr, width), lambda i: (i, 0))],
        out_specs=pl.BlockSpec((tr, LANES), lambda i: (i, 0)),
        out_shape=jax.ShapeDtypeStruct((n, LANES), I32),
        compiler_params=_cparams(("parallel",), est),
        name="topk_threshold",
    )(keys)


def _dsa_sample_attn_kernel(pt_ref, q_ref, key_ref, thr_ref, kn_ref, vn_ref, *refs,
                            n_pages, page, t_new, n_kv, n_rep):
    k_refs, v_refs = refs[:n_pages], refs[n_pages:2 * n_pages]
    o_ref, kall_ref, vall_ref = refs[2 * n_pages:]
    n_heads = n_kv * n_rep
    kvw = n_kv * HEAD_DIM
    past = n_pages * page
    for pg in range(n_pages):
        for g in range(n_kv):
            rows_g = pl.ds(g, page, stride=n_kv)
            kall_ref[pg * page:(pg + 1) * page, g * LANES:(g + 1) * LANES] = k_refs[pg][0, rows_g, :].astype(BF16)
            vall_ref[pg * page:(pg + 1) * page, g * LANES:(g + 1) * LANES] = v_refs[pg][0, rows_g, :].astype(BF16)
    pad = jnp.zeros((LANES - t_new, kvw), F32)
    kall_ref[past:past + LANES, :] = jnp.concatenate([kn_ref[0], pad], axis=0).astype(BF16)
    vall_ref[past:past + LANES, :] = jnp.concatenate([vn_ref[0], pad], axis=0).astype(BF16)
    q = q_ref[0]
    zero = jnp.zeros((t_new, LANES), F32)
    rows = []
    for h in range(n_heads):
        g = h // n_rep
        rows.append(jnp.concatenate([q[:, h * LANES:(h + 1) * LANES] if gg == g else zero for gg in range(n_kv)], axis=1))
    qbd = jnp.concatenate(rows, axis=0).astype(BF16)
    thr = thr_ref[0]
    width = past + LANES
    sel = key_ref[0] >= _rep_lanes(thr, width // LANES)
    bias = jnp.where(sel, 0.0, NEG_BIAS)
    s = _dot_nt(qbd, kall_ref[...]) + jnp.concatenate([bias] * n_heads, axis=0)
    m = jnp.maximum(jnp.max(s, axis=1, keepdims=True), M_FLOOR)
    p = jnp.exp2(s - m)
    l = jnp.sum(p, axis=1, keepdims=True)
    o = _dot(p.astype(BF16), vall_ref[...]) / l
    for h in range(n_heads):
        g = h // n_rep
        o_ref[0, :, h * LANES:(h + 1) * LANES] = o[h * t_new:(h + 1) * t_new, g * LANES:(g + 1) * LANES]


def _dsa_sample_attn(pt_flat, q_s, keys, thr, k_new, v_new, cache_k, cache_v, *, n_pages, n_kv, page0):
    b, t_new, d = q_s.shape
    page = cache_k.shape[1] // n_kv
    kvw = n_kv * HEAD_DIM
    n_rep = d // HEAD_DIM // n_kv
    width = n_pages * page + LANES
    seq = lambda w: pl.BlockSpec((1, t_new, w), lambda s, pt: (s, 0, 0))
    page_specs = [pl.BlockSpec((1, page * n_kv, HEAD_DIM), lambda s, pt, pg=pg: (page0 + pt[s * n_pages + pg], 0, 0))
                  for pg in range(n_pages)]
    est = 2 * (2 * n_pages * page * kvw * 4 + t_new * (2 * d + width + LANES + 2 * kvw) * 4) \
        + 2 * width * kvw * 2 + 6 * (d // HEAD_DIM) * t_new * width * 4
    return pl.pallas_call(
        functools.partial(_dsa_sample_attn_kernel, n_pages=n_pages, page=page, t_new=t_new, n_kv=n_kv, n_rep=n_rep),
        grid_spec=pltpu.PrefetchScalarGridSpec(
            num_scalar_prefetch=1,
            grid=(b,),
            in_specs=[seq(d), seq(width), seq(LANES), seq(kvw), seq(kvw)] + page_specs + page_specs,
            out_specs=seq(d),
            scratch_shapes=[pltpu.VMEM((width, kvw), BF16), pltpu.VMEM((width, kvw), BF16)],
        ),
        out_shape=jax.ShapeDtypeStruct((b, t_new, d), F32),
        compiler_params=_cparams(("parallel",), est),
        name="dsa_sample_attn",
    )(pt_flat, q_s, keys, thr, k_new, v_new, *([cache_k] * n_pages), *([cache_v] * n_pages))


def _proj_conv_kernel(x_ref, sh_ref, sc_ref, w_ref, b_ref, v_ref, *, d_model):
    xb = (x_ref[...] * (1.0 + sc_ref[...]) + sh_ref[...]).astype(BF16)
    for c0 in range(0, d_model, 512):
        sl = slice(c0, c0 + 512)
        b_ref[:, sl] = _dot(xb, w_ref[:, c0:c0 + 512])
        cg = _dot(xb, w_ref[:, d_model + c0:d_model + c0 + 512])
        xi = _dot(xb, w_ref[:, 2 * d_model + c0:2 * d_model + c0 + 512])
        v_ref[:, sl] = cg * xi


def _proj_conv(x_all, mb, w_bf, *, npb):
    nt, d = x_all.shape
    tm = TOKEN_BLOCK
    row = pl.BlockSpec((tm, d), lambda i: (i, 0))
    est = w_bf.size * 2 + 2 * tm * d * 4 * 5 + 8 * tm * 512 * 4
    return pl.pallas_call(
        functools.partial(_proj_conv_kernel, d_model=d),
        grid=(nt // tm,),
        in_specs=[row, _mod_spec(tm, d, 0, npb), _mod_spec(tm, d, 1, npb), pl.BlockSpec(memory_space=pltpu.VMEM)],
        out_specs=(row, row),
        out_shape=(jax.ShapeDtypeStruct((nt, d), F32), jax.ShapeDtypeStruct((nt, d), F32)),
        compiler_params=_cparams(("parallel",), est),
        name="proj_conv",
    )(x_all, mb, mb, w_bf)


def _conv_kernel(v_ref, b_ref, p1_ref, p2_ref, k_ref, o_ref, *, period):
    v = v_ref[...]
    tm = v.shape[0]
    t = lax.broadcasted_iota(I32, v.shape, 0) % period
    s1 = jnp.where(t == 0, p1_ref[...], pltpu.roll(v, 1, 0))
    s2 = jnp.where(t == 0, p2_ref[...], jnp.where(t == 1, p1_ref[...], pltpu.roll(v, 2, 0)))
    conv = k_ref[0:1, :] * s2 + k_ref[1:2, :] * s1 + k_ref[2:3, :] * v
    o_ref[...] = (b_ref[...] * conv).astype(BF16)


def _conv_mix(v, bgate, p1, p2, kern, *, period, row0_blocks, n_rows, prev_per_block):
    d = v.shape[1]
    tm = TOKEN_BLOCK
    row = pl.BlockSpec((tm, d), lambda i: (i + row0_blocks, 0))
    if prev_per_block:
        prev = pl.BlockSpec((SUBLANES, d), lambda i: (i, 0))
    else:
        prev = pl.BlockSpec((tm, d), lambda i: (i, 0))
    kpad = jnp.zeros((SUBLANES, d), F32).at[:CONV_W].set(kern)
    est = 2 * tm * d * (4 * 4 + 2) + 8 * tm * d * 4
    return pl.pallas_call(
        functools.partial(_conv_kernel if not prev_per_block else _conv_kernel_blockprev, period=period),
        grid=(n_rows // tm,),
        in_specs=[row, row, prev, prev, pl.BlockSpec((SUBLANES, d), lambda i: (0, 0))],
        out_specs=pl.BlockSpec((tm, d), lambda i: (i, 0)),
        out_shape=jax.ShapeDtypeStruct((n_rows, d), BF16),
        compiler_params=_cparams(("parallel",), est),
        name="conv_mix",
    )(v, bgate, p1, p2, kpad)


def _conv_kernel_blockprev(v_ref, b_ref, p1_ref, p2_ref, k_ref, o_ref, *, period):
    v = v_ref[...]
    t = lax.broadcasted_iota(I32, v.shape, 0) % period
    p1 = jnp.broadcast_to(p1_ref[0:1, :], v.shape)
    p2 = jnp.broadcast_to(p2_ref[0:1, :], v.shape)
    s1 = jnp.where(t == 0, p1, pltpu.roll(v, 1, 0))
    s2 = jnp.where(t == 0, p2, jnp.where(t == 1, p1, pltpu.roll(v, 2, 0)))
    conv = k_ref[0:1, :] * s2 + k_ref[1:2, :] * s1 + k_ref[2:3, :] * v
    o_ref[...] = (b_ref[...] * conv).astype(BF16)


def _post_mix_kernel(a_ref, x_ref, g_ref, sh_ref, sc_ref, w_ref, lng_ref, lnb_ref, rwh_ref, rwl_ref, rb_ref,
                     x1_ref, u_ref, ti_ref, tg_ref, *, alpha):
    mix = _dot(a_ref[...], w_ref[...])
    x1 = _layer_norm_rows(alpha * x_ref[...] + g_ref[...] * mix, lng_ref[...], lnb_ref[...])
    x1_ref[...] = x1
    u = x1 * (1.0 + sc_ref[...]) + sh_ref[...]
    u_hi = u.astype(BF16)
    bits = lax.bitcast_convert_type(u_hi.astype(F32), I32)
    half = bits.shape[1] // 2
    u_ref[...] = (jnp.right_shift(bits[:, :half], 16) & 0xFFFF) | (bits[:, half:] & HI16_MASK)
    u_lo = (u - u_hi.astype(F32)).astype(BF16)
    logits = _dot(u_hi, rwh_ref[...]) + _dot(u_lo, rwh_ref[...]) + _dot(u_hi, rwl_ref[...]) + rb_ref[...]
    tm = logits.shape[0]
    lane = lax.broadcasted_iota(I32, (tm, LANES), 1)
    lane_f = lane.astype(F32)
    ti = jnp.zeros((tm, LANES), I32)
    tv = jnp.full((tm, LANES), -jnp.inf, F32)
    for r in range(TOP_K):
        m = jnp.max(logits, axis=-1, keepdims=True)
        idx = jnp.min(jnp.where(logits == m, lane_f, float(LANES)), axis=-1, keepdims=True)
        hit = lane_f == idx
        ti = jnp.where(lane == r, idx.astype(I32), ti)
        tv = jnp.where(lane == r, m, tv)
        logits = jnp.where(hit, -jnp.inf, logits)
    e = jnp.exp(tv - jnp.max(tv, axis=-1, keepdims=True))
    tg_ref[...] = e / jnp.sum(e, axis=-1, keepdims=True)
    ti_ref[...] = ti


def _post_mix(a, x_all, mb, w_bf, lng, lnb, rwh, rwl, rb, *, npb, alpha):
    nt, d = x_all.shape
    tm = TOKEN_BLOCK
    row = lambda w: pl.BlockSpec((tm, w), lambda i: (i, 0))
    vec = lambda w: pl.BlockSpec((1, w), lambda i: (0, 0))
    whole = pl.BlockSpec(memory_space=pltpu.VMEM)
    est = w_bf.size * 2 + 2 * d * LANES * 2 + 2 * tm * d * (2 + 4 * 4 + 4 + 2) + 4 * tm * LANES * 4 + 8 * tm * d * 4
    return pl.pallas_call(
        functools.partial(_post_mix_kernel, alpha=alpha),
        grid=(nt // tm,),
        in_specs=[row(d), row(d), _mod_spec(tm, d, 2, npb), _mod_spec(tm, d, 3, npb), _mod_spec(tm, d, 4, npb),
                  whole, vec(d), vec(d), whole, whole, vec(LANES)],
        out_specs=(row(d), row(d // 2), row(LANES), row(LANES)),
        out_shape=(jax.ShapeDtypeStruct((nt, d), F32), jax.ShapeDtypeStruct((nt, d // 2), I32),
                   jax.ShapeDtypeStruct((nt, LANES), I32), jax.ShapeDtypeStruct((nt, LANES), F32)),
        compiler_params=_cparams(("parallel",), est),
        name="post_mix",
    )(a, x_all, mb, mb, mb, w_bf, lng, lnb, rwh, rwl, rb)


def _gather_rows(src, idx):
    n, d = src.shape
    m = idx.shape[0]
    n_workers = SC_CORES * SC_SUBCORES
    per_w = m // n_workers
    rows = GATHER_BUF_BYTES // (d * src.dtype.itemsize)
    assert m % n_workers == 0 and per_w % (2 * rows) == 0
    mesh = plsc.VectorSubcoreMesh(core_axis_name="core", subcore_axis_name="subcore")

    @functools.partial(
        pl.kernel, out_type=jax.ShapeDtypeStruct((m, d), src.dtype), mesh=mesh, name="gather_rows",
        scratch_types=[pltpu.VMEM((per_w,), I32), pltpu.VMEM((rows, d), src.dtype), pltpu.VMEM((rows, d), src.dtype),
                       pltpu.SemaphoreType.DMA, pltpu.SemaphoreType.DMA])
    def gather(src_hbm, idx_hbm, dst_hbm, idx_v, buf0, buf1, sem0, sem1):
        base = (lax.axis_index("subcore") * SC_CORES + lax.axis_index("core")) * per_w
        pltpu.sync_copy(idx_hbm.at[pl.ds(base, per_w)], idx_v)

        @pl.loop(0, per_w, step=2 * rows)
        def _(off):
            g0 = pltpu.async_copy(src_hbm.at[idx_v.at[pl.ds(off, rows)]], buf0, sem0)
            g1 = pltpu.async_copy(src_hbm.at[idx_v.at[pl.ds(off + rows, rows)]], buf1, sem1)
            g0.wait()
            w0 = pltpu.async_copy(buf0, dst_hbm.at[pl.ds(base + off, rows)], sem0)
            g1.wait()
            w1 = pltpu.async_copy(buf1, dst_hbm.at[pl.ds(base + off + rows, rows)], sem1)
            w0.wait()
            w1.wait()

    return gather(src, idx)


def _expert_kernel(ie_ref, ib_ref, nv_ref, ob_ref, *refs, n_sub, tf, tn, n_up):
    x_refs = refs[:n_sub]
    wu_ref, bu_ref, wd_ref, bd_ref, o_ref, xb_ref, h_ref, wub_ref, wdp_ref, wdb_ref = refs[n_sub:]
    i, j = pl.program_id(0), pl.program_id(1)
    nv = nv_ref[i]
    half = LANES // 2
    d_half = xb_ref.shape[1] // 2
    n_hid = h_ref.shape[1]
    sub_half = n_sub // 2
    row_halves = [(0, sub_half * MOE_ROWS, 0), (sub_half * MOE_ROWS, n_sub * MOE_ROWS, sub_half)]

    @pl.when((j == 0) & (nv > 0))
    def _():
        for r in range(n_sub):
            rs = slice(r * MOE_ROWS, (r + 1) * MOE_ROWS)
            w = x_refs[r][...]
            xb_ref[rs, :d_half] = lax.bitcast_convert_type(jnp.left_shift(w, 16), F32).astype(BF16)
            xb_ref[rs, d_half:] = lax.bitcast_convert_type(w & HI16_MASK, F32).astype(BF16)

    @pl.when((j < n_up) & (nv > 0))
    def _():
        wub_ref[...] = wu_ref[0, 0].astype(BF16)
        bu = bu_ref[0, 0]
        col = pl.multiple_of(j * tf, tf)
        for r0, r1, first_sub in row_halves:
            @pl.when(first_sub < nv)
            def _(r0=r0, r1=r1):
                h = _dot(xb_ref[r0:r1, :], wub_ref[...]) + bu
                even = (lax.broadcasted_iota(I32, (r1 - r0, LANES), 1) % 2) == 0
                prods = []
                for t in range(2 * tf // LANES):
                    ht = h[:, t * LANES:(t + 1) * LANES]
                    gate = jnp.minimum(ht, SWIGLU_LIMIT)
                    up = jnp.clip(ht, -SWIGLU_LIMIT, SWIGLU_LIMIT)
                    act = jnp.where(even, gate * jax.nn.sigmoid(SWIGLU_ALPHA * gate), up + 1.0)
                    prods.append(act * pltpu.roll(act, LANES - 1, 1))
                comp = [jnp.where(even, prods[2 * t], pltpu.roll(prods[2 * t + 1], 1, 1))
                        for t in range(tf // LANES)]
                h_ref[r0:r1, pl.ds(col, tf)] = jnp.concatenate(comp, axis=1).astype(BF16)

    @pl.when((j >= n_up) & (nv > 0))
    def _():
        for c in range(tn // LANES):
            cs = slice(c * LANES, (c + 1) * LANES)
            for qd in range(n_hid // LANES):
                wdp_ref[pl.ds(qd * LANES, half, stride=2), :] = wd_ref[0, 0, qd * LANES:qd * LANES + half, cs]
                wdp_ref[pl.ds(qd * LANES + 1, half, stride=2), :] = wd_ref[0, 0, qd * LANES + half:(qd + 1) * LANES, cs]
            wdb_ref[:, cs] = wdp_ref[...].astype(BF16)
        bd = bd_ref[0, 0]
        for r0, r1, first_sub in row_halves:
            @pl.when(first_sub < nv)
            def _(r0=r0, r1=r1):
                o_ref[r0:r1, :] = _dot(h_ref[r0:r1, :], wdb_ref[...]) + bd

            @pl.when(first_sub >= nv)
            def _(r0=r0, r1=r1):
                o_ref[r0:r1, :] = jnp.zeros((r1 - r0, tn), F32)


def _experts(xs, plan, w_up, b_up, w_down, b_down, *, layer):
    item_e, item_b0, item_nv, item_ob = plan
    n_items = item_e.shape[0]
    d = w_up.shape[2]
    f = w_down.shape[2]
    tf = min(MOE_FF_TILE, f)
    tn = min(MOE_OUT_TILE, d)
    n_up, n_down = f // tf, d // tn
    n_sub = MOE_SUB
    sb = n_sub * MOE_ROWS
    assert xs.shape[1] * 2 == d and n_sub % 2 == 0

    def x_map(r):
        return lambda i, j, ie, ib, nv, ob: (jnp.where(r < nv[i], ib[i] + r, ib[i]), 0)

    def up_chunk(i, j, nv):
        return jnp.where(nv[i] > 0, jnp.minimum(j, n_up - 1), n_up - 1)

    def down_chunk(i, j, nv):
        return jnp.where(nv[i] > 0, jnp.maximum(j - n_up, 0), n_down - 1)

    in_specs = [pl.BlockSpec((MOE_ROWS, d // 2), x_map(r)) for r in range(n_sub)] + [
        pl.BlockSpec((1, 1, d, 2 * tf), lambda i, j, ie, ib, nv, ob: (layer, ie[i], 0, up_chunk(i, j, nv))),
        pl.BlockSpec((1, 1, 1, 2 * tf), lambda i, j, ie, ib, nv, ob: (layer, ie[i], 0, up_chunk(i, j, nv))),
        pl.BlockSpec((1, 1, f, tn), lambda i, j, ie, ib, nv, ob: (layer, ie[i], 0, down_chunk(i, j, nv))),
        pl.BlockSpec((1, 1, 1, tn), lambda i, j, ie, ib, nv, ob: (layer, ie[i], 0, down_chunk(i, j, nv))),
    ]
    est = 2 * sb * d * 2 + sb * (d + f) * 2 + 2 * (d * 2 * tf + f * tn) * 4 + (d * 2 * tf + f * tn) * 2 \
        + f * LANES * 4 + 2 * sb * tn * 4 + 6 * (sb // 2) * 2 * tf * 4
    depth, n_exp = w_up.shape[0], w_up.shape[1]
    return pl.pallas_call(
        functools.partial(_expert_kernel, n_sub=n_sub, tf=tf, tn=tn, n_up=n_up),
        grid_spec=pltpu.PrefetchScalarGridSpec(
            num_scalar_prefetch=4,
            grid=(n_items, n_up + n_down),
            in_specs=in_specs,
            out_specs=pl.BlockSpec((sb, tn), lambda i, j, ie, ib, nv, ob: (ob[i], down_chunk(i, j, nv))),
            scratch_shapes=[pltpu.VMEM((sb, d), BF16), pltpu.VMEM((sb, f), BF16), pltpu.VMEM((d, 2 * tf), BF16),
                            pltpu.VMEM((f, LANES), F32), pltpu.VMEM((f, tn), BF16)],
        ),
        out_shape=jax.ShapeDtypeStruct((n_items * sb, d), F32),
        compiler_params=_cparams(("arbitrary", "arbitrary"), est),
        name="experts",
    )(item_e, item_b0, item_nv, item_ob, *([xs] * n_sub), w_up,
      b_up.reshape(depth, n_exp, 1, 2 * f), w_down, b_down.reshape(depth, n_exp, 1, d))


def _combine_kernel(*refs, alpha):
    y_refs = refs[:TOP_K]
    tg_ref, x_ref, g_ref, lng_ref, lnb_ref, o_ref = refs[TOP_K:]
    tg = tg_ref[...]
    moe = jnp.zeros(x_ref.shape, F32)
    for r in range(TOP_K):
        moe = moe + tg[:, r:r + 1] * y_refs[r][...]
    o_ref[...] = _layer_norm_rows(alpha * x_ref[...] + g_ref[...] * moe, lng_ref[...], lnb_ref[...])


def _combine(y4, tg, x1, mb, lng, lnb, *, npb, alpha):
    nt, d = x1.shape
    tm = TOKEN_BLOCK
    nbt = nt // tm
    row = lambda w: pl.BlockSpec((tm, w), lambda i: (i, 0))
    vec = pl.BlockSpec((1, d), lambda i: (0, 0))
    y_specs = [pl.BlockSpec((tm, d), lambda i, r=r: (r * nbt + i, 0)) for r in range(TOP_K)]
    est = 2 * tm * (TOP_K * d + LANES + 3 * d) * 4 + 6 * tm * d * 4
    return pl.pallas_call(
        functools.partial(_combine_kernel, alpha=alpha),
        grid=(nbt,),
        in_specs=y_specs + [row(LANES), row(d), _mod_spec(tm, d, 5, npb), vec, vec],
        out_specs=row(d),
        out_shape=jax.ShapeDtypeStruct((nt, d), F32),
        compiler_params=_cparams(("parallel",), est),
        name="moe_combine",
    )(*([y4] * TOP_K), tg, x1, mb, lng, lnb)


def _moe_plan(top_e, n_exp):
    nt = top_e.shape[0]
    na = nt * TOP_K
    sb = MOE_SUB * MOE_ROWS
    e_flat = top_e.reshape(-1)
    order = jnp.argsort(e_flat).astype(I32)
    e_sorted = e_flat[order]
    experts = jnp.arange(n_exp, dtype=I32)
    starts = jnp.searchsorted(e_sorted, experts, side='left').astype(I32)
    counts = jnp.searchsorted(e_sorted, experts, side='right').astype(I32) - starts
    padded = (counts + MOE_ROWS - 1) // MOE_ROWS * MOE_ROWS
    gend = jnp.cumsum(padded)
    gstart = gend - padded
    rank = jnp.arange(na, dtype=I32) - starts[e_sorted]
    n_slot_blocks = na // MOE_ROWS + n_exp
    slots = jnp.arange(n_slot_blocks * MOE_ROWS, dtype=I32)
    e_slot = jnp.minimum(jnp.searchsorted(gend, slots, side='right'), n_exp - 1).astype(I32)
    pos = slots - gstart[e_slot]
    src = jnp.clip(starts[e_slot] + pos, 0, na - 1)
    slot_tok = jnp.where((pos >= 0) & (pos < counts[e_slot]), order[src] // TOP_K, 0).astype(I32)
    items_e = (counts + sb - 1) // sb
    item_end = jnp.cumsum(items_e)
    item_first = item_end - items_e
    out_row_sorted = (item_first[e_sorted] + rank // sb) * sb + rank % sb
    _, out_row_of_assign = lax.sort((order, out_row_sorted), num_keys=1)
    n_items = na // sb + n_exp
    ids = jnp.arange(n_items, dtype=I32)
    n_real = item_end[-1]
    is_real = ids < n_real
    ids_c = jnp.minimum(ids, n_real - 1).astype(I32)
    e_of = jnp.minimum(jnp.searchsorted(item_end, ids_c, side='right'), n_exp - 1).astype(I32)
    s_in_e = ids_c - item_first[e_of]
    rows_left = counts[e_of] - s_in_e * sb
    nvb = jnp.where(is_real, (jnp.minimum(rows_left, sb) + MOE_ROWS - 1) // MOE_ROWS, 0).astype(I32)
    blk0 = ((gstart[e_of] + s_in_e * sb) // MOE_ROWS).astype(I32)
    return slot_tok, out_row_of_assign, (e_of, blk0, nvb, ids_c)


def _rope_tables(pos, width):
    inv = ROPE_THETA ** (-jnp.arange(0, width, 2, dtype=F32) / width)
    ang = pos.astype(F32)[:, None] * inv[None, :]
    c, s = jnp.cos(ang), jnp.sin(ang)
    reps = LANES // width
    return jnp.tile(jnp.concatenate([c, c], axis=-1), (1, reps)), jnp.tile(jnp.concatenate([-s, s], axis=-1), (1, reps))


def _pad_lanes(v, fill=0.0):
    return jnp.full((1, LANES), fill, F32).at[0, :v.shape[0]].set(v)


def kernel(x_prompt, x_sample, cache_k, cache_v, cache_kidx, state_conv, page_table, c_prompt, c_sample, ada_w, ada_b, ln_g, ln_b, attn_w_in, attn_kidx_g, attn_kidx_b, attn_w_out, conv_w_in, conv_kernel, conv_w_out, router_w, router_b, expert_w_up, expert_b_up, expert_w_down, expert_b_down):
    bp, n_prompt, d = x_prompt.shape
    b, t_new, _ = x_sample.shape
    depth = ada_w.shape[0]
    n_kv = cache_k.shape[3]
    page = cache_k.shape[2]
    n_pages = page_table.shape[1]
    past = n_pages * page
    n_exp = router_w.shape[-1]
    n_sample = b * t_new
    nt = n_prompt + n_sample
    tm = TOKEN_BLOCK
    assert bp == 1 and t_new == SUBLANES and n_prompt % tm == 0 and n_sample % tm == 0 and n_exp <= LANES
    assert d % 512 == 0 and (IDX_HEADS * IDX_DIM) % 512 == 0 and n_prompt % min(KEY_CHUNK, n_prompt) == 0
    npb = n_prompt // tm
    alpha = (2.0 * depth) ** 0.25
    kvw = n_kv * HEAD_DIM
    iw = IDX_HEADS * IDX_DIM

    x_all = jnp.concatenate([x_prompt.reshape(n_prompt, d), x_sample.reshape(n_sample, d)], axis=0)
    c_all = jnp.concatenate([c_prompt, c_sample], axis=0)
    mc = -(-c_all.shape[0] // SUBLANES) * SUBLANES
    c_all = jnp.pad(c_all, ((0, mc - c_all.shape[0]), (0, 0)))
    mod = _ada_mod(c_all, ada_w, ada_b)

    pos = jnp.concatenate([jnp.arange(n_prompt, dtype=I32), jnp.tile(past + jnp.arange(t_new, dtype=I32), b)])
    tabs = _rope_tables(pos, HEAD_DIM) + _rope_tables(pos, IDX_DIM)
    pt_flat = page_table.reshape(-1).astype(I32)
    n_pool = cache_k.shape[1]
    n_cache = cache_k.shape[0] * n_pool
    cache_k2 = cache_k.reshape(n_cache, page * n_kv, HEAD_DIM)
    cache_v2 = cache_v.reshape(n_cache, page * n_kv, HEAD_DIM)
    cache_kit = jnp.swapaxes(cache_kidx.reshape(n_cache, page, IDX_DIM), 1, 2)

    outs = dict(kp=[], vp=[], kip=[], ks=[], vs=[], kis=[], cp=[], cs=[])
    for i in range(depth):
        mb = jnp.concatenate([jnp.broadcast_to(mod[i, 0:1], (tm, 6 * d)),
                              jnp.repeat(mod[i, 1:1 + b], t_new, axis=0)], axis=0)
        if i % 2 == 0:
            a = i // 2
            w_pad = jnp.pad(attn_w_in[a], ((0, 0), (0, LANES - IDX_DIM - IDX_HEADS))).astype(BF16)
            q, k32, v32, kb, vb, qa, qb, kw, ke = _proj_attn(
                x_all, mb, w_pad, tabs, _pad_lanes(attn_kidx_g[a]), _pad_lanes(attn_kidx_b[a]), npb=npb, n_kv=n_kv)
            topk_p = min(TOPK_MAX, n_prompt // TOPK_DIV)
            o_p = _dsa_prompt(q, qa, qb, kw, ke[:n_prompt], kb[:n_prompt], vb[:n_prompt],
                              n_prompt=n_prompt, topk=topk_p, n_kv=n_kv)
            seq3 = lambda z: z[n_prompt:].astype(F32).reshape(b, t_new, z.shape[1])
            keys = _dsa_sample_scores(pt_flat, seq3(qa), seq3(qb), seq3(kw), seq3(ke),
                                      cache_kit, n_pages=n_pages, page0=a * n_pool)
            topk_s = min(TOPK_MAX, (past + t_new) // TOPK_DIV)
            thr = _thresholds(keys.reshape(n_sample, -1), topk=topk_s)
            o_s = _dsa_sample_attn(pt_flat, seq3(q), keys, thr.reshape(b, t_new, LANES), seq3(k32), seq3(v32),
                                   cache_k2, cache_v2, n_pages=n_pages, n_kv=n_kv, page0=a * n_pool)
            mix_in = jnp.concatenate([o_p, o_s.reshape(n_sample, d).astype(BF16)], axis=0)
            w_out = attn_w_out[a].astype(BF16)
            outs['kp'].append(k32[:n_prompt].reshape(1, n_prompt, n_kv, HEAD_DIM))
            outs['vp'].append(v32[:n_prompt].reshape(1, n_prompt, n_kv, HEAD_DIM))
            outs['kip'].append(kw[:n_prompt, :IDX_DIM].reshape(1, n_prompt, IDX_DIM))
            outs['ks'].append(k32[n_prompt:].reshape(b, t_new, n_kv, HEAD_DIM))
            outs['vs'].append(v32[n_prompt:].reshape(b, t_new, n_kv, HEAD_DIM))
            outs['kis'].append(kw[n_prompt:, :IDX_DIM].reshape(b, t_new, IDX_DIM))
        else:
            ci = i // 2
            bgate, v = _proj_conv(x_all, mb, conv_w_in[ci].astype(BF16), npb=npb)
            zrow = jnp.zeros((1, d), F32)
            prev1 = jnp.concatenate([zrow, v[tm - 1:n_prompt - 1:tm]], axis=0)
            prev2 = jnp.concatenate([zrow, v[tm - 2:n_prompt - 2:tm]], axis=0)
            mix_p = _conv_mix(v, bgate, jnp.repeat(prev1, SUBLANES, axis=0), jnp.repeat(prev2, SUBLANES, axis=0),
                              conv_kernel[ci], period=tm, row0_blocks=0, n_rows=n_prompt, prev_per_block=True)
            st = state_conv[ci]
            mix_s = _conv_mix(v, bgate, jnp.repeat(st[:, 1], t_new, axis=0), jnp.repeat(st[:, 0], t_new, axis=0),
                              conv_kernel[ci], period=t_new, row0_blocks=npb, n_rows=n_sample, prev_per_block=False)
            mix_in = jnp.concatenate([mix_p, mix_s], axis=0)
            w_out = conv_w_out[ci].astype(BF16)
            outs['cp'].append(v[n_prompt - (CONV_W - 1):n_prompt].reshape(1, CONV_W - 1, d))
            outs['cs'].append(v[n_prompt:].reshape(b, t_new, d)[:, t_new - (CONV_W - 1):])

        rw = jnp.pad(router_w[i], ((0, 0), (0, LANES - n_exp)))
        rwh = rw.astype(BF16)
        rwl = (rw - rwh.astype(F32)).astype(BF16)
        rb = _pad_lanes(router_b[i], fill=-1e30)
        x1, u2, ti, tg = _post_mix(mix_in, x_all, mb, w_out, ln_g[i, 0:1], ln_b[i, 0:1], rwh, rwl, rb,
                                   npb=npb, alpha=alpha)
        slot_tok, out_row_of_assign, plan = _moe_plan(ti[:, :TOP_K], n_exp)
        xs = _gather_rows(u2, slot_tok)
        ys = _experts(xs, plan, expert_w_up, expert_b_up, expert_w_down, expert_b_down, layer=i)
        y4 = _gather_rows(ys, out_row_of_assign.reshape(nt, TOP_K).T.reshape(-1))
        x_all = _combine(y4, tg, x1, mb, ln_g[i, 1:2], ln_b[i, 1:2], npb=npb, alpha=alpha)

    y_prompt = x_all[:n_prompt].reshape(1, n_prompt, d)
    y_sample = x_all[n_prompt:].reshape(b, t_new, d)
    st = lambda name: jnp.stack(outs[name])
    return (y_prompt, y_sample, st('kp'), st('vp'), st('kip'), st('ks'), st('vs'), st('kis'), st('cp'), st('cs'))
```

```python
import functools

import jax
import jax.numpy as jnp
from jax import lax
from jax.experimental import pallas as pl
from jax.experimental.pallas import tpu as pltpu
from jax.experimental.pallas import tpu_sc as plsc

F32 = jnp.float32
BF16 = jnp.bfloat16
I32 = jnp.int32

HEAD_DIM = 128
IDX_HEADS = 16
IDX_DIM = 64
TOPK_MAX = 256
TOPK_DIV = 4
CONV_W = 3
TOP_K = 4
SWIGLU_LIMIT = 7.0
SWIGLU_ALPHA = 1.702
ROPE_THETA = 10000.0
LN_EPS = 1e-5

LANES = 128
SUBLANES = 8
SC_CORES = 2
SC_SUBCORES = 16
V7X_VMEM_BYTES = 64 * 1024 * 1024
VMEM_CAP_BYTES = V7X_VMEM_BYTES - 8 * 1024 * 1024

TOKEN_BLOCK = 256
Q_TILE = 128
KEY_CHUNK = 512
SEL_ROWS = 256
MOE_ROWS = 128
MOE_SUB = 10
MOE_FF_TILE = 256
MOE_OUT_TILE = 512
GATHER_BUF_BYTES = 128 * 1024
ADA_N_TILE = 1024

INT_MIN = -2 ** 31
HI16_MASK = -65536
QK_SCALE_LOG2 = (HEAD_DIM ** -0.5) * 1.4426950408889634
KEY_NEG_INF = -2139095041
NEG_BIAS = -2e30
M_FLOOR = -1e30


def _cparams(sem, est_bytes):
    limit = int(min(max(est_bytes, 16 * 1024 * 1024), VMEM_CAP_BYTES))
    return pltpu.CompilerParams(dimension_semantics=sem, vmem_limit_bytes=limit)


def _dot(a, b):
    return jnp.dot(a, b, preferred_element_type=F32)


def _dot_nt(a, b):
    return lax.dot_general(a, b, (((1,), (1,)), ((), ())), preferred_element_type=F32)


def _rep_lanes(x, n):
    return x if n == 1 else jnp.concatenate([x] * n, axis=1)


def _layer_norm_rows(y, g, b):
    mu = jnp.mean(y, axis=-1, keepdims=True)
    d = y - mu
    var = jnp.mean(d * d, axis=-1, keepdims=True)
    return d * lax.rsqrt(var + LN_EPS) * g + b


def _sort_key(s):
    bits = lax.bitcast_convert_type(s, I32)
    return bits ^ (jnp.right_shift(bits, 31) & 0x7FFFFFFF)


def _ada_kernel(c_ref, w_ref, b_ref, o_ref):
    c = c_ref[...]
    a = (c * jax.nn.sigmoid(c)).astype(BF16)
    o_ref[0] = _dot(a, w_ref[0].astype(BF16)) + b_ref[0]


def _ada_mod(c_all, ada_w, ada_b):
    depth, d, n6 = ada_w.shape
    mc = c_all.shape[0]
    tn = ADA_N_TILE
    est = 2 * (d * tn * 4 + mc * tn * 4) + mc * d * 4 * 2 + d * tn * 2
    return pl.pallas_call(
        _ada_kernel,
        grid=(depth, n6 // tn),
        in_specs=[
            pl.BlockSpec((mc, d), lambda l, j: (0, 0)),
            pl.BlockSpec((1, d, tn), lambda l, j: (l, 0, j)),
            pl.BlockSpec((1, 1, tn), lambda l, j: (l, 0, j)),
        ],
        out_specs=pl.BlockSpec((1, mc, tn), lambda l, j: (l, 0, j)),
        out_shape=jax.ShapeDtypeStruct((depth, mc, n6), F32),
        compiler_params=_cparams(("parallel", "parallel"), est),
        name="ada_mod",
    )(c_all, ada_w, ada_b.reshape(depth, 1, n6))


def _proj_attn_kernel(x_ref, sh_ref, sc_ref, w_ref, c128_ref, s128_ref, c64_ref, s64_ref, kng_ref, knb_ref,
                      q_ref, k_ref, v_ref, kb_ref, vb_ref, qa_ref, qb_ref, kw_ref, ke_ref, *, d_model, n_kv):
    xb = (x_ref[...] * (1.0 + sc_ref[...]) + sh_ref[...]).astype(BF16)
    c128, s128 = c128_ref[...], s128_ref[...]
    c64, s64 = c64_ref[...], s64_ref[...]
    tm = xb.shape[0]
    lane = lax.broadcasted_iota(I32, (tm, LANES), 1)
    low_half = (lane % IDX_DIM) < (IDX_DIM // 2)

    def rope128(y):
        return y * c128 + pltpu.roll(y, HEAD_DIM // 2, 1) * s128

    def rope64(y):
        rot = jnp.where(low_half, pltpu.roll(y, LANES - IDX_DIM // 2, 1), pltpu.roll(y, IDX_DIM // 2, 1))
        return y * c64 + rot * s64

    kvw = n_kv * HEAD_DIM
    col = 0
    for c0 in range(0, d_model, 512):
        y = _dot(xb, w_ref[:, col + c0:col + c0 + 512])
        for t in range(4):
            r = rope128(y[:, t * LANES:(t + 1) * LANES]) * QK_SCALE_LOG2
            q_ref[:, c0 + t * LANES:c0 + (t + 1) * LANES] = r.astype(BF16)
    col += d_model
    y = _dot(xb, w_ref[:, col:col + kvw])
    for t in range(n_kv):
        r = rope128(y[:, t * LANES:(t + 1) * LANES])
        k_ref[:, t * LANES:(t + 1) * LANES] = r
        kb_ref[:, t * LANES:(t + 1) * LANES] = r.astype(BF16)
    col += kvw
    y = _dot(xb, w_ref[:, col:col + kvw])
    v_ref[...] = y
    vb_ref[...] = y.astype(BF16)
    col += kvw
    iw = IDX_HEADS * IDX_DIM
    for c0 in range(0, iw, 512):
        y = _dot(xb, w_ref[:, col + c0:col + c0 + 512])
        for t in range(4):
            r = rope64(y[:, t * LANES:(t + 1) * LANES])
            qa_ref[:, c0 + t * LANES:c0 + (t + 1) * LANES] = r.astype(BF16)
            qb_ref[:, c0 + t * LANES:c0 + (t + 1) * LANES] = pltpu.roll(r, IDX_DIM, 1).astype(BF16)
    col += iw
    y = _dot(xb, w_ref[:, col:col + LANES])
    is_key = lane < IDX_DIM
    mu = jnp.sum(jnp.where(is_key, y, 0.0), axis=-1, keepdims=True) * (1.0 / IDX_DIM)
    dlt = jnp.where(is_key, y - mu, 0.0)
    var = jnp.sum(dlt * dlt, axis=-1, keepdims=True) * (1.0 / IDX_DIM)
    kn = dlt * lax.rsqrt(var + LN_EPS) * kng_ref[...] + knb_ref[...]
    ki = rope64(kn)
    wscale = (IDX_HEADS ** -0.5) * (IDX_DIM ** -0.5)
    wh = jnp.where((lane >= IDX_DIM) & (lane < IDX_DIM + IDX_HEADS), y * wscale, 0.0)
    kw_ref[...] = ki + wh
    ke_ref[...] = ki.astype(BF16)


def _mod_spec(tm, d, col, npb):
    return pl.BlockSpec((tm, d), lambda i: (jnp.where(i < npb, 0, i - npb + 1), col))


def _proj_attn(x_all, mb, w_pad, tabs, kng, knb, *, npb, n_kv):
    nt, d = x_all.shape
    tm = TOKEN_BLOCK
    kvw = n_kv * HEAD_DIM
    iw = IDX_HEADS * IDX_DIM
    row = lambda w: pl.BlockSpec((tm, w), lambda i: (i, 0))
    tab = pl.BlockSpec((tm, LANES), lambda i: (i, 0))
    vec = pl.BlockSpec((1, LANES), lambda i: (0, 0))
    out_shapes = (
        jax.ShapeDtypeStruct((nt, d), BF16),
        jax.ShapeDtypeStruct((nt, kvw), F32),
        jax.ShapeDtypeStruct((nt, kvw), F32),
        jax.ShapeDtypeStruct((nt, kvw), BF16),
        jax.ShapeDtypeStruct((nt, kvw), BF16),
        jax.ShapeDtypeStruct((nt, iw), BF16),
        jax.ShapeDtypeStruct((nt, iw), BF16),
        jax.ShapeDtypeStruct((nt, LANES), F32),
        jax.ShapeDtypeStruct((nt, LANES), BF16),
    )
    out_specs = (row(d), row(kvw), row(kvw), row(kvw), row(kvw), row(iw), row(iw), row(LANES), row(LANES))
    est = w_pad.size * 2 + 2 * tm * (3 * d * 4 + 4 * LANES * 4) + 2 * tm * (d * 2 + kvw * 12 + iw * 4 + LANES * 6) \
        + 8 * tm * 512 * 4
    return pl.pallas_call(
        functools.partial(_proj_attn_kernel, d_model=d, n_kv=n_kv),
        grid=(nt // tm,),
        in_specs=[row(d), _mod_spec(tm, d, 0, npb), _mod_spec(tm, d, 1, npb),
                  pl.BlockSpec(memory_space=pltpu.VMEM), tab, tab, tab, tab, vec, vec],
        out_specs=out_specs,
        out_shape=out_shapes,
        compiler_params=_cparams(("parallel",), est),
        name="proj_attn",
    )(x_all, mb, mb, w_pad, *tabs, kng, knb)


def _kth_largest_key(key_ref, n_chunks, chunk, kk):
    rows = key_ref.shape[0]

    def count_ge(cand):
        candb = jnp.broadcast_to(cand, (rows, LANES))

        def body(c, acc):
            base = pl.multiple_of(c * chunk, chunk)
            for j in range(chunk // LANES):
                blk = key_ref[:, pl.ds(base + j * LANES, LANES)]
                acc = acc + jnp.where(blk >= candb, 1.0, 0.0)
            return acc

        acc = lax.fori_loop(0, n_chunks, body, jnp.zeros((rows, LANES), F32))
        return jnp.sum(acc, axis=1, keepdims=True)

    kkf = float(kk)
    c0 = count_ge(jnp.zeros((rows, 1), I32))
    prefix = jnp.where(c0 >= kkf, 0, INT_MIN).astype(I32)
    n_all = float(n_chunks * chunk) if isinstance(n_chunks, int) else (n_chunks * chunk).astype(F32)
    cnt = jnp.where(c0 >= kkf, c0, n_all)

    def unsettled(cnt):
        return jnp.max(jnp.abs(cnt - kkf)).astype(I32)

    def cond(carry):
        b, _, _, open_rows = carry
        return (b < 31) & (open_rows > 0)

    def bit_body(carry):
        b, prefix, cnt, _ = carry
        cand = prefix | jnp.left_shift(jnp.int32(1), 30 - b)
        c = count_ge(cand)
        keep = c >= kkf
        cnt = jnp.where(keep, c, cnt)
        return b + 1, jnp.where(keep, cand, prefix), cnt, unsettled(cnt)

    _, prefix, _, _ = lax.while_loop(cond, bit_body, (jnp.int32(0), prefix, cnt, unsettled(cnt)))
    return prefix


def _dsa_prompt_kernel(q_ref, qa_ref, qb_ref, kw_ref, ke_ref, k_ref, v_ref, o_ref,
                       key_ref, whb_ref, m_ref, l_ref, acc_ref, *, tq, tc, topk, n_kv, n_rep):
    i = pl.program_id(0)
    q0 = i * tq
    n_ch = (q0 + tq + tc - 1) // tc
    kw = kw_ref[...]
    for h in range(IDX_HEADS):
        whb_ref[h] = jnp.broadcast_to(kw[:, IDX_DIM + h:IDX_DIM + h + 1], (tq, LANES))
    qpos = q0 + lax.broadcasted_iota(I32, (tq, LANES), 0)
    lane = lax.broadcasted_iota(I32, (tq, LANES), 1)
    n_sub = tc // LANES

    def score_body(c, carry):
        base = pl.multiple_of(c * tc, tc)
        ke = ke_ref[pl.ds(base, tc), :]
        acc = [jnp.zeros((tq, LANES), F32) for _ in range(n_sub)]
        for p in range(IDX_HEADS // 2):
            d_even = _dot_nt(qa_ref[:, p * LANES:(p + 1) * LANES], ke)
            d_odd = _dot_nt(qb_ref[:, p * LANES:(p + 1) * LANES], ke)
            w_even, w_odd = whb_ref[2 * p], whb_ref[2 * p + 1]
            for j in range(n_sub):
                sl = slice(j * LANES, (j + 1) * LANES)
                acc[j] = acc[j] + jnp.maximum(d_even[:, sl], 0.0) * w_even + jnp.maximum(d_odd[:, sl], 0.0) * w_odd
        for j in range(n_sub):
            kpos = base + j * LANES + lane
            key_ref[:, pl.ds(base + j * LANES, LANES)] = jnp.where(kpos <= qpos, _sort_key(acc[j]), INT_MIN)
        return carry

    lax.fori_loop(0, n_ch, score_body, 0)

    thr = jnp.maximum(_kth_largest_key(key_ref, n_ch, tc, topk), KEY_NEG_INF + 1)
    thrb = jnp.broadcast_to(thr, (tq, LANES))
    neg_bits = lax.bitcast_convert_type(jnp.float32(NEG_BIAS), I32)

    def bias_body(c, carry):
        base = pl.multiple_of(c * tc, tc)
        for j in range(n_sub):
            sl = pl.ds(base + j * LANES, LANES)
            key_ref[:, sl] = jnp.where(key_ref[:, sl] >= thrb, 0, neg_bits)
        return carry

    lax.fori_loop(0, n_ch, bias_body, 0)

    for g in range(n_kv):
        qg = jnp.concatenate([q_ref[:, (g * n_rep + r) * LANES:(g * n_rep + r + 1) * LANES] for r in range(n_rep)], axis=0)
        m_ref[...] = jnp.full(m_ref.shape, M_FLOOR, F32)
        l_ref[...] = jnp.zeros(l_ref.shape, F32)
        acc_ref[...] = jnp.zeros(acc_ref.shape, F32)

        def att_body(c, carry, g=g, qg=qg):
            base = pl.multiple_of(c * tc, tc)
            kc = k_ref[pl.ds(base, tc), g * LANES:(g + 1) * LANES]
            vc = v_ref[pl.ds(base, tc), g * LANES:(g + 1) * LANES]
            bias = lax.bitcast_convert_type(key_ref[:, pl.ds(base, tc)], F32)
            s = _dot_nt(qg, kc) + jnp.concatenate([bias] * n_rep, axis=0)
            m_old = m_ref[...]
            m_new = jnp.maximum(m_old, jnp.max(s, axis=1, keepdims=True))
            alpha = jnp.exp2(m_old - m_new)
            p = jnp.exp2(s - _rep_lanes(m_new, n_sub))
            l_ref[...] = alpha * l_ref[...] + jnp.sum(p, axis=1, keepdims=True)
            acc_ref[...] = alpha * acc_ref[...] + _dot(p.astype(BF16), vc)
            m_ref[...] = m_new
            return carry

        lax.fori_loop(0, n_ch, att_body, 0)
        o = acc_ref[...] / l_ref[...]
        for r in range(n_rep):
            h = g * n_rep + r
            o_ref[:, h * LANES:(h + 1) * LANES] = o[r * tq:(r + 1) * tq].astype(BF16)


def _dsa_prompt(q, qa, qb, kw, ke, kb, vb, *, n_prompt, topk, n_kv):
    d = q.shape[1]
    n_rep = d // HEAD_DIM // n_kv
    tq, tc = Q_TILE, min(KEY_CHUNK, n_prompt)
    iw = IDX_HEADS * IDX_DIM
    kvw = n_kv * HEAD_DIM
    whole = pl.BlockSpec(memory_space=pltpu.VMEM)
    row = lambda w: pl.BlockSpec((tq, w), lambda i: (i, 0))
    est = n_prompt * (LANES * 2 + kvw * 4) + tq * n_prompt * 4 + IDX_HEADS * tq * LANES * 4 \
        + 3 * n_rep * tq * LANES * 4 + 2 * tq * (2 * d * 2 + 2 * iw * 2 + LANES * 4) + 10 * n_rep * tq * tc * 4
    return pl.pallas_call(
        functools.partial(_dsa_prompt_kernel, tq=tq, tc=tc, topk=topk, n_kv=n_kv, n_rep=n_rep),
        grid=(n_prompt // tq,),
        in_specs=[row(d), row(iw), row(iw), row(LANES), whole, whole, whole],
        out_specs=row(d),
        out_shape=jax.ShapeDtypeStruct((n_prompt, d), BF16),
        scratch_shapes=[
            pltpu.VMEM((tq, n_prompt), I32),
            pltpu.VMEM((IDX_HEADS, tq, LANES), F32),
            pltpu.VMEM((n_rep * tq, LANES), F32),
            pltpu.VMEM((n_rep * tq, LANES), F32),
            pltpu.VMEM((n_rep * tq, LANES), F32),
        ],
        compiler_params=_cparams(("parallel",), est),
        name="dsa_prompt",
    )(q, qa, qb, kw, ke, kb, vb)


def _dsa_sample_score_kernel(pt_ref, qa_ref, qb_ref, kw_ref, ken_ref, *refs, n_pages, page, t_new):
    page_refs, key_ref = refs[:n_pages], refs[n_pages]
    qa = qa_ref[0]
    qb = qb_ref[0]
    kw = kw_ref[0]
    n_pair = IDX_HEADS // 2
    q_even = jnp.concatenate([qa[:, p * LANES:(p + 1) * LANES] for p in range(n_pair)], axis=0).astype(BF16)
    q_odd = jnp.concatenate([qb[:, p * LANES:(p + 1) * LANES] for p in range(n_pair)], axis=0).astype(BF16)
    w_even = [jnp.broadcast_to(kw[:, IDX_DIM + 2 * p:IDX_DIM + 2 * p + 1], (t_new, LANES)) for p in range(n_pair)]
    w_odd = [jnp.broadcast_to(kw[:, IDX_DIM + 2 * p + 1:IDX_DIM + 2 * p + 2], (t_new, LANES)) for p in range(n_pair)]

    def scores(d_even, d_odd):
        s = jnp.zeros((t_new, LANES), F32)
        for p in range(n_pair):
            s = s + jnp.maximum(d_even[p * t_new:(p + 1) * t_new], 0.0) * w_even[p] \
                  + jnp.maximum(d_odd[p * t_new:(p + 1) * t_new], 0.0) * w_odd[p]
        return s

    zeros = jnp.zeros((LANES - IDX_DIM, page), F32)
    for pg in range(n_pages):
        ket = jnp.concatenate([page_refs[pg][0], zeros], axis=0).astype(BF16)
        key_ref[0, :, pg * page:(pg + 1) * page] = _sort_key(scores(_dot(q_even, ket), _dot(q_odd, ket)))
    ke_new = jnp.concatenate([ken_ref[0], jnp.zeros((LANES - t_new, LANES), F32)], axis=0).astype(BF16)
    s_new = scores(_dot_nt(q_even, ke_new), _dot_nt(q_odd, ke_new))
    qi = lax.broadcasted_iota(I32, (t_new, LANES), 0)
    kj = lax.broadcasted_iota(I32, (t_new, LANES), 1)
    key_ref[0, :, n_pages * page:n_pages * page + LANES] = jnp.where(kj <= qi, _sort_key(s_new), INT_MIN)


def _dsa_sample_scores(pt_flat, qa_s, qb_s, kw_s, ke_new, cache_kit, *, n_pages, page0):
    b, t_new, iw = qa_s.shape
    page = cache_kit.shape[2]
    width = n_pages * page + LANES
    seq = lambda w: pl.BlockSpec((1, t_new, w), lambda s, pt: (s, 0, 0))
    page_specs = [pl.BlockSpec((1, IDX_DIM, page), lambda s, pt, pg=pg: (page0 + pt[s * n_pages + pg], 0, 0))
                  for pg in range(n_pages)]
    est = 2 * (n_pages * page * LANES * 4 + t_new * (2 * iw + 2 * LANES + width) * 4) + 64 * page * LANES * 4
    return pl.pallas_call(
        functools.partial(_dsa_sample_score_kernel, n_pages=n_pages, page=page, t_new=t_new),
        grid_spec=pltpu.PrefetchScalarGridSpec(
            num_scalar_prefetch=1,
            grid=(b,),
            in_specs=[seq(iw), seq(iw), seq(LANES), seq(LANES)] + page_specs,
            out_specs=seq(width),
        ),
        out_shape=jax.ShapeDtypeStruct((b, t_new, width), I32),
        compiler_params=_cparams(("parallel",), est),
        name="dsa_sample_scores",
    )(pt_flat, qa_s, qb_s, kw_s, ke_new, *([cache_kit] * n_pages))


def _threshold_kernel(key_ref, thr_ref, *, topk, n_tiles):
    thr = jnp.maximum(_kth_largest_key(key_ref, n_tiles, LANES, topk), KEY_NEG_INF + 1)
    thr_ref[...] = jnp.broadcast_to(thr, thr_ref.shape)


def _thresholds(keys, *, topk):
    n, width = keys.shape
    tr = min(SEL_ROWS, n)
    est = 2 * tr * (width + LANES) * 4 + 8 * tr * LANES * 4
    return pl.pallas_call(
        functools.partial(_threshold_kernel, topk=topk, n_tiles=width // LANES),
        grid=(n // tr,),
        in_specs=[pl.BlockSpec((tr, width), lambda i: (i, 0))],
        out_specs=pl.BlockSpec((tr, LANES), lambda i: (i, 0)),
        out_shape=jax.ShapeDtypeStruct((n, LANES), I32),
        compiler_params=_cparams(("parallel",), est),
        name="topk_threshold",
    )(keys)


def _dsa_sample_attn_kernel(pt_ref, q_ref, key_ref, thr_ref, kn_ref, vn_ref, *refs,
                            n_pages, page, t_new, n_kv, n_rep):
    k_refs, v_refs = refs[:n_pages], refs[n_pages:2 * n_pages]
    o_ref, kall_ref, vall_ref = refs[2 * n_pages:]
    n_heads = n_kv * n_rep
    kvw = n_kv * HEAD_DIM
    past = n_pages * page
    for pg in range(n_pages):
        for g in range(n_kv):
            rows_g = pl.ds(g, page, stride=n_kv)
            kall_ref[pg * page:(pg + 1) * page, g * LANES:(g + 1) * LANES] = k_refs[pg][0, rows_g, :].astype(BF16)
            vall_ref[pg * page:(pg + 1) * page, g * LANES:(g + 1) * LANES] = v_refs[pg][0, rows_g, :].astype(BF16)
    pad = jnp.zeros((LANES - t_new, kvw), F32)
    kall_ref[past:past + LANES, :] = jnp.concatenate([kn_ref[0], pad], axis=0).astype(BF16)
    vall_ref[past:past + LANES, :] = jnp.concatenate([vn_ref[0], pad], axis=0).astype(BF16)
    q = q_ref[0]
    zero = jnp.zeros((t_new, LANES), F32)
    rows = []
    for h in range(n_heads):
        g = h // n_rep
        rows.append(jnp.concatenate([q[:, h * LANES:(h + 1) * LANES] if gg == g else zero for gg in range(n_kv)], axis=1))
    qbd = jnp.concatenate(rows, axis=0).astype(BF16)
    thr = thr_ref[0]
    width = past + LANES
    sel = key_ref[0] >= _rep_lanes(thr, width // LANES)
    bias = jnp.where(sel, 0.0, NEG_BIAS)
    s = _dot_nt(qbd, kall_ref[...]) + jnp.concatenate([bias] * n_heads, axis=0)
    m = jnp.maximum(jnp.max(s, axis=1, keepdims=True), M_FLOOR)
    p = jnp.exp2(s - m)
    l = jnp.sum(p, axis=1, keepdims=True)
    o = _dot(p.astype(BF16), vall_ref[...]) / l
    for h in range(n_heads):
        g = h // n_rep
        o_ref[0, :, h * LANES:(h + 1) * LANES] = o[h * t_new:(h + 1) * t_new, g * LANES:(g + 1) * LANES]


def _dsa_sample_attn(pt_flat, q_s, keys, thr, k_new, v_new, cache_k, cache_v, *, n_pages, n_kv, page0):
    b, t_new, d = q_s.shape
    page = cache_k.shape[1] // n_kv
    kvw = n_kv * HEAD_DIM
    n_rep = d // HEAD_DIM // n_kv
    width = n_pages * page + LANES
    seq = lambda w: pl.BlockSpec((1, t_new, w), lambda s, pt: (s, 0, 0))
    page_specs = [pl.BlockSpec((1, page * n_kv, HEAD_DIM), lambda s, pt, pg=pg: (page0 + pt[s * n_pages + pg], 0, 0))
                  for pg in range(n_pages)]
    est = 2 * (2 * n_pages * page * kvw * 4 + t_new * (2 * d + width + LANES + 2 * kvw) * 4) \
        + 2 * width * kvw * 2 + 6 * (d // HEAD_DIM) * t_new * width * 4
    return pl.pallas_call(
        functools.partial(_dsa_sample_attn_kernel, n_pages=n_pages, page=page, t_new=t_new, n_kv=n_kv, n_rep=n_rep),
        grid_spec=pltpu.PrefetchScalarGridSpec(
            num_scalar_prefetch=1,
            grid=(b,),
            in_specs=[seq(d), seq(width), seq(LANES), seq(kvw), seq(kvw)] + page_specs + page_specs,
            out_specs=seq(d),
            scratch_shapes=[pltpu.VMEM((width, kvw), BF16), pltpu.VMEM((width, kvw), BF16)],
        ),
        out_shape=jax.ShapeDtypeStruct((b, t_new, d), F32),
        compiler_params=_cparams(("parallel",), est),
        name="dsa_sample_attn",
    )(pt_flat, q_s, keys, thr, k_new, v_new, *([cache_k] * n_pages), *([cache_v] * n_pages))


def _proj_conv_kernel(x_ref, sh_ref, sc_ref, w_ref, b_ref, v_ref, *, d_model):
    xb = (x_ref[...] * (1.0 + sc_ref[...]) + sh_ref[...]).astype(BF16)
    for c0 in range(0, d_model, 512):
        sl = slice(c0, c0 + 512)
        b_ref[:, sl] = _dot(xb, w_ref[:, c0:c0 + 512])
        cg = _dot(xb, w_ref[:, d_model + c0:d_model + c0 + 512])
        xi = _dot(xb, w_ref[:, 2 * d_model + c0:2 * d_model + c0 + 512])
        v_ref[:, sl] = cg * xi


def _proj_conv(x_all, mb, w_bf, *, npb):
    nt, d = x_all.shape
    tm = TOKEN_BLOCK
    row = pl.BlockSpec((tm, d), lambda i: (i, 0))
    est = w_bf.size * 2 + 2 * tm * d * 4 * 5 + 8 * tm * 512 * 4
    return pl.pallas_call(
        functools.partial(_proj_conv_kernel, d_model=d),
        grid=(nt // tm,),
        in_specs=[row, _mod_spec(tm, d, 0, npb), _mod_spec(tm, d, 1, npb), pl.BlockSpec(memory_space=pltpu.VMEM)],
        out_specs=(row, row),
        out_shape=(jax.ShapeDtypeStruct((nt, d), F32), jax.ShapeDtypeStruct((nt, d), F32)),
        compiler_params=_cparams(("parallel",), est),
        name="proj_conv",
    )(x_all, mb, mb, w_bf)


def _conv_kernel(v_ref, b_ref, p1_ref, p2_ref, k_ref, o_ref, *, period):
    v = v_ref[...]
    tm = v.shape[0]
    t = lax.broadcasted_iota(I32, v.shape, 0) % period
    s1 = jnp.where(t == 0, p1_ref[...], pltpu.roll(v, 1, 0))
    s2 = jnp.where(t == 0, p2_ref[...], jnp.where(t == 1, p1_ref[...], pltpu.roll(v, 2, 0)))
    conv = k_ref[0:1, :] * s2 + k_ref[1:2, :] * s1 + k_ref[2:3, :] * v
    o_ref[...] = (b_ref[...] * conv).astype(BF16)


def _conv_mix(v, bgate, p1, p2, kern, *, period, row0_blocks, n_rows, prev_per_block):
    d = v.shape[1]
    tm = TOKEN_BLOCK
    row = pl.BlockSpec((tm, d), lambda i: (i + row0_blocks, 0))
    if prev_per_block:
        prev = pl.BlockSpec((SUBLANES, d), lambda i: (i, 0))
    else:
        prev = pl.BlockSpec((tm, d), lambda i: (i, 0))
    kpad = jnp.zeros((SUBLANES, d), F32).at[:CONV_W].set(kern)
    est = 2 * tm * d * (4 * 4 + 2) + 8 * tm * d * 4
    return pl.pallas_call(
        functools.partial(_conv_kernel if not prev_per_block else _conv_kernel_blockprev, period=period),
        grid=(n_rows // tm,),
        in_specs=[row, row, prev, prev, pl.BlockSpec((SUBLANES, d), lambda i: (0, 0))],
        out_specs=pl.BlockSpec((tm, d), lambda i: (i, 0)),
        out_shape=jax.ShapeDtypeStruct((n_rows, d), BF16),
        compiler_params=_cparams(("parallel",), est),
        name="conv_mix",
    )(v, bgate, p1, p2, kpad)


def _conv_kernel_blockprev(v_ref, b_ref, p1_ref, p2_ref, k_ref, o_ref, *, period):
    v = v_ref[...]
    t = lax.broadcasted_iota(I32, v.shape, 0) % period
    p1 = jnp.broadcast_to(p1_ref[0:1, :], v.shape)
    p2 = jnp.broadcast_to(p2_ref[0:1, :], v.shape)
    s1 = jnp.where(t == 0, p1, pltpu.roll(v, 1, 0))
    s2 = jnp.where(t == 0, p2, jnp.where(t == 1, p1, pltpu.roll(v, 2, 0)))
    conv = k_ref[0:1, :] * s2 + k_ref[1:2, :] * s1 + k_ref[2:3, :] * v
    o_ref[...] = (b_ref[...] * conv).astype(BF16)


def _post_mix_kernel(a_ref, x_ref, g_ref, sh_ref, sc_ref, w_ref, lng_ref, lnb_ref, rwh_ref, rwl_ref, rb_ref,
                     x1_ref, u_ref, ti_ref, tg_ref, *, alpha):
    mix = _dot(a_ref[...], w_ref[...])
    x1 = _layer_norm_rows(alpha * x_ref[...] + g_ref[...] * mix, lng_ref[...], lnb_ref[...])
    x1_ref[...] = x1
    u = x1 * (1.0 + sc_ref[...]) + sh_ref[...]
    u_hi = u.astype(BF16)
    bits = lax.bitcast_convert_type(u_hi.astype(F32), I32)
    half = bits.shape[1] // 2
    u_ref[...] = (jnp.right_shift(bits[:, :half], 16) & 0xFFFF) | (bits[:, half:] & HI16_MASK)
    u_lo = (u - u_hi.astype(F32)).astype(BF16)
    logits = _dot(u_hi, rwh_ref[...]) + _dot(u_lo, rwh_ref[...]) + _dot(u_hi, rwl_ref[...]) + rb_ref[...]
    tm = logits.shape[0]
    lane = lax.broadcasted_iota(I32, (tm, LANES), 1)
    lane_f = lane.astype(F32)
    ti = jnp.zeros((tm, LANES), I32)
    tv = jnp.full((tm, LANES), -jnp.inf, F32)
    for r in range(TOP_K):
        m = jnp.max(logits, axis=-1, keepdims=True)
        idx = jnp.min(jnp.where(logits == m, lane_f, float(LANES)), axis=-1, keepdims=True)
        hit = lane_f == idx
        ti = jnp.where(lane == r, idx.astype(I32), ti)
        tv = jnp.where(lane == r, m, tv)
        logits = jnp.where(hit, -jnp.inf, logits)
    e = jnp.exp(tv - jnp.max(tv, axis=-1, keepdims=True))
    tg_ref[...] = e / jnp.sum(e, axis=-1, keepdims=True)
    ti_ref[...] = ti


def _post_mix(a, x_all, mb, w_bf, lng, lnb, rwh, rwl, rb, *, npb, alpha):
    nt, d = x_all.shape
    tm = TOKEN_BLOCK
    row = lambda w: pl.BlockSpec((tm, w), lambda i: (i, 0))
    vec = lambda w: pl.BlockSpec((1, w), lambda i: (0, 0))
    whole = pl.BlockSpec(memory_space=pltpu.VMEM)
    est = w_bf.size * 2 + 2 * d * LANES * 2 + 2 * tm * d * (2 + 4 * 4 + 4 + 2) + 4 * tm * LANES * 4 + 8 * tm * d * 4
    return pl.pallas_call(
        functools.partial(_post_mix_kernel, alpha=alpha),
        grid=(nt // tm,),
        in_specs=[row(d), row(d), _mod_spec(tm, d, 2, npb), _mod_spec(tm, d, 3, npb), _mod_spec(tm, d, 4, npb),
                  whole, vec(d), vec(d), whole, whole, vec(LANES)],
        out_specs=(row(d), row(d // 2), row(LANES), row(LANES)),
        out_shape=(jax.ShapeDtypeStruct((nt, d), F32), jax.ShapeDtypeStruct((nt, d // 2), I32),
                   jax.ShapeDtypeStruct((nt, LANES), I32), jax.ShapeDtypeStruct((nt, LANES), F32)),
        compiler_params=_cparams(("parallel",), est),
        name="post_mix",
    )(a, x_all, mb, mb, mb, w_bf, lng, lnb, rwh, rwl, rb)


def _gather_rows(src, idx):
    n, d = src.shape
    m = idx.shape[0]
    n_workers = SC_CORES * SC_SUBCORES
    per_w = m // n_workers
    rows = GATHER_BUF_BYTES // (d * src.dtype.itemsize)
    assert m % n_workers == 0 and per_w % (2 * rows) == 0
    mesh = plsc.VectorSubcoreMesh(core_axis_name="core", subcore_axis_name="subcore")

    @functools.partial(
        pl.kernel, out_type=jax.ShapeDtypeStruct((m, d), src.dtype), mesh=mesh, name="gather_rows",
        scratch_types=[pltpu.VMEM((per_w,), I32), pltpu.VMEM((rows, d), src.dtype), pltpu.VMEM((rows, d), src.dtype),
                       pltpu.SemaphoreType.DMA, pltpu.SemaphoreType.DMA])
    def gather(src_hbm, idx_hbm, dst_hbm, idx_v, buf0, buf1, sem0, sem1):
        base = (lax.axis_index("subcore") * SC_CORES + lax.axis_index("core")) * per_w
        pltpu.sync_copy(idx_hbm.at[pl.ds(base, per_w)], idx_v)

        @pl.loop(0, per_w, step=2 * rows)
        def _(off):
            g0 = pltpu.async_copy(src_hbm.at[idx_v.at[pl.ds(off, rows)]], buf0, sem0)
            g1 = pltpu.async_copy(src_hbm.at[idx_v.at[pl.ds(off + rows, rows)]], buf1, sem1)
            g0.wait()
            w0 = pltpu.async_copy(buf0, dst_hbm.at[pl.ds(base + off, rows)], sem0)
            g1.wait()
            w1 = pltpu.async_copy(buf1, dst_hbm.at[pl.ds(base + off + rows, rows)], sem1)
            w0.wait()
            w1.wait()

    return gather(src, idx)


def _expert_kernel(ie_ref, ib_ref, nv_ref, ob_ref, *refs, n_sub, tf, tn, n_up):
    x_refs = refs[:n_sub]
    wu_ref, bu_ref, wd_ref, bd_ref, o_ref, xb_ref, h_ref, wub_ref, wdp_ref, wdb_ref = refs[n_sub:]
    i, j = pl.program_id(0), pl.program_id(1)
    nv = nv_ref[i]
    half = LANES // 2
    d_half = xb_ref.shape[1] // 2
    n_hid = h_ref.shape[1]
    sub_half = n_sub // 2
    row_halves = [(0, sub_half * MOE_ROWS, 0), (sub_half * MOE_ROWS, n_sub * MOE_ROWS, sub_half)]

    @pl.when((j == 0) & (nv > 0))
    def _():
        for r in range(n_sub):
            rs = slice(r * MOE_ROWS, (r + 1) * MOE_ROWS)
            w = x_refs[r][...]
            xb_ref[rs, :d_half] = lax.bitcast_convert_type(jnp.left_shift(w, 16), F32).astype(BF16)
            xb_ref[rs, d_half:] = lax.bitcast_convert_type(w & HI16_MASK, F32).astype(BF16)

    @pl.when((j < n_up) & (nv > 0))
    def _():
        wub_ref[...] = wu_ref[0, 0].astype(BF16)
        bu = bu_ref[0, 0]
        col = pl.multiple_of(j * tf, tf)

        def up_rows(r0, r1):
            h = _dot(xb_ref[r0:r1, :], wub_ref[...]) + bu
            even = (lax.broadcasted_iota(I32, (r1 - r0, LANES), 1) % 2) == 0
            prods = []
            for t in range(2 * tf // LANES):
                ht = h[:, t * LANES:(t + 1) * LANES]
                gate = jnp.minimum(ht, SWIGLU_LIMIT)
                up = jnp.clip(ht, -SWIGLU_LIMIT, SWIGLU_LIMIT)
                act = jnp.where(even, gate * jax.nn.sigmoid(SWIGLU_ALPHA * gate), up + 1.0)
                prods.append(act * pltpu.roll(act, LANES - 1, 1))
            comp = [jnp.where(even, prods[2 * t], pltpu.roll(prods[2 * t + 1], 1, 1))
                    for t in range(tf // LANES)]
            h_ref[r0:r1, pl.ds(col, tf)] = jnp.concatenate(comp, axis=1).astype(BF16)

        @pl.when(nv > sub_half)
        def _():
            for r0, r1, _unused in row_halves:
                up_rows(r0, r1)

        @pl.when(nv <= sub_half)
        def _():
            up_rows(*row_halves[0][:2])

    @pl.when((j >= n_up) & (nv > 0))
    def _():
        for c in range(tn // LANES):
            cs = slice(c * LANES, (c + 1) * LANES)
            for qd in range(n_hid // LANES):
                wdp_ref[pl.ds(qd * LANES, half, stride=2), :] = wd_ref[0, 0, qd * LANES:qd * LANES + half, cs]
                wdp_ref[pl.ds(qd * LANES + 1, half, stride=2), :] = wd_ref[0, 0, qd * LANES + half:(qd + 1) * LANES, cs]
            wdb_ref[:, cs] = wdp_ref[...].astype(BF16)
        bd = bd_ref[0, 0]
        for r0, r1, first_sub in row_halves:
            @pl.when(first_sub < nv)
            def _(r0=r0, r1=r1):
                o_ref[r0:r1, :] = _dot(h_ref[r0:r1, :], wdb_ref[...]) + bd

            @pl.when(first_sub >= nv)
            def _(r0=r0, r1=r1):
                o_ref[r0:r1, :] = jnp.zeros((r1 - r0, tn), F32)


def _experts(xs, plan, w_up, b_up, w_down, b_down, *, layer):
    item_e, item_b0, item_nv, item_ob = plan
    n_items = item_e.shape[0]
    d = w_up.shape[2]
    f = w_down.shape[2]
    tf = min(MOE_FF_TILE, f)
    tn = min(MOE_OUT_TILE, d)
    n_up, n_down = f // tf, d // tn
    n_sub = MOE_SUB
    sb = n_sub * MOE_ROWS
    assert xs.shape[1] * 2 == d and n_sub % 2 == 0

    def x_map(r):
        return lambda i, j, ie, ib, nv, ob: (jnp.where(r < nv[i], ib[i] + r, ib[i]), 0)

    def up_chunk(i, j, nv):
        return jnp.where(nv[i] > 0, jnp.minimum(j, n_up - 1), n_up - 1)

    def down_chunk(i, j, nv):
        return jnp.where(nv[i] > 0, jnp.maximum(j - n_up, 0), n_down - 1)

    in_specs = [pl.BlockSpec((MOE_ROWS, d // 2), x_map(r)) for r in range(n_sub)] + [
        pl.BlockSpec((1, 1, d, 2 * tf), lambda i, j, ie, ib, nv, ob: (layer, ie[i], 0, up_chunk(i, j, nv))),
        pl.BlockSpec((1, 1, 1, 2 * tf), lambda i, j, ie, ib, nv, ob: (layer, ie[i], 0, up_chunk(i, j, nv))),
        pl.BlockSpec((1, 1, f, tn), lambda i, j, ie, ib, nv, ob: (layer, ie[i], 0, down_chunk(i, j, nv))),
        pl.BlockSpec((1, 1, 1, tn), lambda i, j, ie, ib, nv, ob: (layer, ie[i], 0, down_chunk(i, j, nv))),
    ]
    est = 2 * sb * d * 2 + sb * (d + f) * 2 + 2 * (d * 2 * tf + f * tn) * 4 + (d * 2 * tf + f * tn) * 2 \
        + f * LANES * 4 + 2 * sb * tn * 4 + 6 * (sb // 2) * 2 * tf * 4
    depth, n_exp = w_up.shape[0], w_up.shape[1]
    return pl.pallas_call(
        functools.partial(_expert_kernel, n_sub=n_sub, tf=tf, tn=tn, n_up=n_up),
        grid_spec=pltpu.PrefetchScalarGridSpec(
            num_scalar_prefetch=4,
            grid=(n_items, n_up + n_down),
            in_specs=in_specs,
            out_specs=pl.BlockSpec((sb, tn), lambda i, j, ie, ib, nv, ob: (ob[i], down_chunk(i, j, nv))),
            scratch_shapes=[pltpu.VMEM((sb, d), BF16), pltpu.VMEM((sb, f), BF16), pltpu.VMEM((d, 2 * tf), BF16),
                            pltpu.VMEM((f, LANES), F32), pltpu.VMEM((f, tn), BF16)],
        ),
        out_shape=jax.ShapeDtypeStruct((n_items * sb, d), F32),
        compiler_params=_cparams(("arbitrary", "arbitrary"), est),
        name="experts",
    )(item_e, item_b0, item_nv, item_ob, *([xs] * n_sub), w_up,
      b_up.reshape(depth, n_exp, 1, 2 * f), w_down, b_down.reshape(depth, n_exp, 1, d))


def _combine_kernel(*refs, alpha):
    y_refs = refs[:TOP_K]
    tg_ref, x_ref, g_ref, lng_ref, lnb_ref, o_ref = refs[TOP_K:]
    tg = tg_ref[...]
    moe = jnp.zeros(x_ref.shape, F32)
    for r in range(TOP_K):
        moe = moe + tg[:, r:r + 1] * y_refs[r][...]
    o_ref[...] = _layer_norm_rows(alpha * x_ref[...] + g_ref[...] * moe, lng_ref[...], lnb_ref[...])


def _combine(y4, tg, x1, mb, lng, lnb, *, npb, alpha):
    nt, d = x1.shape
    tm = TOKEN_BLOCK
    nbt = nt // tm
    row = lambda w: pl.BlockSpec((tm, w), lambda i: (i, 0))
    vec = pl.BlockSpec((1, d), lambda i: (0, 0))
    y_specs = [pl.BlockSpec((tm, d), lambda i, r=r: (r * nbt + i, 0)) for r in range(TOP_K)]
    est = 2 * tm * (TOP_K * d + LANES + 3 * d) * 4 + 6 * tm * d * 4
    return pl.pallas_call(
        functools.partial(_combine_kernel, alpha=alpha),
        grid=(nbt,),
        in_specs=y_specs + [row(LANES), row(d), _mod_spec(tm, d, 5, npb), vec, vec],
        out_specs=row(d),
        out_shape=jax.ShapeDtypeStruct((nt, d), F32),
        compiler_params=_cparams(("parallel",), est),
        name="moe_combine",
    )(*([y4] * TOP_K), tg, x1, mb, lng, lnb)


def _moe_plan(top_e, n_exp):
    nt = top_e.shape[0]
    na = nt * TOP_K
    sb = MOE_SUB * MOE_ROWS
    e_flat = top_e.reshape(-1)
    order = jnp.argsort(e_flat).astype(I32)
    e_sorted = e_flat[order]
    experts = jnp.arange(n_exp, dtype=I32)
    counts = jnp.sum((e_flat[None, :] == experts[:, None]).astype(I32), axis=1)
    starts = jnp.cumsum(counts) - counts
    padded = (counts + MOE_ROWS - 1) // MOE_ROWS * MOE_ROWS
    gend = jnp.cumsum(padded)
    gstart = gend - padded
    rank = jnp.arange(na, dtype=I32) - starts[e_sorted]
    n_slot_blocks = na // MOE_ROWS + n_exp
    slots = jnp.arange(n_slot_blocks * MOE_ROWS, dtype=I32)
    e_slot = jnp.minimum(jnp.sum((gend[None, :] <= slots[:, None]).astype(I32), axis=1), n_exp - 1)
    pos = slots - gstart[e_slot]
    src = jnp.clip(starts[e_slot] + pos, 0, na - 1)
    slot_tok = jnp.where((pos >= 0) & (pos < counts[e_slot]), order[src] // TOP_K, 0).astype(I32)
    items_e = (counts + sb - 1) // sb
    item_end = jnp.cumsum(items_e)
    item_first = item_end - items_e
    out_row_sorted = (item_first[e_sorted] + rank // sb) * sb + rank % sb
    _, out_row_of_assign = lax.sort((order, out_row_sorted), num_keys=1)
    n_items = na // sb + n_exp
    ids = jnp.arange(n_items, dtype=I32)
    n_real = item_end[-1]
    is_real = ids < n_real
    ids_c = jnp.minimum(ids, n_real - 1).astype(I32)
    e_of = jnp.minimum(jnp.sum((item_end[None, :] <= ids_c[:, None]).astype(I32), axis=1), n_exp - 1)
    s_in_e = ids_c - item_first[e_of]
    rows_left = counts[e_of] - s_in_e * sb
    nvb = jnp.where(is_real, (jnp.minimum(rows_left, sb) + MOE_ROWS - 1) // MOE_ROWS, 0).astype(I32)
    blk0 = ((gstart[e_of] + s_in_e * sb) // MOE_ROWS).astype(I32)
    return slot_tok, out_row_of_assign, (e_of, blk0, nvb, ids_c)


def _rope_tables(pos, width):
    inv = ROPE_THETA ** (-jnp.arange(0, width, 2, dtype=F32) / width)
    ang = pos.astype(F32)[:, None] * inv[None, :]
    c, s = jnp.cos(ang), jnp.sin(ang)
    reps = LANES // width
    return jnp.tile(jnp.concatenate([c, c], axis=-1), (1, reps)), jnp.tile(jnp.concatenate([-s, s], axis=-1), (1, reps))


def _pad_lanes(v, fill=0.0):
    return jnp.full((1, LANES), fill, F32).at[0, :v.shape[0]].set(v)


def kernel(x_prompt, x_sample, cache_k, cache_v, cache_kidx, state_conv, page_table, c_prompt, c_sample, ada_w, ada_b, ln_g, ln_b, attn_w_in, attn_kidx_g, attn_kidx_b, attn_w_out, conv_w_in, conv_kernel, conv_w_out, router_w, router_b, expert_w_up, expert_b_up, expert_w_down, expert_b_down):
    bp, n_prompt, d = x_prompt.shape
    b, t_new, _ = x_sample.shape
    depth = ada_w.shape[0]
    n_kv = cache_k.shape[3]
    page = cache_k.shape[2]
    n_pages = page_table.shape[1]
    past = n_pages * page
    n_exp = router_w.shape[-1]
    n_sample = b * t_new
    nt = n_prompt + n_sample
    tm = TOKEN_BLOCK
    assert bp == 1 and t_new == SUBLANES and n_prompt % tm == 0 and n_sample % tm == 0 and n_exp <= LANES
    assert d % 512 == 0 and (IDX_HEADS * IDX_DIM) % 512 == 0 and n_prompt % min(KEY_CHUNK, n_prompt) == 0
    npb = n_prompt // tm
    alpha = (2.0 * depth) ** 0.25
    kvw = n_kv * HEAD_DIM
    iw = IDX_HEADS * IDX_DIM

    x_all = jnp.concatenate([x_prompt.reshape(n_prompt, d), x_sample.reshape(n_sample, d)], axis=0)
    c_all = jnp.concatenate([c_prompt, c_sample], axis=0)
    mc = -(-c_all.shape[0] // SUBLANES) * SUBLANES
    c_all = jnp.pad(c_all, ((0, mc - c_all.shape[0]), (0, 0)))
    mod = _ada_mod(c_all, ada_w, ada_b)

    pos = jnp.concatenate([jnp.arange(n_prompt, dtype=I32), jnp.tile(past + jnp.arange(t_new, dtype=I32), b)])
    tabs = _rope_tables(pos, HEAD_DIM) + _rope_tables(pos, IDX_DIM)
    pt_flat = page_table.reshape(-1).astype(I32)
    n_pool = cache_k.shape[1]
    n_cache = cache_k.shape[0] * n_pool
    cache_k2 = cache_k.reshape(n_cache, page * n_kv, HEAD_DIM)
    cache_v2 = cache_v.reshape(n_cache, page * n_kv, HEAD_DIM)
    cache_kit = jnp.swapaxes(cache_kidx.reshape(n_cache, page, IDX_DIM), 1, 2)

    outs = dict(kp=[], vp=[], kip=[], ks=[], vs=[], kis=[], cp=[], cs=[])
    for i in range(depth):
        mb = jnp.concatenate([jnp.broadcast_to(mod[i, 0:1], (tm, 6 * d)),
                              jnp.repeat(mod[i, 1:1 + b], t_new, axis=0)], axis=0)
        if i % 2 == 0:
            a = i // 2
            w_pad = jnp.pad(attn_w_in[a], ((0, 0), (0, LANES - IDX_DIM - IDX_HEADS))).astype(BF16)
            q, k32, v32, kb, vb, qa, qb, kw, ke = _proj_attn(
                x_all, mb, w_pad, tabs, _pad_lanes(attn_kidx_g[a]), _pad_lanes(attn_kidx_b[a]), npb=npb, n_kv=n_kv)
            topk_p = min(TOPK_MAX, n_prompt // TOPK_DIV)
            o_p = _dsa_prompt(q, qa, qb, kw, ke[:n_prompt], kb[:n_prompt], vb[:n_prompt],
                              n_prompt=n_prompt, topk=topk_p, n_kv=n_kv)
            seq3 = lambda z: z[n_prompt:].astype(F32).reshape(b, t_new, z.shape[1])
            keys = _dsa_sample_scores(pt_flat, seq3(qa), seq3(qb), seq3(kw), seq3(ke),
                                      cache_kit, n_pages=n_pages, page0=a * n_pool)
            topk_s = min(TOPK_MAX, (past + t_new) // TOPK_DIV)
            thr = _thresholds(keys.reshape(n_sample, -1), topk=topk_s)
            o_s = _dsa_sample_attn(pt_flat, seq3(q), keys, thr.reshape(b, t_new, LANES), seq3(k32), seq3(v32),
                                   cache_k2, cache_v2, n_pages=n_pages, n_kv=n_kv, page0=a * n_pool)
            mix_in = jnp.concatenate([o_p, o_s.reshape(n_sample, d).astype(BF16)], axis=0)
            w_out = attn_w_out[a].astype(BF16)
            outs['kp'].append(k32[:n_prompt].reshape(1, n_prompt, n_kv, HEAD_DIM))
            outs['vp'].append(v32[:n_prompt].reshape(1, n_prompt, n_kv, HEAD_DIM))
            outs['kip'].append(kw[:n_prompt, :IDX_DIM].reshape(1, n_prompt, IDX_DIM))
            outs['ks'].append(k32[n_prompt:].reshape(b, t_new, n_kv, HEAD_DIM))
            outs['vs'].append(v32[n_prompt:].reshape(b, t_new, n_kv, HEAD_DIM))
            outs['kis'].append(kw[n_prompt:, :IDX_DIM].reshape(b, t_new, IDX_DIM))
        else:
            ci = i // 2
            bgate, v = _proj_conv(x_all, mb, conv_w_in[ci].astype(BF16), npb=npb)
            zrow = jnp.zeros((1, d), F32)
            prev1 = jnp.concatenate([zrow, v[tm - 1:n_prompt - 1:tm]], axis=0)
            prev2 = jnp.concatenate([zrow, v[tm - 2:n_prompt - 2:tm]], axis=0)
            mix_p = _conv_mix(v, bgate, jnp.repeat(prev1, SUBLANES, axis=0), jnp.repeat(prev2, SUBLANES, axis=0),
                              conv_kernel[ci], period=tm, row0_blocks=0, n_rows=n_prompt, prev_per_block=True)
            st = state_conv[ci]
            mix_s = _conv_mix(v, bgate, jnp.repeat(st[:, 1], t_new, axis=0), jnp.repeat(st[:, 0], t_new, axis=0),
                              conv_kernel[ci], period=t_new, row0_blocks=npb, n_rows=n_sample, prev_per_block=False)
            mix_in = jnp.concatenate([mix_p, mix_s], axis=0)
            w_out = conv_w_out[ci].astype(BF16)
            outs['cp'].append(v[n_prompt - (CONV_W - 1):n_prompt].reshape(1, CONV_W - 1, d))
            outs['cs'].append(v[n_prompt:].reshape(b, t_new, d)[:, t_new - (CONV_W - 1):])

        rw = jnp.pad(router_w[i], ((0, 0), (0, LANES - n_exp)))
        rwh = rw.astype(BF16)
        rwl = (rw - rwh.astype(F32)).astype(BF16)
        rb = _pad_lanes(router_b[i], fill=-1e30)
        x1, u2, ti, tg = _post_mix(mix_in, x_all, mb, w_out, ln_g[i, 0:1], ln_b[i, 0:1], rwh, rwl, rb,
                                   npb=npb, alpha=alpha)
        slot_tok, out_row_of_assign, plan = _moe_plan(ti[:, :TOP_K], n_exp)
        xs = _gather_rows(u2, slot_tok)
        ys = _experts(xs, plan, expert_w_up, expert_b_up, expert_w_down, expert_b_down, layer=i)
        y4 = _gather_rows(ys, out_row_of_assign.reshape(nt, TOP_K).T.reshape(-1))
        x_all = _combine(y4, tg, x1, mb, ln_g[i, 1:2], ln_b[i, 1:2], npb=npb, alpha=alpha)

    y_prompt = x_all[:n_prompt].reshape(1, n_prompt, d)
    y_sample = x_all[n_prompt:].reshape(b, t_new, d)
    st = lambda name: jnp.stack(outs[name])
    return (y_prompt, y_sample, st('kp'), st('vp'), st('kip'), st('ks'), st('vs'), st('kis'), st('cp'), st('cs'))
```

```python
import functools

import jax
import jax.numpy as jnp
from jax import lax
from jax.experimental import pallas as pl
from jax.experimental.pallas import tpu as pltpu
from jax.experimental.pallas import tpu_sc as plsc

F32 = jnp.float32
BF16 = jnp.bfloat16
I32 = jnp.int32

HEAD_DIM = 128
IDX_HEADS = 16
IDX_DIM = 64
TOPK_MAX = 256
TOPK_DIV = 4
CONV_W = 3
TOP_K = 4
SWIGLU_LIMIT = 7.0
SWIGLU_ALPHA = 1.702
ROPE_THETA = 10000.0
LN_EPS = 1e-5

LANES = 128
SUBLANES = 8
SC_CORES = 2
SC_SUBCORES = 16
V7X_VMEM_BYTES = 64 * 1024 * 1024
VMEM_CAP_BYTES = V7X_VMEM_BYTES - 8 * 1024 * 1024

TOKEN_BLOCK = 256
Q_TILE = 128
KEY_CHUNK = 512
SEL_ROWS = 256
MOE_ROWS = 128
MOE_SUB = 10
MOE_FF_TILE = 256
MOE_OUT_TILE = 512
GATHER_BUF_BYTES = 128 * 1024
ADA_N_TILE = 1024

INT_MIN = -2 ** 31
HI16_MASK = -65536
LOG2_E = 1.4426950408889634
QK_SCALE_LOG2 = (HEAD_DIM ** -0.5) * LOG2_E
KEY_NEG_INF = -2139095041
NEG_BIAS = -2e30
M_FLOOR = -1e30


def _cparams(sem, est_bytes):
    limit = int(min(max(est_bytes, 16 * 1024 * 1024), VMEM_CAP_BYTES))
    return pltpu.CompilerParams(dimension_semantics=sem, vmem_limit_bytes=limit)


def _dot(a, b):
    return jnp.dot(a, b, preferred_element_type=F32)


def _dot_nt(a, b):
    return lax.dot_general(a, b, (((1,), (1,)), ((), ())), preferred_element_type=F32)


def _rep_lanes(x, n):
    return x if n == 1 else jnp.concatenate([x] * n, axis=1)


def _layer_norm_rows(y, g, b):
    mu = jnp.mean(y, axis=-1, keepdims=True)
    d = y - mu
    var = jnp.mean(d * d, axis=-1, keepdims=True)
    return d * lax.rsqrt(var + LN_EPS) * g + b


def _sort_key(s):
    bits = lax.bitcast_convert_type(s, I32)
    return bits ^ (jnp.right_shift(bits, 31) & 0x7FFFFFFF)


def _ada_kernel(c_ref, w_ref, b_ref, o_ref):
    c = c_ref[...]
    a = (c * jax.nn.sigmoid(c)).astype(BF16)
    o_ref[0] = _dot(a, w_ref[0].astype(BF16)) + b_ref[0]


def _ada_mod(c_all, ada_w, ada_b):
    depth, d, n6 = ada_w.shape
    mc = c_all.shape[0]
    tn = ADA_N_TILE
    est = 2 * (d * tn * 4 + mc * tn * 4) + mc * d * 4 * 2 + d * tn * 2
    return pl.pallas_call(
        _ada_kernel,
        grid=(depth, n6 // tn),
        in_specs=[
            pl.BlockSpec((mc, d), lambda l, j: (0, 0)),
            pl.BlockSpec((1, d, tn), lambda l, j: (l, 0, j)),
            pl.BlockSpec((1, 1, tn), lambda l, j: (l, 0, j)),
        ],
        out_specs=pl.BlockSpec((1, mc, tn), lambda l, j: (l, 0, j)),
        out_shape=jax.ShapeDtypeStruct((depth, mc, n6), F32),
        compiler_params=_cparams(("parallel", "parallel"), est),
        name="ada_mod",
    )(c_all, ada_w, ada_b.reshape(depth, 1, n6))


def _proj_attn_kernel(x_ref, sh_ref, sc_ref, w_ref, c128_ref, s128_ref, c64_ref, s64_ref, kng_ref, knb_ref,
                      q_ref, k_ref, v_ref, kb_ref, vb_ref, qa_ref, qb_ref, kw_ref, ke_ref, *, d_model, n_kv):
    xb = (x_ref[...] * (1.0 + sc_ref[...]) + sh_ref[...]).astype(BF16)
    c128, s128 = c128_ref[...], s128_ref[...]
    c64, s64 = c64_ref[...], s64_ref[...]
    tm = xb.shape[0]
    lane = lax.broadcasted_iota(I32, (tm, LANES), 1)
    low_half = (lane % IDX_DIM) < (IDX_DIM // 2)

    def rope128(y):
        return y * c128 + pltpu.roll(y, HEAD_DIM // 2, 1) * s128

    def rope64(y):
        rot = jnp.where(low_half, pltpu.roll(y, LANES - IDX_DIM // 2, 1), pltpu.roll(y, IDX_DIM // 2, 1))
        return y * c64 + rot * s64

    kvw = n_kv * HEAD_DIM
    col = 0
    for c0 in range(0, d_model, 512):
        y = _dot(xb, w_ref[:, col + c0:col + c0 + 512])
        for t in range(4):
            r = rope128(y[:, t * LANES:(t + 1) * LANES]) * QK_SCALE_LOG2
            q_ref[:, c0 + t * LANES:c0 + (t + 1) * LANES] = r.astype(BF16)
    col += d_model
    y = _dot(xb, w_ref[:, col:col + kvw])
    for t in range(n_kv):
        r = rope128(y[:, t * LANES:(t + 1) * LANES])
        k_ref[:, t * LANES:(t + 1) * LANES] = r
        kb_ref[:, t * LANES:(t + 1) * LANES] = r.astype(BF16)
    col += kvw
    y = _dot(xb, w_ref[:, col:col + kvw])
    v_ref[...] = y
    vb_ref[...] = y.astype(BF16)
    col += kvw
    iw = IDX_HEADS * IDX_DIM
    for c0 in range(0, iw, 512):
        y = _dot(xb, w_ref[:, col + c0:col + c0 + 512])
        for t in range(4):
            r = rope64(y[:, t * LANES:(t + 1) * LANES])
            qa_ref[:, c0 + t * LANES:c0 + (t + 1) * LANES] = r.astype(BF16)
            qb_ref[:, c0 + t * LANES:c0 + (t + 1) * LANES] = pltpu.roll(r, IDX_DIM, 1).astype(BF16)
    col += iw
    y = _dot(xb, w_ref[:, col:col + LANES])
    is_key = lane < IDX_DIM
    mu = jnp.sum(jnp.where(is_key, y, 0.0), axis=-1, keepdims=True) * (1.0 / IDX_DIM)
    dlt = jnp.where(is_key, y - mu, 0.0)
    var = jnp.sum(dlt * dlt, axis=-1, keepdims=True) * (1.0 / IDX_DIM)
    kn = dlt * lax.rsqrt(var + LN_EPS) * kng_ref[...] + knb_ref[...]
    ki = rope64(kn)
    wscale = (IDX_HEADS ** -0.5) * (IDX_DIM ** -0.5)
    wh = jnp.where((lane >= IDX_DIM) & (lane < IDX_DIM + IDX_HEADS), y * wscale, 0.0)
    kw_ref[...] = ki + wh
    ke_ref[...] = ki.astype(BF16)


def _mod_spec(tm, d, col, npb):
    return pl.BlockSpec((tm, d), lambda i: (jnp.where(i < npb, 0, i - npb + 1), col))


def _proj_attn(x_all, mb, w_pad, tabs, kng, knb, *, npb, n_kv):
    nt, d = x_all.shape
    tm = TOKEN_BLOCK
    kvw = n_kv * HEAD_DIM
    iw = IDX_HEADS * IDX_DIM
    row = lambda w: pl.BlockSpec((tm, w), lambda i: (i, 0))
    tab = pl.BlockSpec((tm, LANES), lambda i: (i, 0))
    vec = pl.BlockSpec((1, LANES), lambda i: (0, 0))
    out_shapes = (
        jax.ShapeDtypeStruct((nt, d), BF16),
        jax.ShapeDtypeStruct((nt, kvw), F32),
        jax.ShapeDtypeStruct((nt, kvw), F32),
        jax.ShapeDtypeStruct((nt, kvw), BF16),
        jax.ShapeDtypeStruct((nt, kvw), BF16),
        jax.ShapeDtypeStruct((nt, iw), BF16),
        jax.ShapeDtypeStruct((nt, iw), BF16),
        jax.ShapeDtypeStruct((nt, LANES), F32),
        jax.ShapeDtypeStruct((nt, LANES), BF16),
    )
    out_specs = (row(d), row(kvw), row(kvw), row(kvw), row(kvw), row(iw), row(iw), row(LANES), row(LANES))
    est = w_pad.size * 2 + 2 * tm * (3 * d * 4 + 4 * LANES * 4) + 2 * tm * (d * 2 + kvw * 12 + iw * 4 + LANES * 6) \
        + 8 * tm * 512 * 4
    return pl.pallas_call(
        functools.partial(_proj_attn_kernel, d_model=d, n_kv=n_kv),
        grid=(nt // tm,),
        in_specs=[row(d), _mod_spec(tm, d, 0, npb), _mod_spec(tm, d, 1, npb),
                  pl.BlockSpec(memory_space=pltpu.VMEM), tab, tab, tab, tab, vec, vec],
        out_specs=out_specs,
        out_shape=out_shapes,
        compiler_params=_cparams(("parallel",), est),
        name="proj_attn",
    )(x_all, mb, mb, w_pad, *tabs, kng, knb)


def _kth_largest_key(key_ref, n_chunks, chunk, kk):
    rows = key_ref.shape[0]

    def count_ge(cand):
        candb = jnp.broadcast_to(cand, (rows, LANES))

        def body(c, acc):
            base = pl.multiple_of(c * chunk, chunk)
            for j in range(chunk // LANES):
                blk = key_ref[:, pl.ds(base + j * LANES, LANES)]
                acc = acc + jnp.where(blk >= candb, 1.0, 0.0)
            return acc

        acc = lax.fori_loop(0, n_chunks, body, jnp.zeros((rows, LANES), F32))
        return jnp.sum(acc, axis=1, keepdims=True)

    kkf = float(kk)
    c0 = count_ge(jnp.zeros((rows, 1), I32))
    prefix = jnp.where(c0 >= kkf, 0, INT_MIN).astype(I32)
    n_all = float(n_chunks * chunk) if isinstance(n_chunks, int) else (n_chunks * chunk).astype(F32)
    cnt = jnp.where(c0 >= kkf, c0, n_all)

    def unsettled(cnt):
        return jnp.max(jnp.abs(cnt - kkf)).astype(I32)

    def cond(carry):
        b, _, _, open_rows = carry
        return (b < 31) & (open_rows > 0)

    def bit_body(carry):
        b, prefix, cnt, _ = carry
        cand = prefix | jnp.left_shift(jnp.int32(1), 30 - b)
        c = count_ge(cand)
        keep = c >= kkf
        cnt = jnp.where(keep, c, cnt)
        return b + 1, jnp.where(keep, cand, prefix), cnt, unsettled(cnt)

    _, prefix, _, _ = lax.while_loop(cond, bit_body, (jnp.int32(0), prefix, cnt, unsettled(cnt)))
    return prefix


def _dsa_prompt_kernel(q_ref, qa_ref, qb_ref, kw_ref, ke_ref, k_ref, v_ref, o_ref,
                       key_ref, whb_ref, qs_ref, m_ref, l_ref, acc_ref, *, tq, tc, topk, n_kv, n_rep):
    i = pl.program_id(0)
    q0 = i * tq
    n_ch = (q0 + tq + tc - 1) // tc
    kw = kw_ref[...]
    for h in range(IDX_HEADS):
        whb_ref[h] = jnp.broadcast_to(kw[:, IDX_DIM + h:IDX_DIM + h + 1], (tq, LANES))
    qpos = q0 + lax.broadcasted_iota(I32, (tq, LANES), 0)
    lane = lax.broadcasted_iota(I32, (tq, LANES), 1)
    n_sub = tc // LANES

    def score_body(c, carry):
        base = pl.multiple_of(c * tc, tc)
        ke = ke_ref[pl.ds(base, tc), :]
        acc = [jnp.zeros((tq, LANES), F32) for _ in range(n_sub)]
        for p in range(IDX_HEADS // 2):
            d_even = _dot_nt(qa_ref[:, p * LANES:(p + 1) * LANES], ke)
            d_odd = _dot_nt(qb_ref[:, p * LANES:(p + 1) * LANES], ke)
            w_even, w_odd = whb_ref[2 * p], whb_ref[2 * p + 1]
            for j in range(n_sub):
                sl = slice(j * LANES, (j + 1) * LANES)
                acc[j] = acc[j] + jnp.maximum(d_even[:, sl], 0.0) * w_even + jnp.maximum(d_odd[:, sl], 0.0) * w_odd
        for j in range(n_sub):
            kpos = base + j * LANES + lane
            key_ref[:, pl.ds(base + j * LANES, LANES)] = jnp.where(kpos <= qpos, _sort_key(acc[j]), INT_MIN)
        return carry

    lax.fori_loop(0, n_ch, score_body, 0)

    thr = jnp.maximum(_kth_largest_key(key_ref, n_ch, tc, topk), KEY_NEG_INF + 1)
    thrb = jnp.broadcast_to(thr, (tq, LANES))
    neg_bits = lax.bitcast_convert_type(jnp.float32(NEG_BIAS), I32)

    def bias_body(c, carry):
        base = pl.multiple_of(c * tc, tc)
        for j in range(n_sub):
            sl = pl.ds(base + j * LANES, LANES)
            key_ref[:, sl] = jnp.where(key_ref[:, sl] >= thrb, 0, neg_bits)
        return carry

    lax.fori_loop(0, n_ch, bias_body, 0)

    for g in range(n_kv):
        qs_ref[g] = jnp.concatenate(
            [q_ref[:, (g * n_rep + r) * LANES:(g * n_rep + r + 1) * LANES] for r in range(n_rep)], axis=0)
    m_ref[...] = jnp.full(m_ref.shape, M_FLOOR, F32)
    l_ref[...] = jnp.zeros(l_ref.shape, F32)
    acc_ref[...] = jnp.zeros(acc_ref.shape, F32)

    def att_body(c, carry):
        base = pl.multiple_of(c * tc, tc)
        bias = lax.bitcast_convert_type(key_ref[:, pl.ds(base, tc)], F32)
        bias = jnp.concatenate([bias] * n_rep, axis=0)
        for g in range(n_kv):
            kc = k_ref[pl.ds(base, tc), g * LANES:(g + 1) * LANES]
            vc = v_ref[pl.ds(base, tc), g * LANES:(g + 1) * LANES]
            s = _dot_nt(qs_ref[g], kc) + bias
            m_old = m_ref[g]
            m_new = jnp.maximum(m_old, jnp.max(s, axis=1, keepdims=True))
            alpha = jnp.exp2(m_old - m_new)
            p = jnp.exp2(s - _rep_lanes(m_new, n_sub))
            l_ref[g] = alpha * l_ref[g] + jnp.sum(p, axis=1, keepdims=True)
            acc_ref[g] = alpha * acc_ref[g] + _dot(p.astype(BF16), vc)
            m_ref[g] = m_new
        return carry

    lax.fori_loop(0, n_ch, att_body, 0)
    for g in range(n_kv):
        o = acc_ref[g] / l_ref[g]
        for r in range(n_rep):
            h = g * n_rep + r
            o_ref[:, h * LANES:(h + 1) * LANES] = o[r * tq:(r + 1) * tq].astype(BF16)


def _dsa_prompt(q, qa, qb, kw, ke, kb, vb, *, n_prompt, topk, n_kv):
    d = q.shape[1]
    n_rep = d // HEAD_DIM // n_kv
    tq, tc = Q_TILE, min(KEY_CHUNK, n_prompt)
    iw = IDX_HEADS * IDX_DIM
    kvw = n_kv * HEAD_DIM
    whole = pl.BlockSpec(memory_space=pltpu.VMEM)
    row = lambda w: pl.BlockSpec((tq, w), lambda i: (i, 0))
    est = n_prompt * (LANES * 2 + kvw * 4) + tq * n_prompt * 4 + IDX_HEADS * tq * LANES * 4 \
        + 3 * n_rep * tq * LANES * 4 + 2 * tq * (2 * d * 2 + 2 * iw * 2 + LANES * 4) + 10 * n_rep * tq * tc * 4
    return pl.pallas_call(
        functools.partial(_dsa_prompt_kernel, tq=tq, tc=tc, topk=topk, n_kv=n_kv, n_rep=n_rep),
        grid=(n_prompt // tq,),
        in_specs=[row(d), row(iw), row(iw), row(LANES), whole, whole, whole],
        out_specs=row(d),
        out_shape=jax.ShapeDtypeStruct((n_prompt, d), BF16),
        scratch_shapes=[
            pltpu.VMEM((tq, n_prompt), I32),
            pltpu.VMEM((IDX_HEADS, tq, LANES), F32),
            pltpu.VMEM((n_kv, n_rep * tq, LANES), BF16),
            pltpu.VMEM((n_kv, n_rep * tq, LANES), F32),
            pltpu.VMEM((n_kv, n_rep * tq, LANES), F32),
            pltpu.VMEM((n_kv, n_rep * tq, LANES), F32),
        ],
        compiler_params=_cparams(("parallel",), est),
        name="dsa_prompt",
    )(q, qa, qb, kw, ke, kb, vb)


def _dsa_sample_score_kernel(pt_ref, qa_ref, qb_ref, kw_ref, ken_ref, *refs, n_pages, page, t_new):
    page_refs, key_ref = refs[:n_pages], refs[n_pages]
    qa = qa_ref[0]
    qb = qb_ref[0]
    kw = kw_ref[0]
    n_pair = IDX_HEADS // 2
    q_even = jnp.concatenate([qa[:, p * LANES:(p + 1) * LANES] for p in range(n_pair)], axis=0).astype(BF16)
    q_odd = jnp.concatenate([qb[:, p * LANES:(p + 1) * LANES] for p in range(n_pair)], axis=0).astype(BF16)
    w_even = [jnp.broadcast_to(kw[:, IDX_DIM + 2 * p:IDX_DIM + 2 * p + 1], (t_new, LANES)) for p in range(n_pair)]
    w_odd = [jnp.broadcast_to(kw[:, IDX_DIM + 2 * p + 1:IDX_DIM + 2 * p + 2], (t_new, LANES)) for p in range(n_pair)]

    def scores(d_even, d_odd):
        s = jnp.zeros((t_new, LANES), F32)
        for p in range(n_pair):
            s = s + jnp.maximum(d_even[p * t_new:(p + 1) * t_new], 0.0) * w_even[p] \
                  + jnp.maximum(d_odd[p * t_new:(p + 1) * t_new], 0.0) * w_odd[p]
        return s

    zeros = jnp.zeros((LANES - IDX_DIM, page), F32)
    for pg in range(n_pages):
        ket = jnp.concatenate([page_refs[pg][0], zeros], axis=0).astype(BF16)
        key_ref[0, :, pg * page:(pg + 1) * page] = _sort_key(scores(_dot(q_even, ket), _dot(q_odd, ket)))
    ke_new = jnp.concatenate([ken_ref[0], jnp.zeros((LANES - t_new, LANES), F32)], axis=0).astype(BF16)
    s_new = scores(_dot_nt(q_even, ke_new), _dot_nt(q_odd, ke_new))
    qi = lax.broadcasted_iota(I32, (t_new, LANES), 0)
    kj = lax.broadcasted_iota(I32, (t_new, LANES), 1)
    key_ref[0, :, n_pages * page:n_pages * page + LANES] = jnp.where(kj <= qi, _sort_key(s_new), INT_MIN)


def _dsa_sample_scores(pt_flat, qa_s, qb_s, kw_s, ke_new, cache_kit, *, n_pages, page0):
    b, t_new, iw = qa_s.shape
    page = cache_kit.shape[2]
    width = n_pages * page + LANES
    seq = lambda w: pl.BlockSpec((1, t_new, w), lambda s, pt: (s, 0, 0))
    page_specs = [pl.BlockSpec((1, IDX_DIM, page), lambda s, pt, pg=pg: (page0 + pt[s * n_pages + pg], 0, 0))
                  for pg in range(n_pages)]
    est = 2 * (n_pages * page * LANES * 4 + t_new * (2 * iw + 2 * LANES + width) * 4) + 64 * page * LANES * 4
    return pl.pallas_call(
        functools.partial(_dsa_sample_score_kernel, n_pages=n_pages, page=page, t_new=t_new),
        grid_spec=pltpu.PrefetchScalarGridSpec(
            num_scalar_prefetch=1,
            grid=(b,),
            in_specs=[seq(iw), seq(iw), seq(LANES), seq(LANES)] + page_specs,
            out_specs=seq(width),
        ),
        out_shape=jax.ShapeDtypeStruct((b, t_new, width), I32),
        compiler_params=_cparams(("parallel",), est),
        name="dsa_sample_scores",
    )(pt_flat, qa_s, qb_s, kw_s, ke_new, *([cache_kit] * n_pages))


def _threshold_kernel(key_ref, thr_ref, *, topk, n_tiles):
    thr = jnp.maximum(_kth_largest_key(key_ref, n_tiles, LANES, topk), KEY_NEG_INF + 1)
    thr_ref[...] = jnp.broadcast_to(thr, thr_ref.shape)


def _thresholds(keys, *, topk):
    n, width = keys.shape
    tr = min(SEL_ROWS, n)
    est = 2 * tr * (width + LANES) * 4 + 8 * tr * LANES * 4
    return pl.pallas_call(
        functools.partial(_threshold_kernel, topk=topk, n_tiles=width // LANES),
        grid=(n // tr,),
        in_specs=[pl.BlockSpec((tr, width), lambda i: (i, 0))],
        out_specs=pl.BlockSpec((tr, LANES), lambda i: (i, 0)),
        out_shape=jax.ShapeDtypeStruct((n, LANES), I32),
        compiler_params=_cparams(("parallel",), est),
        name="topk_threshold",
    )(keys)


def _dsa_sample_attn_kernel(pt_ref, q_ref, key_ref, thr_ref, kn_ref, vn_ref, *refs,
                            n_pages, page, t_new, n_kv, n_rep):
    k_refs, v_refs = refs[:n_pages], refs[n_pages:2 * n_pages]
    o_ref, kall_ref, vall_ref = refs[2 * n_pages:]
    n_heads = n_kv * n_rep
    kvw = n_kv * HEAD_DIM
    past = n_pages * page
    for pg in range(n_pages):
        for g in range(n_kv):
            rows_g = pl.ds(g, page, stride=n_kv)
            kall_ref[pg * page:(pg + 1) * page, g * LANES:(g + 1) * LANES] = k_refs[pg][0, rows_g, :].astype(BF16)
            vall_ref[pg * page:(pg + 1) * page, g * LANES:(g + 1) * LANES] = v_refs[pg][0, rows_g, :].astype(BF16)
    pad = jnp.zeros((LANES - t_new, kvw), F32)
    kall_ref[past:past + LANES, :] = jnp.concatenate([kn_ref[0], pad], axis=0).astype(BF16)
    vall_ref[past:past + LANES, :] = jnp.concatenate([vn_ref[0], pad], axis=0).astype(BF16)
    q = q_ref[0]
    zero = jnp.zeros((t_new, LANES), F32)
    rows = []
    for h in range(n_heads):
        g = h // n_rep
        rows.append(jnp.concatenate([q[:, h * LANES:(h + 1) * LANES] if gg == g else zero for gg in range(n_kv)], axis=1))
    qbd = jnp.concatenate(rows, axis=0).astype(BF16)
    thr = thr_ref[0]
    width = past + LANES
    sel = key_ref[0] >= _rep_lanes(thr, width // LANES)
    bias = jnp.where(sel, 0.0, NEG_BIAS)
    s = _dot_nt(qbd, kall_ref[...]) + jnp.concatenate([bias] * n_heads, axis=0)
    m = jnp.maximum(jnp.max(s, axis=1, keepdims=True), M_FLOOR)
    p = jnp.exp2(s - m)
    l = jnp.sum(p, axis=1, keepdims=True)
    o = _dot(p.astype(BF16), vall_ref[...]) / l
    for h in range(n_heads):
        g = h // n_rep
        o_ref[0, :, h * LANES:(h + 1) * LANES] = o[h * t_new:(h + 1) * t_new, g * LANES:(g + 1) * LANES]


def _dsa_sample_attn(pt_flat, q_s, keys, thr, k_new, v_new, cache_k, cache_v, *, n_pages, n_kv, page0):
    b, t_new, d = q_s.shape
    page = cache_k.shape[1] // n_kv
    kvw = n_kv * HEAD_DIM
    n_rep = d // HEAD_DIM // n_kv
    width = n_pages * page + LANES
    seq = lambda w: pl.BlockSpec((1, t_new, w), lambda s, pt: (s, 0, 0))
    page_specs = [pl.BlockSpec((1, page * n_kv, HEAD_DIM), lambda s, pt, pg=pg: (page0 + pt[s * n_pages + pg], 0, 0))
                  for pg in range(n_pages)]
    est = 2 * (2 * n_pages * page * kvw * 4 + t_new * (2 * d + width + LANES + 2 * kvw) * 4) \
        + 2 * width * kvw * 2 + 6 * (d // HEAD_DIM) * t_new * width * 4
    return pl.pallas_call(
        functools.partial(_dsa_sample_attn_kernel, n_pages=n_pages, page=page, t_new=t_new, n_kv=n_kv, n_rep=n_rep),
        grid_spec=pltpu.PrefetchScalarGridSpec(
            num_scalar_prefetch=1,
            grid=(b,),
            in_specs=[seq(d), seq(width), seq(LANES), seq(kvw), seq(kvw)] + page_specs + page_specs,
            out_specs=seq(d),
            scratch_shapes=[pltpu.VMEM((width, kvw), BF16), pltpu.VMEM((width, kvw), BF16)],
        ),
        out_shape=jax.ShapeDtypeStruct((b, t_new, d), F32),
        compiler_params=_cparams(("parallel",), est),
        name="dsa_sample_attn",
    )(pt_flat, q_s, keys, thr, k_new, v_new, *([cache_k] * n_pages), *([cache_v] * n_pages))


def _proj_conv_kernel(x_ref, sh_ref, sc_ref, w_ref, b_ref, v_ref, *, d_model):
    xb = (x_ref[...] * (1.0 + sc_ref[...]) + sh_ref[...]).astype(BF16)
    for c0 in range(0, d_model, 512):
        sl = slice(c0, c0 + 512)
        b_ref[:, sl] = _dot(xb, w_ref[:, c0:c0 + 512])
        cg = _dot(xb, w_ref[:, d_model + c0:d_model + c0 + 512])
        xi = _dot(xb, w_ref[:, 2 * d_model + c0:2 * d_model + c0 + 512])
        v_ref[:, sl] = cg * xi


def _proj_conv(x_all, mb, w_bf, *, npb):
    nt, d = x_all.shape
    tm = TOKEN_BLOCK
    row = pl.BlockSpec((tm, d), lambda i: (i, 0))
    est = w_bf.size * 2 + 2 * tm * d * 4 * 5 + 8 * tm * 512 * 4
    return pl.pallas_call(
        functools.partial(_proj_conv_kernel, d_model=d),
        grid=(nt // tm,),
        in_specs=[row, _mod_spec(tm, d, 0, npb), _mod_spec(tm, d, 1, npb), pl.BlockSpec(memory_space=pltpu.VMEM)],
        out_specs=(row, row),
        out_shape=(jax.ShapeDtypeStruct((nt, d), F32), jax.ShapeDtypeStruct((nt, d), F32)),
        compiler_params=_cparams(("parallel",), est),
        name="proj_conv",
    )(x_all, mb, mb, w_bf)


def _conv_kernel(v_ref, b_ref, p1_ref, p2_ref, k_ref, o_ref, *, period):
    v = v_ref[...]
    tm = v.shape[0]
    t = lax.broadcasted_iota(I32, v.shape, 0) % period
    s1 = jnp.where(t == 0, p1_ref[...], pltpu.roll(v, 1, 0))
    s2 = jnp.where(t == 0, p2_ref[...], jnp.where(t == 1, p1_ref[...], pltpu.roll(v, 2, 0)))
    conv = k_ref[0:1, :] * s2 + k_ref[1:2, :] * s1 + k_ref[2:3, :] * v
    o_ref[...] = (b_ref[...] * conv).astype(BF16)


def _conv_mix(v, bgate, p1, p2, kern, *, period, row0_blocks, n_rows, prev_per_block):
    d = v.shape[1]
    tm = TOKEN_BLOCK
    row = pl.BlockSpec((tm, d), lambda i: (i + row0_blocks, 0))
    if prev_per_block:
        prev = pl.BlockSpec((SUBLANES, d), lambda i: (i, 0))
    else:
        prev = pl.BlockSpec((tm, d), lambda i: (i, 0))
    kpad = jnp.zeros((SUBLANES, d), F32).at[:CONV_W].set(kern)
    est = 2 * tm * d * (4 * 4 + 2) + 8 * tm * d * 4
    return pl.pallas_call(
        functools.partial(_conv_kernel if not prev_per_block else _conv_kernel_blockprev, period=period),
        grid=(n_rows // tm,),
        in_specs=[row, row, prev, prev, pl.BlockSpec((SUBLANES, d), lambda i: (0, 0))],
        out_specs=pl.BlockSpec((tm, d), lambda i: (i, 0)),
        out_shape=jax.ShapeDtypeStruct((n_rows, d), BF16),
        compiler_params=_cparams(("parallel",), est),
        name="conv_mix",
    )(v, bgate, p1, p2, kpad)


def _conv_kernel_blockprev(v_ref, b_ref, p1_ref, p2_ref, k_ref, o_ref, *, period):
    v = v_ref[...]
    t = lax.broadcasted_iota(I32, v.shape, 0) % period
    p1 = jnp.broadcast_to(p1_ref[0:1, :], v.shape)
    p2 = jnp.broadcast_to(p2_ref[0:1, :], v.shape)
    s1 = jnp.where(t == 0, p1, pltpu.roll(v, 1, 0))
    s2 = jnp.where(t == 0, p2, jnp.where(t == 1, p1, pltpu.roll(v, 2, 0)))
    conv = k_ref[0:1, :] * s2 + k_ref[1:2, :] * s1 + k_ref[2:3, :] * v
    o_ref[...] = (b_ref[...] * conv).astype(BF16)


def _post_mix_kernel(a_ref, x_ref, g_ref, sh_ref, sc_ref, w_ref, lng_ref, lnb_ref, rwh_ref, rwl_ref, rb_ref,
                     x1_ref, u_ref, ti_ref, tg_ref, *, alpha):
    mix = _dot(a_ref[...], w_ref[...])
    x1 = _layer_norm_rows(alpha * x_ref[...] + g_ref[...] * mix, lng_ref[...], lnb_ref[...])
    x1_ref[...] = x1
    u = x1 * (1.0 + sc_ref[...]) + sh_ref[...]
    u_hi = u.astype(BF16)
    bits = lax.bitcast_convert_type(u_hi.astype(F32), I32)
    half = bits.shape[1] // 2
    u_ref[...] = (jnp.right_shift(bits[:, :half], 16) & 0xFFFF) | (bits[:, half:] & HI16_MASK)
    u_lo = (u - u_hi.astype(F32)).astype(BF16)
    logits = _dot(u_hi, rwh_ref[...]) + _dot(u_lo, rwh_ref[...]) + _dot(u_hi, rwl_ref[...]) + rb_ref[...]
    tm = logits.shape[0]
    lane = lax.broadcasted_iota(I32, (tm, LANES), 1)
    lane_f = lane.astype(F32)
    ti = jnp.zeros((tm, LANES), I32)
    tv = jnp.full((tm, LANES), -jnp.inf, F32)
    for r in range(TOP_K):
        m = jnp.max(logits, axis=-1, keepdims=True)
        idx = jnp.min(jnp.where(logits == m, lane_f, float(LANES)), axis=-1, keepdims=True)
        hit = lane_f == idx
        ti = jnp.where(lane == r, idx.astype(I32), ti)
        tv = jnp.where(lane == r, m, tv)
        logits = jnp.where(hit, -jnp.inf, logits)
    e = jnp.exp(tv - jnp.max(tv, axis=-1, keepdims=True))
    tg_ref[...] = e / jnp.sum(e, axis=-1, keepdims=True)
    ti_ref[...] = ti


def _post_mix(a, x_all, mb, w_bf, lng, lnb, rwh, rwl, rb, *, npb, alpha):
    nt, d = x_all.shape
    tm = TOKEN_BLOCK
    row = lambda w: pl.BlockSpec((tm, w), lambda i: (i, 0))
    vec = lambda w: pl.BlockSpec((1, w), lambda i: (0, 0))
    whole = pl.BlockSpec(memory_space=pltpu.VMEM)
    est = w_bf.size * 2 + 2 * d * LANES * 2 + 2 * tm * d * (2 + 4 * 4 + 4 + 2) + 4 * tm * LANES * 4 + 8 * tm * d * 4
    return pl.pallas_call(
        functools.partial(_post_mix_kernel, alpha=alpha),
        grid=(nt // tm,),
        in_specs=[row(d), row(d), _mod_spec(tm, d, 2, npb), _mod_spec(tm, d, 3, npb), _mod_spec(tm, d, 4, npb),
                  whole, vec(d), vec(d), whole, whole, vec(LANES)],
        out_specs=(row(d), row(d // 2), row(LANES), row(LANES)),
        out_shape=(jax.ShapeDtypeStruct((nt, d), F32), jax.ShapeDtypeStruct((nt, d // 2), I32),
                   jax.ShapeDtypeStruct((nt, LANES), I32), jax.ShapeDtypeStruct((nt, LANES), F32)),
        compiler_params=_cparams(("parallel",), est),
        name="post_mix",
    )(a, x_all, mb, mb, mb, w_bf, lng, lnb, rwh, rwl, rb)


def _gather_rows(src, idx):
    n, d = src.shape
    m = idx.shape[0]
    n_workers = SC_CORES * SC_SUBCORES
    per_w = m // n_workers
    rows = GATHER_BUF_BYTES // (d * src.dtype.itemsize)
    assert m % n_workers == 0 and per_w % (2 * rows) == 0
    mesh = plsc.VectorSubcoreMesh(core_axis_name="core", subcore_axis_name="subcore")

    @functools.partial(
        pl.kernel, out_type=jax.ShapeDtypeStruct((m, d), src.dtype), mesh=mesh, name="gather_rows",
        scratch_types=[pltpu.VMEM((per_w,), I32), pltpu.VMEM((rows, d), src.dtype), pltpu.VMEM((rows, d), src.dtype),
                       pltpu.SemaphoreType.DMA, pltpu.SemaphoreType.DMA])
    def gather(src_hbm, idx_hbm, dst_hbm, idx_v, buf0, buf1, sem0, sem1):
        base = (lax.axis_index("subcore") * SC_CORES + lax.axis_index("core")) * per_w
        pltpu.sync_copy(idx_hbm.at[pl.ds(base, per_w)], idx_v)

        @pl.loop(0, per_w, step=2 * rows)
        def _(off):
            g0 = pltpu.async_copy(src_hbm.at[idx_v.at[pl.ds(off, rows)]], buf0, sem0)
            g1 = pltpu.async_copy(src_hbm.at[idx_v.at[pl.ds(off + rows, rows)]], buf1, sem1)
            g0.wait()
            w0 = pltpu.async_copy(buf0, dst_hbm.at[pl.ds(base + off, rows)], sem0)
            g1.wait()
            w1 = pltpu.async_copy(buf1, dst_hbm.at[pl.ds(base + off + rows, rows)], sem1)
            w0.wait()
            w1.wait()

    return gather(src, idx)


def _expert_kernel(ie_ref, ib_ref, nv_ref, ob_ref, *refs, n_sub, tf, tn, n_up):
    x_refs = refs[:n_sub]
    wu_ref, bu_ref, wd_ref, bd_ref, o_ref, xb_ref, h_ref, wub_ref, wdp_ref, wdb_ref = refs[n_sub:]
    i, j = pl.program_id(0), pl.program_id(1)
    nv = nv_ref[i]
    half = LANES // 2
    d_half = xb_ref.shape[1] // 2
    n_hid = h_ref.shape[1]
    sub_half = n_sub // 2
    row_halves = [(0, sub_half * MOE_ROWS, 0), (sub_half * MOE_ROWS, n_sub * MOE_ROWS, sub_half)]

    @pl.when((j == 0) & (nv > 0))
    def _():
        for r in range(n_sub):
            rs = slice(r * MOE_ROWS, (r + 1) * MOE_ROWS)
            w = x_refs[r][...]
            xb_ref[rs, :d_half] = lax.bitcast_convert_type(jnp.left_shift(w, 16), F32).astype(BF16)
            xb_ref[rs, d_half:] = lax.bitcast_convert_type(w & HI16_MASK, F32).astype(BF16)

    @pl.when((j < n_up) & (nv > 0))
    def _():
        wub_ref[...] = wu_ref[0, 0].astype(BF16)
        bu = bu_ref[0, 0]
        col = pl.multiple_of(j * tf, tf)

        def up_rows(r0, r1):
            h = _dot(xb_ref[r0:r1, :], wub_ref[...]) + bu
            even = (lax.broadcasted_iota(I32, (r1 - r0, LANES), 1) % 2) == 0
            prods = []
            for t in range(2 * tf // LANES):
                capped = jnp.minimum(h[:, t * LANES:(t + 1) * LANES], SWIGLU_LIMIT)
                gate_act = capped / (1.0 + jnp.exp2(capped * (-SWIGLU_ALPHA * LOG2_E)))
                act = jnp.where(even, gate_act, jnp.maximum(capped, -SWIGLU_LIMIT) + 1.0)
                prods.append(act * pltpu.roll(act, LANES - 1, 1))
            comp = [jnp.where(even, prods[2 * t], pltpu.roll(prods[2 * t + 1], 1, 1))
                    for t in range(tf // LANES)]
            h_ref[r0:r1, pl.ds(col, tf)] = jnp.concatenate(comp, axis=1).astype(BF16)

        @pl.when(nv > sub_half)
        def _():
            for r0, r1, _unused in row_halves:
                up_rows(r0, r1)

        @pl.when(nv <= sub_half)
        def _():
            up_rows(*row_halves[0][:2])

    @pl.when((j >= n_up) & (nv > 0))
    def _():
        for c in range(tn // LANES):
            cs = slice(c * LANES, (c + 1) * LANES)
            for qd in range(n_hid // LANES):
                wdp_ref[pl.ds(qd * LANES, half, stride=2), :] = wd_ref[0, 0, qd * LANES:qd * LANES + half, cs]
                wdp_ref[pl.ds(qd * LANES + 1, half, stride=2), :] = wd_ref[0, 0, qd * LANES + half:(qd + 1) * LANES, cs]
            wdb_ref[:, cs] = wdp_ref[...].astype(BF16)
        bd = bd_ref[0, 0]
        for r0, r1, first_sub in row_halves:
            @pl.when(first_sub < nv)
            def _(r0=r0, r1=r1):
                o_ref[r0:r1, :] = _dot(h_ref[r0:r1, :], wdb_ref[...]) + bd

            @pl.when(first_sub >= nv)
            def _(r0=r0, r1=r1):
                o_ref[r0:r1, :] = jnp.zeros((r1 - r0, tn), F32)


def _experts(xs, plan, w_up, b_up, w_down, b_down, *, layer):
    item_e, item_b0, item_nv, item_ob = plan
    n_items = item_e.shape[0]
    d = w_up.shape[2]
    f = w_down.shape[2]
    tf = min(MOE_FF_TILE, f)
    tn = min(MOE_OUT_TILE, d)
    n_up, n_down = f // tf, d // tn
    n_sub = MOE_SUB
    sb = n_sub * MOE_ROWS
    assert xs.shape[1] * 2 == d and n_sub % 2 == 0

    def x_map(r):
        return lambda i, j, ie, ib, nv, ob: (jnp.where(r < nv[i], ib[i] + r, ib[i]), 0)

    def up_chunk(i, j, nv):
        return jnp.where(nv[i] > 0, jnp.minimum(j, n_up - 1), n_up - 1)

    def down_chunk(i, j, nv):
        return jnp.where(nv[i] > 0, jnp.maximum(j - n_up, 0), n_down - 1)

    in_specs = [pl.BlockSpec((MOE_ROWS, d // 2), x_map(r)) for r in range(n_sub)] + [
        pl.BlockSpec((1, 1, d, 2 * tf), lambda i, j, ie, ib, nv, ob: (layer, ie[i], 0, up_chunk(i, j, nv))),
        pl.BlockSpec((1, 1, 1, 2 * tf), lambda i, j, ie, ib, nv, ob: (layer, ie[i], 0, up_chunk(i, j, nv))),
        pl.BlockSpec((1, 1, f, tn), lambda i, j, ie, ib, nv, ob: (layer, ie[i], 0, down_chunk(i, j, nv))),
        pl.BlockSpec((1, 1, 1, tn), lambda i, j, ie, ib, nv, ob: (layer, ie[i], 0, down_chunk(i, j, nv))),
    ]
    est = 2 * sb * d * 2 + sb * (d + f) * 2 + 2 * (d * 2 * tf + f * tn) * 4 + (d * 2 * tf + f * tn) * 2 \
        + f * LANES * 4 + 2 * sb * tn * 4 + 6 * (sb // 2) * 2 * tf * 4
    depth, n_exp = w_up.shape[0], w_up.shape[1]
    return pl.pallas_call(
        functools.partial(_expert_kernel, n_sub=n_sub, tf=tf, tn=tn, n_up=n_up),
        grid_spec=pltpu.PrefetchScalarGridSpec(
            num_scalar_prefetch=4,
            grid=(n_items, n_up + n_down),
            in_specs=in_specs,
            out_specs=pl.BlockSpec((sb, tn), lambda i, j, ie, ib, nv, ob: (ob[i], down_chunk(i, j, nv))),
            scratch_shapes=[pltpu.VMEM((sb, d), BF16), pltpu.VMEM((sb, f), BF16), pltpu.VMEM((d, 2 * tf), BF16),
                            pltpu.VMEM((f, LANES), F32), pltpu.VMEM((f, tn), BF16)],
        ),
        out_shape=jax.ShapeDtypeStruct((n_items * sb, d), F32),
        compiler_params=_cparams(("arbitrary", "arbitrary"), est),
        name="experts",
    )(item_e, item_b0, item_nv, item_ob, *([xs] * n_sub), w_up,
      b_up.reshape(depth, n_exp, 1, 2 * f), w_down, b_down.reshape(depth, n_exp, 1, d))


def _combine_kernel(*refs, alpha):
    y_refs = refs[:TOP_K]
    tg_ref, x_ref, g_ref, lng_ref, lnb_ref, o_ref = refs[TOP_K:]
    tg = tg_ref[...]
    moe = jnp.zeros(x_ref.shape, F32)
    for r in range(TOP_K):
        moe = moe + tg[:, r:r + 1] * y_refs[r][...]
    o_ref[...] = _layer_norm_rows(alpha * x_ref[...] + g_ref[...] * moe, lng_ref[...], lnb_ref[...])


def _combine(y4, tg, x1, mb, lng, lnb, *, npb, alpha):
    nt, d = x1.shape
    tm = TOKEN_BLOCK
    nbt = nt // tm
    row = lambda w: pl.BlockSpec((tm, w), lambda i: (i, 0))
    vec = pl.BlockSpec((1, d), lambda i: (0, 0))
    y_specs = [pl.BlockSpec((tm, d), lambda i, r=r: (r * nbt + i, 0)) for r in range(TOP_K)]
    est = 2 * tm * (TOP_K * d + LANES + 3 * d) * 4 + 6 * tm * d * 4
    return pl.pallas_call(
        functools.partial(_combine_kernel, alpha=alpha),
        grid=(nbt,),
        in_specs=y_specs + [row(LANES), row(d), _mod_spec(tm, d, 5, npb), vec, vec],
        out_specs=row(d),
        out_shape=jax.ShapeDtypeStruct((nt, d), F32),
        compiler_params=_cparams(("parallel",), est),
        name="moe_combine",
    )(*([y4] * TOP_K), tg, x1, mb, lng, lnb)


def _moe_plan(top_e, n_exp):
    nt = top_e.shape[0]
    na = nt * TOP_K
    sb = MOE_SUB * MOE_ROWS
    e_flat = top_e.reshape(-1)
    order = jnp.argsort(e_flat).astype(I32)
    e_sorted = e_flat[order]
    experts = jnp.arange(n_exp, dtype=I32)
    counts = jnp.sum((e_flat[None, :] == experts[:, None]).astype(I32), axis=1)
    starts = jnp.cumsum(counts) - counts
    padded = (counts + MOE_ROWS - 1) // MOE_ROWS * MOE_ROWS
    gend = jnp.cumsum(padded)
    gstart = gend - padded
    rank = jnp.arange(na, dtype=I32) - starts[e_sorted]
    n_slot_blocks = na // MOE_ROWS + n_exp
    slots = jnp.arange(n_slot_blocks * MOE_ROWS, dtype=I32)
    e_slot = jnp.minimum(jnp.sum((gend[None, :] <= slots[:, None]).astype(I32), axis=1), n_exp - 1)
    pos = slots - gstart[e_slot]
    src = jnp.clip(starts[e_slot] + pos, 0, na - 1)
    slot_tok = jnp.where((pos >= 0) & (pos < counts[e_slot]), order[src] // TOP_K, 0).astype(I32)
    items_e = (counts + sb - 1) // sb
    item_end = jnp.cumsum(items_e)
    item_first = item_end - items_e
    out_row_sorted = (item_first[e_sorted] + rank // sb) * sb + rank % sb
    _, out_row_of_assign = lax.sort((order, out_row_sorted), num_keys=1)
    n_items = na // sb + n_exp
    ids = jnp.arange(n_items, dtype=I32)
    n_real = item_end[-1]
    is_real = ids < n_real
    ids_c = jnp.minimum(ids, n_real - 1).astype(I32)
    e_of = jnp.minimum(jnp.sum((item_end[None, :] <= ids_c[:, None]).astype(I32), axis=1), n_exp - 1)
    s_in_e = ids_c - item_first[e_of]
    rows_left = counts[e_of] - s_in_e * sb
    nvb = jnp.where(is_real, (jnp.minimum(rows_left, sb) + MOE_ROWS - 1) // MOE_ROWS, 0).astype(I32)
    blk0 = ((gstart[e_of] + s_in_e * sb) // MOE_ROWS).astype(I32)
    return slot_tok, out_row_of_assign, (e_of, blk0, nvb, ids_c)


def _rope_tables(pos, width):
    inv = ROPE_THETA ** (-jnp.arange(0, width, 2, dtype=F32) / width)
    ang = pos.astype(F32)[:, None] * inv[None, :]
    c, s = jnp.cos(ang), jnp.sin(ang)
    reps = LANES // width
    return jnp.tile(jnp.concatenate([c, c], axis=-1), (1, reps)), jnp.tile(jnp.concatenate([-s, s], axis=-1), (1, reps))


def _pad_lanes(v, fill=0.0):
    return jnp.full((1, LANES), fill, F32).at[0, :v.shape[0]].set(v)


def kernel(x_prompt, x_sample, cache_k, cache_v, cache_kidx, state_conv, page_table, c_prompt, c_sample, ada_w, ada_b, ln_g, ln_b, attn_w_in, attn_kidx_g, attn_kidx_b, attn_w_out, conv_w_in, conv_kernel, conv_w_out, router_w, router_b, expert_w_up, expert_b_up, expert_w_down, expert_b_down):
    bp, n_prompt, d = x_prompt.shape
    b, t_new, _ = x_sample.shape
    depth = ada_w.shape[0]
    n_kv = cache_k.shape[3]
    page = cache_k.shape[2]
    n_pages = page_table.shape[1]
    past = n_pages * page
    n_exp = router_w.shape[-1]
    n_sample = b * t_new
    nt = n_prompt + n_sample
    tm = TOKEN_BLOCK
    assert bp == 1 and t_new == SUBLANES and n_prompt % tm == 0 and n_sample % tm == 0 and n_exp <= LANES
    assert d % 512 == 0 and (IDX_HEADS * IDX_DIM) % 512 == 0 and n_prompt % min(KEY_CHUNK, n_prompt) == 0
    npb = n_prompt // tm
    alpha = (2.0 * depth) ** 0.25
    kvw = n_kv * HEAD_DIM
    iw = IDX_HEADS * IDX_DIM

    x_all = jnp.concatenate([x_prompt.reshape(n_prompt, d), x_sample.reshape(n_sample, d)], axis=0)
    c_all = jnp.concatenate([c_prompt, c_sample], axis=0)
    mc = -(-c_all.shape[0] // SUBLANES) * SUBLANES
    c_all = jnp.pad(c_all, ((0, mc - c_all.shape[0]), (0, 0)))
    mod = _ada_mod(c_all, ada_w, ada_b)

    pos = jnp.concatenate([jnp.arange(n_prompt, dtype=I32), jnp.tile(past + jnp.arange(t_new, dtype=I32), b)])
    tabs = _rope_tables(pos, HEAD_DIM) + _rope_tables(pos, IDX_DIM)
    pt_flat = page_table.reshape(-1).astype(I32)
    n_pool = cache_k.shape[1]
    n_cache = cache_k.shape[0] * n_pool
    cache_k2 = cache_k.reshape(n_cache, page * n_kv, HEAD_DIM)
    cache_v2 = cache_v.reshape(n_cache, page * n_kv, HEAD_DIM)
    cache_kit = jnp.swapaxes(cache_kidx.reshape(n_cache, page, IDX_DIM), 1, 2)

    outs = dict(kp=[], vp=[], kip=[], ks=[], vs=[], kis=[], cp=[], cs=[])
    for i in range(depth):
        mb = jnp.concatenate([jnp.broadcast_to(mod[i, 0:1], (tm, 6 * d)),
                              jnp.repeat(mod[i, 1:1 + b], t_new, axis=0)], axis=0)
        if i % 2 == 0:
            a = i // 2
            w_pad = jnp.pad(attn_w_in[a], ((0, 0), (0, LANES - IDX_DIM - IDX_HEADS))).astype(BF16)
            q, k32, v32, kb, vb, qa, qb, kw, ke = _proj_attn(
                x_all, mb, w_pad, tabs, _pad_lanes(attn_kidx_g[a]), _pad_lanes(attn_kidx_b[a]), npb=npb, n_kv=n_kv)
            topk_p = min(TOPK_MAX, n_prompt // TOPK_DIV)
            o_p = _dsa_prompt(q, qa, qb, kw, ke[:n_prompt], kb[:n_prompt], vb[:n_prompt],
                              n_prompt=n_prompt, topk=topk_p, n_kv=n_kv)
            seq3 = lambda z: z[n_prompt:].astype(F32).reshape(b, t_new, z.shape[1])
            keys = _dsa_sample_scores(pt_flat, seq3(qa), seq3(qb), seq3(kw), seq3(ke),
                                      cache_kit, n_pages=n_pages, page0=a * n_pool)
            topk_s = min(TOPK_MAX, (past + t_new) // TOPK_DIV)
            thr = _thresholds(keys.reshape(n_sample, -1), topk=topk_s)
            o_s = _dsa_sample_attn(pt_flat, seq3(q), keys, thr.reshape(b, t_new, LANES), seq3(k32), seq3(v32),
                                   cache_k2, cache_v2, n_pages=n_pages, n_kv=n_kv, page0=a * n_pool)
            mix_in = jnp.concatenate([o_p, o_s.reshape(n_sample, d).astype(BF16)], axis=0)
            w_out = attn_w_out[a].astype(BF16)
            outs['kp'].append(k32[:n_prompt].reshape(1, n_prompt, n_kv, HEAD_DIM))
            outs['vp'].append(v32[:n_prompt].reshape(1, n_prompt, n_kv, HEAD_DIM))
            outs['kip'].append(kw[:n_prompt, :IDX_DIM].reshape(1, n_prompt, IDX_DIM))
            outs['ks'].append(k32[n_prompt:].reshape(b, t_new, n_kv, HEAD_DIM))
            outs['vs'].append(v32[n_prompt:].reshape(b, t_new, n_kv, HEAD_DIM))
            outs['kis'].append(kw[n_prompt:, :IDX_DIM].reshape(b, t_new, IDX_DIM))
        else:
            ci = i // 2
            bgate, v = _proj_conv(x_all, mb, conv_w_in[ci].astype(BF16), npb=npb)
            zrow = jnp.zeros((1, d), F32)
            prev1 = jnp.concatenate([zrow, v[tm - 1:n_prompt - 1:tm]], axis=0)
            prev2 = jnp.concatenate([zrow, v[tm - 2:n_prompt - 2:tm]], axis=0)
            mix_p = _conv_mix(v, bgate, jnp.repeat(prev1, SUBLANES, axis=0), jnp.repeat(prev2, SUBLANES, axis=0),
                              conv_kernel[ci], period=tm, row0_blocks=0, n_rows=n_prompt, prev_per_block=True)
            st = state_conv[ci]
            mix_s = _conv_mix(v, bgate, jnp.repeat(st[:, 1], t_new, axis=0), jnp.repeat(st[:, 0], t_new, axis=0),
                              conv_kernel[ci], period=t_new, row0_blocks=npb, n_rows=n_sample, prev_per_block=False)
            mix_in = jnp.concatenate([mix_p, mix_s], axis=0)
            w_out = conv_w_out[ci].astype(BF16)
            outs['cp'].append(v[n_prompt - (CONV_W - 1):n_prompt].reshape(1, CONV_W - 1, d))
            outs['cs'].append(v[n_prompt:].reshape(b, t_new, d)[:, t_new - (CONV_W - 1):])

        rw = jnp.pad(router_w[i], ((0, 0), (0, LANES - n_exp)))
        rwh = rw.astype(BF16)
        rwl = (rw - rwh.astype(F32)).astype(BF16)
        rb = _pad_lanes(router_b[i], fill=-1e30)
        x1, u2, ti, tg = _post_mix(mix_in, x_all, mb, w_out, ln_g[i, 0:1], ln_b[i, 0:1], rwh, rwl, rb,
                                   npb=npb, alpha=alpha)
        slot_tok, out_row_of_assign, plan = _moe_plan(ti[:, :TOP_K], n_exp)
        xs = _gather_rows(u2, slot_tok)
        ys = _experts(xs, plan, expert_w_up, expert_b_up, expert_w_down, expert_b_down, layer=i)
        y4 = _gather_rows(ys, out_row_of_assign.reshape(nt, TOP_K).T.reshape(-1))
        x_all = _combine(y4, tg, x1, mb, ln_g[i, 1:2], ln_b[i, 1:2], npb=npb, alpha=alpha)

    y_prompt = x_all[:n_prompt].reshape(1, n_prompt, d)
    y_sample = x_all[n_prompt:].reshape(b, t_new, d)
    st = lambda name: jnp.stack(outs[name])
    return (y_prompt, y_sample, st('kp'), st('vp'), st('kip'), st('ks'), st('vs'), st('kis'), st('cp'), st('cs'))
```

```python
import functools

import jax
import jax.numpy as jnp
from jax import lax
from jax.experimental import pallas as pl
from jax.experimental.pallas import tpu as pltpu
from jax.experimental.pallas import tpu_sc as plsc

F32 = jnp.float32
BF16 = jnp.bfloat16
I32 = jnp.int32

HEAD_DIM = 128
IDX_HEADS = 16
IDX_DIM = 64
TOPK_MAX = 256
TOPK_DIV = 4
CONV_W = 3
TOP_K = 4
SWIGLU_LIMIT = 7.0
SWIGLU_ALPHA = 1.702
ROPE_THETA = 10000.0
LN_EPS = 1e-5

LANES = 128
SUBLANES = 8
SC_CORES = 2
SC_SUBCORES = 16
V7X_VMEM_BYTES = 64 * 1024 * 1024
VMEM_CAP_BYTES = V7X_VMEM_BYTES - 8 * 1024 * 1024

TOKEN_BLOCK = 256
Q_TILE = 128
KEY_CHUNK = 512
SEL_ROWS = 256
MOE_ROWS = 128
MOE_SUB = 10
MOE_ROW_GROUPS = 3
MOE_FF_TILE = 256
MOE_OUT_TILE = 512
GATHER_BUF_BYTES = 128 * 1024
ADA_N_TILE = 1024

INT_MIN = -2 ** 31
HI16_MASK = -65536
LOG2_E = 1.4426950408889634
QK_SCALE_LOG2 = (HEAD_DIM ** -0.5) * LOG2_E
KEY_NEG_INF = -2139095041
NEG_BIAS = -2e30
M_FLOOR = -1e30


def _cparams(sem, est_bytes):
    limit = int(min(max(est_bytes, 16 * 1024 * 1024), VMEM_CAP_BYTES))
    return pltpu.CompilerParams(dimension_semantics=sem, vmem_limit_bytes=limit)


def _dot(a, b):
    return jnp.dot(a, b, preferred_element_type=F32)


def _dot_nt(a, b):
    return lax.dot_general(a, b, (((1,), (1,)), ((), ())), preferred_element_type=F32)


def _rep_lanes(x, n):
    return x if n == 1 else jnp.concatenate([x] * n, axis=1)


def _layer_norm_rows(y, g, b):
    mu = jnp.mean(y, axis=-1, keepdims=True)
    d = y - mu
    var = jnp.mean(d * d, axis=-1, keepdims=True)
    return d * lax.rsqrt(var + LN_EPS) * g + b


def _sort_key(s):
    bits = lax.bitcast_convert_type(s, I32)
    return bits ^ (jnp.right_shift(bits, 31) & 0x7FFFFFFF)


def _ada_kernel(c_ref, w_ref, b_ref, o_ref):
    c = c_ref[...]
    a = (c * jax.nn.sigmoid(c)).astype(BF16)
    o_ref[0] = _dot(a, w_ref[0].astype(BF16)) + b_ref[0]


def _ada_mod(c_all, ada_w, ada_b):
    depth, d, n6 = ada_w.shape
    mc = c_all.shape[0]
    tn = ADA_N_TILE
    est = 2 * (d * tn * 4 + mc * tn * 4) + mc * d * 4 * 2 + d * tn * 2
    return pl.pallas_call(
        _ada_kernel,
        grid=(depth, n6 // tn),
        in_specs=[
            pl.BlockSpec((mc, d), lambda l, j: (0, 0)),
            pl.BlockSpec((1, d, tn), lambda l, j: (l, 0, j)),
            pl.BlockSpec((1, 1, tn), lambda l, j: (l, 0, j)),
        ],
        out_specs=pl.BlockSpec((1, mc, tn), lambda l, j: (l, 0, j)),
        out_shape=jax.ShapeDtypeStruct((depth, mc, n6), F32),
        compiler_params=_cparams(("parallel", "parallel"), est),
        name="ada_mod",
    )(c_all, ada_w, ada_b.reshape(depth, 1, n6))


def _proj_attn_kernel(x_ref, sh_ref, sc_ref, w_ref, c128_ref, s128_ref, c64_ref, s64_ref, kng_ref, knb_ref,
                      q_ref, k_ref, v_ref, kb_ref, vb_ref, qa_ref, qb_ref, kw_ref, ke_ref, *, d_model, n_kv):
    xb = (x_ref[...] * (1.0 + sc_ref[...]) + sh_ref[...]).astype(BF16)
    c128, s128 = c128_ref[...], s128_ref[...]
    c64, s64 = c64_ref[...], s64_ref[...]
    tm = xb.shape[0]
    lane = lax.broadcasted_iota(I32, (tm, LANES), 1)
    low_half = (lane % IDX_DIM) < (IDX_DIM // 2)

    def rope128(y):
        return y * c128 + pltpu.roll(y, HEAD_DIM // 2, 1) * s128

    def rope64(y):
        rot = jnp.where(low_half, pltpu.roll(y, LANES - IDX_DIM // 2, 1), pltpu.roll(y, IDX_DIM // 2, 1))
        return y * c64 + rot * s64

    kvw = n_kv * HEAD_DIM
    col = 0
    for c0 in range(0, d_model, 512):
        y = _dot(xb, w_ref[:, col + c0:col + c0 + 512])
        for t in range(4):
            r = rope128(y[:, t * LANES:(t + 1) * LANES]) * QK_SCALE_LOG2
            q_ref[:, c0 + t * LANES:c0 + (t + 1) * LANES] = r.astype(BF16)
    col += d_model
    y = _dot(xb, w_ref[:, col:col + kvw])
    for t in range(n_kv):
        r = rope128(y[:, t * LANES:(t + 1) * LANES])
        k_ref[:, t * LANES:(t + 1) * LANES] = r
        kb_ref[:, t * LANES:(t + 1) * LANES] = r.astype(BF16)
    col += kvw
    y = _dot(xb, w_ref[:, col:col + kvw])
    v_ref[...] = y
    vb_ref[...] = y.astype(BF16)
    col += kvw
    iw = IDX_HEADS * IDX_DIM
    for c0 in range(0, iw, 512):
        y = _dot(xb, w_ref[:, col + c0:col + c0 + 512])
        for t in range(4):
            r = rope64(y[:, t * LANES:(t + 1) * LANES])
            qa_ref[:, c0 + t * LANES:c0 + (t + 1) * LANES] = r.astype(BF16)
            qb_ref[:, c0 + t * LANES:c0 + (t + 1) * LANES] = pltpu.roll(r, IDX_DIM, 1).astype(BF16)
    col += iw
    y = _dot(xb, w_ref[:, col:col + LANES])
    is_key = lane < IDX_DIM
    mu = jnp.sum(jnp.where(is_key, y, 0.0), axis=-1, keepdims=True) * (1.0 / IDX_DIM)
    dlt = jnp.where(is_key, y - mu, 0.0)
    var = jnp.sum(dlt * dlt, axis=-1, keepdims=True) * (1.0 / IDX_DIM)
    kn = dlt * lax.rsqrt(var + LN_EPS) * kng_ref[...] + knb_ref[...]
    ki = rope64(kn)
    wscale = (IDX_HEADS ** -0.5) * (IDX_DIM ** -0.5)
    wh = jnp.where((lane >= IDX_DIM) & (lane < IDX_DIM + IDX_HEADS), y * wscale, 0.0)
    kw_ref[...] = ki + wh
    ke_ref[...] = ki.astype(BF16)


def _mod_spec(tm, d, col, npb):
    return pl.BlockSpec((tm, d), lambda i: (jnp.where(i < npb, 0, i - npb + 1), col))


def _proj_attn(x_all, mb, w_pad, tabs, kng, knb, *, npb, n_kv):
    nt, d = x_all.shape
    tm = TOKEN_BLOCK
    kvw = n_kv * HEAD_DIM
    iw = IDX_HEADS * IDX_DIM
    row = lambda w: pl.BlockSpec((tm, w), lambda i: (i, 0))
    tab = pl.BlockSpec((tm, LANES), lambda i: (i, 0))
    vec = pl.BlockSpec((1, LANES), lambda i: (0, 0))
    out_shapes = (
        jax.ShapeDtypeStruct((nt, d), BF16),
        jax.ShapeDtypeStruct((nt, kvw), F32),
        jax.ShapeDtypeStruct((nt, kvw), F32),
        jax.ShapeDtypeStruct((nt, kvw), BF16),
        jax.ShapeDtypeStruct((nt, kvw), BF16),
        jax.ShapeDtypeStruct((nt, iw), BF16),
        jax.ShapeDtypeStruct((nt, iw), BF16),
        jax.ShapeDtypeStruct((nt, LANES), F32),
        jax.ShapeDtypeStruct((nt, LANES), BF16),
    )
    out_specs = (row(d), row(kvw), row(kvw), row(kvw), row(kvw), row(iw), row(iw), row(LANES), row(LANES))
    est = w_pad.size * 2 + 2 * tm * (3 * d * 4 + 4 * LANES * 4) + 2 * tm * (d * 2 + kvw * 12 + iw * 4 + LANES * 6) \
        + 8 * tm * 512 * 4
    return pl.pallas_call(
        functools.partial(_proj_attn_kernel, d_model=d, n_kv=n_kv),
        grid=(nt // tm,),
        in_specs=[row(d), _mod_spec(tm, d, 0, npb), _mod_spec(tm, d, 1, npb),
                  pl.BlockSpec(memory_space=pltpu.VMEM), tab, tab, tab, tab, vec, vec],
        out_specs=out_specs,
        out_shape=out_shapes,
        compiler_params=_cparams(("parallel",), est),
        name="proj_attn",
    )(x_all, mb, mb, w_pad, *tabs, kng, knb)


def _kth_largest_key(key_ref, n_chunks, chunk, kk):
    rows = key_ref.shape[0]

    def count_ge(cand):
        candb = jnp.broadcast_to(cand, (rows, LANES))

        def body(c, acc):
            base = pl.multiple_of(c * chunk, chunk)
            for j in range(chunk // LANES):
                blk = key_ref[:, pl.ds(base + j * LANES, LANES)]
                acc = acc + jnp.where(blk >= candb, 1.0, 0.0)
            return acc

        acc = lax.fori_loop(0, n_chunks, body, jnp.zeros((rows, LANES), F32))
        return jnp.sum(acc, axis=1, keepdims=True)

    kkf = float(kk)
    c0 = count_ge(jnp.zeros((rows, 1), I32))
    prefix = jnp.where(c0 >= kkf, 0, INT_MIN).astype(I32)
    n_all = float(n_chunks * chunk) if isinstance(n_chunks, int) else (n_chunks * chunk).astype(F32)
    cnt = jnp.where(c0 >= kkf, c0, n_all)

    def unsettled(cnt):
        return jnp.max(jnp.abs(cnt - kkf)).astype(I32)

    def cond(carry):
        b, _, _, open_rows = carry
        return (b < 31) & (open_rows > 0)

    def bit_body(carry):
        b, prefix, cnt, _ = carry
        cand = prefix | jnp.left_shift(jnp.int32(1), 30 - b)
        c = count_ge(cand)
        keep = c >= kkf
        cnt = jnp.where(keep, c, cnt)
        return b + 1, jnp.where(keep, cand, prefix), cnt, unsettled(cnt)

    _, prefix, _, _ = lax.while_loop(cond, bit_body, (jnp.int32(0), prefix, cnt, unsettled(cnt)))
    return prefix


def _dsa_prompt_kernel(q_ref, qa_ref, qb_ref, kw_ref, ke_ref, k_ref, v_ref, o_ref,
                       key_ref, whb_ref, qs_ref, m_ref, l_ref, acc_ref, *, tq, tc, topk, n_kv, n_rep):
    i = pl.program_id(0)
    q0 = i * tq
    n_ch = (q0 + tq + tc - 1) // tc
    kw = kw_ref[...]
    for h in range(IDX_HEADS):
        whb_ref[h] = jnp.broadcast_to(kw[:, IDX_DIM + h:IDX_DIM + h + 1], (tq, LANES))
    qpos = q0 + lax.broadcasted_iota(I32, (tq, LANES), 0)
    lane = lax.broadcasted_iota(I32, (tq, LANES), 1)
    n_sub = tc // LANES

    def score_body(c, carry):
        base = pl.multiple_of(c * tc, tc)
        ke = ke_ref[pl.ds(base, tc), :]
        acc = [jnp.zeros((tq, LANES), F32) for _ in range(n_sub)]
        for p in range(IDX_HEADS // 2):
            d_even = _dot_nt(qa_ref[:, p * LANES:(p + 1) * LANES], ke)
            d_odd = _dot_nt(qb_ref[:, p * LANES:(p + 1) * LANES], ke)
            w_even, w_odd = whb_ref[2 * p], whb_ref[2 * p + 1]
            for j in range(n_sub):
                sl = slice(j * LANES, (j + 1) * LANES)
                acc[j] = acc[j] + jnp.maximum(d_even[:, sl], 0.0) * w_even + jnp.maximum(d_odd[:, sl], 0.0) * w_odd
        for j in range(n_sub):
            kpos = base + j * LANES + lane
            key_ref[:, pl.ds(base + j * LANES, LANES)] = jnp.where(kpos <= qpos, _sort_key(acc[j]), INT_MIN)
        return carry

    lax.fori_loop(0, n_ch, score_body, 0)

    thr = jnp.maximum(_kth_largest_key(key_ref, n_ch, tc, topk), KEY_NEG_INF + 1)
    thrb = jnp.broadcast_to(thr, (tq, LANES))
    neg_bits = lax.bitcast_convert_type(jnp.float32(NEG_BIAS), I32)

    def bias_body(c, carry):
        base = pl.multiple_of(c * tc, tc)
        for j in range(n_sub):
            sl = pl.ds(base + j * LANES, LANES)
            key_ref[:, sl] = jnp.where(key_ref[:, sl] >= thrb, 0, neg_bits)
        return carry

    lax.fori_loop(0, n_ch, bias_body, 0)

    for g in range(n_kv):
        qs_ref[g] = jnp.concatenate(
            [q_ref[:, (g * n_rep + r) * LANES:(g * n_rep + r + 1) * LANES] for r in range(n_rep)], axis=0)
    m_ref[...] = jnp.full(m_ref.shape, M_FLOOR, F32)
    l_ref[...] = jnp.zeros(l_ref.shape, F32)
    acc_ref[...] = jnp.zeros(acc_ref.shape, F32)

    def att_body(c, carry):
        base = pl.multiple_of(c * tc, tc)
        bias = lax.bitcast_convert_type(key_ref[:, pl.ds(base, tc)], F32)
        bias = jnp.concatenate([bias] * n_rep, axis=0)
        for g in range(n_kv):
            kc = k_ref[pl.ds(base, tc), g * LANES:(g + 1) * LANES]
            vc = v_ref[pl.ds(base, tc), g * LANES:(g + 1) * LANES]
            s = _dot_nt(qs_ref[g], kc) + bias
            m_old = m_ref[g]
            m_new = jnp.maximum(m_old, jnp.max(s, axis=1, keepdims=True))
            alpha = jnp.exp2(m_old - m_new)
            p = jnp.exp2(s - _rep_lanes(m_new, n_sub))
            l_ref[g] = alpha * l_ref[g] + jnp.sum(p, axis=1, keepdims=True)
            acc_ref[g] = alpha * acc_ref[g] + _dot(p.astype(BF16), vc)
            m_ref[g] = m_new
        return carry

    lax.fori_loop(0, n_ch, att_body, 0)
    for g in range(n_kv):
        o = acc_ref[g] / l_ref[g]
        for r in range(n_rep):
            h = g * n_rep + r
            o_ref[:, h * LANES:(h + 1) * LANES] = o[r * tq:(r + 1) * tq].astype(BF16)


def _dsa_prompt(q, qa, qb, kw, ke, kb, vb, *, n_prompt, topk, n_kv):
    d = q.shape[1]
    n_rep = d // HEAD_DIM // n_kv
    tq, tc = Q_TILE, min(KEY_CHUNK, n_prompt)
    iw = IDX_HEADS * IDX_DIM
    kvw = n_kv * HEAD_DIM
    whole = pl.BlockSpec(memory_space=pltpu.VMEM)
    row = lambda w: pl.BlockSpec((tq, w), lambda i: (i, 0))
    est = n_prompt * (LANES * 2 + kvw * 4) + tq * n_prompt * 4 + IDX_HEADS * tq * LANES * 4 \
        + 3 * n_rep * tq * LANES * 4 + 2 * tq * (2 * d * 2 + 2 * iw * 2 + LANES * 4) + 10 * n_rep * tq * tc * 4
    return pl.pallas_call(
        functools.partial(_dsa_prompt_kernel, tq=tq, tc=tc, topk=topk, n_kv=n_kv, n_rep=n_rep),
        grid=(n_prompt // tq,),
        in_specs=[row(d), row(iw), row(iw), row(LANES), whole, whole, whole],
        out_specs=row(d),
        out_shape=jax.ShapeDtypeStruct((n_prompt, d), BF16),
        scratch_shapes=[
            pltpu.VMEM((tq, n_prompt), I32),
            pltpu.VMEM((IDX_HEADS, tq, LANES), F32),
            pltpu.VMEM((n_kv, n_rep * tq, LANES), BF16),
            pltpu.VMEM((n_kv, n_rep * tq, LANES), F32),
            pltpu.VMEM((n_kv, n_rep * tq, LANES), F32),
            pltpu.VMEM((n_kv, n_rep * tq, LANES), F32),
        ],
        compiler_params=_cparams(("parallel",), est),
        name="dsa_prompt",
    )(q, qa, qb, kw, ke, kb, vb)


def _dsa_sample_score_kernel(pt_ref, qa_ref, qb_ref, kw_ref, ken_ref, *refs, n_pages, page, t_new):
    page_refs, key_ref = refs[:n_pages], refs[n_pages]
    qa = qa_ref[0]
    qb = qb_ref[0]
    kw = kw_ref[0]
    n_pair = IDX_HEADS // 2
    q_even = jnp.concatenate([qa[:, p * LANES:(p + 1) * LANES] for p in range(n_pair)], axis=0).astype(BF16)
    q_odd = jnp.concatenate([qb[:, p * LANES:(p + 1) * LANES] for p in range(n_pair)], axis=0).astype(BF16)
    w_even = [jnp.broadcast_to(kw[:, IDX_DIM + 2 * p:IDX_DIM + 2 * p + 1], (t_new, LANES)) for p in range(n_pair)]
    w_odd = [jnp.broadcast_to(kw[:, IDX_DIM + 2 * p + 1:IDX_DIM + 2 * p + 2], (t_new, LANES)) for p in range(n_pair)]

    def scores(d_even, d_odd):
        s = jnp.zeros((t_new, LANES), F32)
        for p in range(n_pair):
            s = s + jnp.maximum(d_even[p * t_new:(p + 1) * t_new], 0.0) * w_even[p] \
                  + jnp.maximum(d_odd[p * t_new:(p + 1) * t_new], 0.0) * w_odd[p]
        return s

    zeros = jnp.zeros((LANES - IDX_DIM, page), F32)
    for pg in range(n_pages):
        ket = jnp.concatenate([page_refs[pg][0], zeros], axis=0).astype(BF16)
        key_ref[0, :, pg * page:(pg + 1) * page] = _sort_key(scores(_dot(q_even, ket), _dot(q_odd, ket)))
    ke_new = jnp.concatenate([ken_ref[0], jnp.zeros((LANES - t_new, LANES), F32)], axis=0).astype(BF16)
    s_new = scores(_dot_nt(q_even, ke_new), _dot_nt(q_odd, ke_new))
    qi = lax.broadcasted_iota(I32, (t_new, LANES), 0)
    kj = lax.broadcasted_iota(I32, (t_new, LANES), 1)
    key_ref[0, :, n_pages * page:n_pages * page + LANES] = jnp.where(kj <= qi, _sort_key(s_new), INT_MIN)


def _dsa_sample_scores(pt_flat, qa_s, qb_s, kw_s, ke_new, cache_kit, *, n_pages, page0):
    b, t_new, iw = qa_s.shape
    page = cache_kit.shape[2]
    width = n_pages * page + LANES
    seq = lambda w: pl.BlockSpec((1, t_new, w), lambda s, pt: (s, 0, 0))
    page_specs = [pl.BlockSpec((1, IDX_DIM, page), lambda s, pt, pg=pg: (page0 + pt[s * n_pages + pg], 0, 0))
                  for pg in range(n_pages)]
    est = 2 * (n_pages * page * LANES * 4 + t_new * (2 * iw + 2 * LANES + width) * 4) + 64 * page * LANES * 4
    return pl.pallas_call(
        functools.partial(_dsa_sample_score_kernel, n_pages=n_pages, page=page, t_new=t_new),
        grid_spec=pltpu.PrefetchScalarGridSpec(
            num_scalar_prefetch=1,
            grid=(b,),
            in_specs=[seq(iw), seq(iw), seq(LANES), seq(LANES)] + page_specs,
            out_specs=seq(width),
        ),
        out_shape=jax.ShapeDtypeStruct((b, t_new, width), I32),
        compiler_params=_cparams(("parallel",), est),
        name="dsa_sample_scores",
    )(pt_flat, qa_s, qb_s, kw_s, ke_new, *([cache_kit] * n_pages))


def _threshold_kernel(key_ref, thr_ref, *, topk, n_tiles):
    thr = jnp.maximum(_kth_largest_key(key_ref, n_tiles, LANES, topk), KEY_NEG_INF + 1)
    thr_ref[...] = jnp.broadcast_to(thr, thr_ref.shape)


def _thresholds(keys, *, topk):
    n, width = keys.shape
    tr = min(SEL_ROWS, n)
    est = 2 * tr * (width + LANES) * 4 + 8 * tr * LANES * 4
    return pl.pallas_call(
        functools.partial(_threshold_kernel, topk=topk, n_tiles=width // LANES),
        grid=(n // tr,),
        in_specs=[pl.BlockSpec((tr, width), lambda i: (i, 0))],
        out_specs=pl.BlockSpec((tr, LANES), lambda i: (i, 0)),
        out_shape=jax.ShapeDtypeStruct((n, LANES), I32),
        compiler_params=_cparams(("parallel",), est),
        name="topk_threshold",
    )(keys)


def _dsa_sample_attn_kernel(pt_ref, q_ref, key_ref, thr_ref, kn_ref, vn_ref, *refs,
                            n_pages, page, t_new, n_kv, n_rep):
    k_refs, v_refs = refs[:n_pages], refs[n_pages:2 * n_pages]
    o_ref, kall_ref, vall_ref = refs[2 * n_pages:]
    n_heads = n_kv * n_rep
    kvw = n_kv * HEAD_DIM
    past = n_pages * page
    for pg in range(n_pages):
        for g in range(n_kv):
            rows_g = pl.ds(g, page, stride=n_kv)
            kall_ref[pg * page:(pg + 1) * page, g * LANES:(g + 1) * LANES] = k_refs[pg][0, rows_g, :].astype(BF16)
            vall_ref[pg * page:(pg + 1) * page, g * LANES:(g + 1) * LANES] = v_refs[pg][0, rows_g, :].astype(BF16)
    pad = jnp.zeros((LANES - t_new, kvw), F32)
    kall_ref[past:past + LANES, :] = jnp.concatenate([kn_ref[0], pad], axis=0).astype(BF16)
    vall_ref[past:past + LANES, :] = jnp.concatenate([vn_ref[0], pad], axis=0).astype(BF16)
    q = q_ref[0]
    zero = jnp.zeros((t_new, LANES), F32)
    rows = []
    for h in range(n_heads):
        g = h // n_rep
        rows.append(jnp.concatenate([q[:, h * LANES:(h + 1) * LANES] if gg == g else zero for gg in range(n_kv)], axis=1))
    qbd = jnp.concatenate(rows, axis=0).astype(BF16)
    thr = thr_ref[0]
    width = past + LANES
    sel = key_ref[0] >= _rep_lanes(thr, width // LANES)
    bias = jnp.where(sel, 0.0, NEG_BIAS)
    s = _dot_nt(qbd, kall_ref[...]) + jnp.concatenate([bias] * n_heads, axis=0)
    m = jnp.maximum(jnp.max(s, axis=1, keepdims=True), M_FLOOR)
    p = jnp.exp2(s - m)
    l = jnp.sum(p, axis=1, keepdims=True)
    o = _dot(p.astype(BF16), vall_ref[...]) / l
    for h in range(n_heads):
        g = h // n_rep
        o_ref[0, :, h * LANES:(h + 1) * LANES] = o[h * t_new:(h + 1) * t_new, g * LANES:(g + 1) * LANES]


def _dsa_sample_attn(pt_flat, q_s, keys, thr, k_new, v_new, cache_k, cache_v, *, n_pages, n_kv, page0):
    b, t_new, d = q_s.shape
    page = cache_k.shape[1] // n_kv
    kvw = n_kv * HEAD_DIM
    n_rep = d // HEAD_DIM // n_kv
    width = n_pages * page + LANES
    seq = lambda w: pl.BlockSpec((1, t_new, w), lambda s, pt: (s, 0, 0))
    page_specs = [pl.BlockSpec((1, page * n_kv, HEAD_DIM), lambda s, pt, pg=pg: (page0 + pt[s * n_pages + pg], 0, 0))
                  for pg in range(n_pages)]
    est = 2 * (2 * n_pages * page * kvw * 4 + t_new * (2 * d + width + LANES + 2 * kvw) * 4) \
        + 2 * width * kvw * 2 + 6 * (d // HEAD_DIM) * t_new * width * 4
    return pl.pallas_call(
        functools.partial(_dsa_sample_attn_kernel, n_pages=n_pages, page=page, t_new=t_new, n_kv=n_kv, n_rep=n_rep),
        grid_spec=pltpu.PrefetchScalarGridSpec(
            num_scalar_prefetch=1,
            grid=(b,),
            in_specs=[seq(d), seq(width), seq(LANES), seq(kvw), seq(kvw)] + page_specs + page_specs,
            out_specs=seq(d),
            scratch_shapes=[pltpu.VMEM((width, kvw), BF16), pltpu.VMEM((width, kvw), BF16)],
        ),
        out_shape=jax.ShapeDtypeStruct((b, t_new, d), F32),
        compiler_params=_cparams(("parallel",), est),
        name="dsa_sample_attn",
    )(pt_flat, q_s, keys, thr, k_new, v_new, *([cache_k] * n_pages), *([cache_v] * n_pages))


def _proj_conv_kernel(x_ref, sh_ref, sc_ref, w_ref, b_ref, v_ref, *, d_model):
    xb = (x_ref[...] * (1.0 + sc_ref[...]) + sh_ref[...]).astype(BF16)
    for c0 in range(0, d_model, 512):
        sl = slice(c0, c0 + 512)
        b_ref[:, sl] = _dot(xb, w_ref[:, c0:c0 + 512])
        cg = _dot(xb, w_ref[:, d_model + c0:d_model + c0 + 512])
        xi = _dot(xb, w_ref[:, 2 * d_model + c0:2 * d_model + c0 + 512])
        v_ref[:, sl] = cg * xi


def _proj_conv(x_all, mb, w_bf, *, npb):
    nt, d = x_all.shape
    tm = TOKEN_BLOCK
    row = pl.BlockSpec((tm, d), lambda i: (i, 0))
    est = w_bf.size * 2 + 2 * tm * d * 4 * 5 + 8 * tm * 512 * 4
    return pl.pallas_call(
        functools.partial(_proj_conv_kernel, d_model=d),
        grid=(nt // tm,),
        in_specs=[row, _mod_spec(tm, d, 0, npb), _mod_spec(tm, d, 1, npb), pl.BlockSpec(memory_space=pltpu.VMEM)],
        out_specs=(row, row),
        out_shape=(jax.ShapeDtypeStruct((nt, d), F32), jax.ShapeDtypeStruct((nt, d), F32)),
        compiler_params=_cparams(("parallel",), est),
        name="proj_conv",
    )(x_all, mb, mb, w_bf)


def _conv_kernel(v_ref, b_ref, p1_ref, p2_ref, k_ref, o_ref, *, period):
    v = v_ref[...]
    tm = v.shape[0]
    t = lax.broadcasted_iota(I32, v.shape, 0) % period
    s1 = jnp.where(t == 0, p1_ref[...], pltpu.roll(v, 1, 0))
    s2 = jnp.where(t == 0, p2_ref[...], jnp.where(t == 1, p1_ref[...], pltpu.roll(v, 2, 0)))
    conv = k_ref[0:1, :] * s2 + k_ref[1:2, :] * s1 + k_ref[2:3, :] * v
    o_ref[...] = (b_ref[...] * conv).astype(BF16)


def _conv_mix(v, bgate, p1, p2, kern, *, period, row0_blocks, n_rows, prev_per_block):
    d = v.shape[1]
    tm = TOKEN_BLOCK
    row = pl.BlockSpec((tm, d), lambda i: (i + row0_blocks, 0))
    if prev_per_block:
        prev = pl.BlockSpec((SUBLANES, d), lambda i: (i, 0))
    else:
        prev = pl.BlockSpec((tm, d), lambda i: (i, 0))
    kpad = jnp.zeros((SUBLANES, d), F32).at[:CONV_W].set(kern)
    est = 2 * tm * d * (4 * 4 + 2) + 8 * tm * d * 4
    return pl.pallas_call(
        functools.partial(_conv_kernel if not prev_per_block else _conv_kernel_blockprev, period=period),
        grid=(n_rows // tm,),
        in_specs=[row, row, prev, prev, pl.BlockSpec((SUBLANES, d), lambda i: (0, 0))],
        out_specs=pl.BlockSpec((tm, d), lambda i: (i, 0)),
        out_shape=jax.ShapeDtypeStruct((n_rows, d), BF16),
        compiler_params=_cparams(("parallel",), est),
        name="conv_mix",
    )(v, bgate, p1, p2, kpad)


def _conv_kernel_blockprev(v_ref, b_ref, p1_ref, p2_ref, k_ref, o_ref, *, period):
    v = v_ref[...]
    t = lax.broadcasted_iota(I32, v.shape, 0) % period
    p1 = jnp.broadcast_to(p1_ref[0:1, :], v.shape)
    p2 = jnp.broadcast_to(p2_ref[0:1, :], v.shape)
    s1 = jnp.where(t == 0, p1, pltpu.roll(v, 1, 0))
    s2 = jnp.where(t == 0, p2, jnp.where(t == 1, p1, pltpu.roll(v, 2, 0)))
    conv = k_ref[0:1, :] * s2 + k_ref[1:2, :] * s1 + k_ref[2:3, :] * v
    o_ref[...] = (b_ref[...] * conv).astype(BF16)


def _post_mix_kernel(a_ref, x_ref, g_ref, sh_ref, sc_ref, w_ref, lng_ref, lnb_ref, rwh_ref, rwl_ref, rb_ref,
                     x1_ref, u_ref, ti_ref, tg_ref, *, alpha):
    mix = _dot(a_ref[...], w_ref[...])
    x1 = _layer_norm_rows(alpha * x_ref[...] + g_ref[...] * mix, lng_ref[...], lnb_ref[...])
    x1_ref[...] = x1
    u = x1 * (1.0 + sc_ref[...]) + sh_ref[...]
    u_hi = u.astype(BF16)
    bits = lax.bitcast_convert_type(u_hi.astype(F32), I32)
    half = bits.shape[1] // 2
    u_ref[...] = (jnp.right_shift(bits[:, :half], 16) & 0xFFFF) | (bits[:, half:] & HI16_MASK)
    u_lo = (u - u_hi.astype(F32)).astype(BF16)
    logits = _dot(u_hi, rwh_ref[...]) + _dot(u_lo, rwh_ref[...]) + _dot(u_hi, rwl_ref[...]) + rb_ref[...]
    tm = logits.shape[0]
    lane = lax.broadcasted_iota(I32, (tm, LANES), 1)
    lane_f = lane.astype(F32)
    ti = jnp.zeros((tm, LANES), I32)
    tv = jnp.full((tm, LANES), -jnp.inf, F32)
    for r in range(TOP_K):
        m = jnp.max(logits, axis=-1, keepdims=True)
        idx = jnp.min(jnp.where(logits == m, lane_f, float(LANES)), axis=-1, keepdims=True)
        hit = lane_f == idx
        ti = jnp.where(lane == r, idx.astype(I32), ti)
        tv = jnp.where(lane == r, m, tv)
        logits = jnp.where(hit, -jnp.inf, logits)
    e = jnp.exp(tv - jnp.max(tv, axis=-1, keepdims=True))
    tg_ref[...] = e / jnp.sum(e, axis=-1, keepdims=True)
    ti_ref[...] = ti


def _post_mix(a, x_all, mb, w_bf, lng, lnb, rwh, rwl, rb, *, npb, alpha):
    nt, d = x_all.shape
    tm = TOKEN_BLOCK
    row = lambda w: pl.BlockSpec((tm, w), lambda i: (i, 0))
    vec = lambda w: pl.BlockSpec((1, w), lambda i: (0, 0))
    whole = pl.BlockSpec(memory_space=pltpu.VMEM)
    est = w_bf.size * 2 + 2 * d * LANES * 2 + 2 * tm * d * (2 + 4 * 4 + 4 + 2) + 4 * tm * LANES * 4 + 8 * tm * d * 4
    return pl.pallas_call(
        functools.partial(_post_mix_kernel, alpha=alpha),
        grid=(nt // tm,),
        in_specs=[row(d), row(d), _mod_spec(tm, d, 2, npb), _mod_spec(tm, d, 3, npb), _mod_spec(tm, d, 4, npb),
                  whole, vec(d), vec(d), whole, whole, vec(LANES)],
        out_specs=(row(d), row(d // 2), row(LANES), row(LANES)),
        out_shape=(jax.ShapeDtypeStruct((nt, d), F32), jax.ShapeDtypeStruct((nt, d // 2), I32),
                   jax.ShapeDtypeStruct((nt, LANES), I32), jax.ShapeDtypeStruct((nt, LANES), F32)),
        compiler_params=_cparams(("parallel",), est),
        name="post_mix",
    )(a, x_all, mb, mb, mb, w_bf, lng, lnb, rwh, rwl, rb)


def _gather_rows(src, idx):
    n, d = src.shape
    m = idx.shape[0]
    n_workers = SC_CORES * SC_SUBCORES
    per_w = m // n_workers
    rows = GATHER_BUF_BYTES // (d * src.dtype.itemsize)
    assert m % n_workers == 0 and per_w % (2 * rows) == 0
    mesh = plsc.VectorSubcoreMesh(core_axis_name="core", subcore_axis_name="subcore")

    @functools.partial(
        pl.kernel, out_type=jax.ShapeDtypeStruct((m, d), src.dtype), mesh=mesh, name="gather_rows",
        scratch_types=[pltpu.VMEM((per_w,), I32), pltpu.VMEM((rows, d), src.dtype), pltpu.VMEM((rows, d), src.dtype),
                       pltpu.SemaphoreType.DMA, pltpu.SemaphoreType.DMA])
    def gather(src_hbm, idx_hbm, dst_hbm, idx_v, buf0, buf1, sem0, sem1):
        base = (lax.axis_index("subcore") * SC_CORES + lax.axis_index("core")) * per_w
        pltpu.sync_copy(idx_hbm.at[pl.ds(base, per_w)], idx_v)

        @pl.loop(0, per_w, step=2 * rows)
        def _(off):
            g0 = pltpu.async_copy(src_hbm.at[idx_v.at[pl.ds(off, rows)]], buf0, sem0)
            g1 = pltpu.async_copy(src_hbm.at[idx_v.at[pl.ds(off + rows, rows)]], buf1, sem1)
            g0.wait()
            w0 = pltpu.async_copy(buf0, dst_hbm.at[pl.ds(base + off, rows)], sem0)
            g1.wait()
            w1 = pltpu.async_copy(buf1, dst_hbm.at[pl.ds(base + off + rows, rows)], sem1)
            w0.wait()
            w1.wait()

    return gather(src, idx)


def _expert_kernel(ie_ref, ib_ref, nv_ref, ob_ref, *refs, n_sub, tf, tn, n_up):
    x_refs = refs[:n_sub]
    wu_ref, bu_ref, wd_ref, bd_ref, o_ref, xb_ref, h_ref, wub_ref, wdp_ref, wdb_ref = refs[n_sub:]
    i, j = pl.program_id(0), pl.program_id(1)
    nv = nv_ref[i]
    half = LANES // 2
    d_half = xb_ref.shape[1] // 2
    n_hid = h_ref.shape[1]
    n_grp = min(MOE_ROW_GROUPS, n_sub)
    sizes = [n_sub // n_grp + (1 if g < n_sub % n_grp else 0) for g in range(n_grp)]
    firsts = [sum(sizes[:g]) for g in range(n_grp)]
    row_groups = [(f * MOE_ROWS, (f + s) * MOE_ROWS, f) for f, s in zip(firsts, sizes)]
    n_active = sum(jnp.where(nv > f, 1, 0) for f in firsts)

    @pl.when((j == 0) & (nv > 0))
    def _():
        for r in range(n_sub):
            rs = slice(r * MOE_ROWS, (r + 1) * MOE_ROWS)
            w = x_refs[r][...]
            xb_ref[rs, :d_half] = lax.bitcast_convert_type(jnp.left_shift(w, 16), F32).astype(BF16)
            xb_ref[rs, d_half:] = lax.bitcast_convert_type(w & HI16_MASK, F32).astype(BF16)

    @pl.when((j < n_up) & (nv > 0))
    def _():
        wub_ref[...] = wu_ref[0, 0].astype(BF16)
        bu = bu_ref[0, 0]
        col = pl.multiple_of(j * tf, tf)

        def up_rows(r0, r1):
            h = _dot(xb_ref[r0:r1, :], wub_ref[...]) + bu
            even = (lax.broadcasted_iota(I32, (r1 - r0, LANES), 1) % 2) == 0
            prods = []
            for t in range(2 * tf // LANES):
                capped = jnp.minimum(h[:, t * LANES:(t + 1) * LANES], SWIGLU_LIMIT)
                gate_act = capped / (1.0 + jnp.exp2(capped * (-SWIGLU_ALPHA * LOG2_E)))
                act = jnp.where(even, gate_act, jnp.maximum(capped, -SWIGLU_LIMIT) + 1.0)
                prods.append(act * pltpu.roll(act, LANES - 1, 1))
            comp = [jnp.where(even, prods[2 * t], pltpu.roll(prods[2 * t + 1], 1, 1))
                    for t in range(tf // LANES)]
            h_ref[r0:r1, pl.ds(col, tf)] = jnp.concatenate(comp, axis=1).astype(BF16)

        for k in range(1, n_grp + 1):
            @pl.when(n_active == k)
            def _(k=k):
                for r0, r1, _unused in row_groups[:k]:
                    up_rows(r0, r1)

    @pl.when((j >= n_up) & (nv > 0))
    def _():
        for c in range(tn // LANES):
            cs = slice(c * LANES, (c + 1) * LANES)
            for qd in range(n_hid // LANES):
                wdp_ref[pl.ds(qd * LANES, half, stride=2), :] = wd_ref[0, 0, qd * LANES:qd * LANES + half, cs]
                wdp_ref[pl.ds(qd * LANES + 1, half, stride=2), :] = wd_ref[0, 0, qd * LANES + half:(qd + 1) * LANES, cs]
            wdb_ref[:, cs] = wdp_ref[...].astype(BF16)
        bd = bd_ref[0, 0]
        for r0, r1, first_sub in row_groups:
            @pl.when(first_sub < nv)
            def _(r0=r0, r1=r1):
                o_ref[r0:r1, :] = _dot(h_ref[r0:r1, :], wdb_ref[...]) + bd

            @pl.when(first_sub >= nv)
            def _(r0=r0, r1=r1):
                o_ref[r0:r1, :] = jnp.zeros((r1 - r0, tn), F32)


def _experts(xs, plan, w_up, b_up, w_down, b_down, *, layer):
    item_e, item_b0, item_nv, item_ob = plan
    n_items = item_e.shape[0]
    d = w_up.shape[2]
    f = w_down.shape[2]
    tf = min(MOE_FF_TILE, f)
    tn = min(MOE_OUT_TILE, d)
    n_up, n_down = f // tf, d // tn
    n_sub = MOE_SUB
    sb = n_sub * MOE_ROWS
    assert xs.shape[1] * 2 == d and n_sub % 2 == 0

    def x_map(r):
        return lambda i, j, ie, ib, nv, ob: (jnp.where(r < nv[i], ib[i] + r, ib[i]), 0)

    def up_chunk(i, j, nv):
        return jnp.where(nv[i] > 0, jnp.minimum(j, n_up - 1), n_up - 1)

    def down_chunk(i, j, nv):
        return jnp.where(nv[i] > 0, jnp.maximum(j - n_up, 0), n_down - 1)

    in_specs = [pl.BlockSpec((MOE_ROWS, d // 2), x_map(r)) for r in range(n_sub)] + [
        pl.BlockSpec((1, 1, d, 2 * tf), lambda i, j, ie, ib, nv, ob: (layer, ie[i], 0, up_chunk(i, j, nv))),
        pl.BlockSpec((1, 1, 1, 2 * tf), lambda i, j, ie, ib, nv, ob: (layer, ie[i], 0, up_chunk(i, j, nv))),
        pl.BlockSpec((1, 1, f, tn), lambda i, j, ie, ib, nv, ob: (layer, ie[i], 0, down_chunk(i, j, nv))),
        pl.BlockSpec((1, 1, 1, tn), lambda i, j, ie, ib, nv, ob: (layer, ie[i], 0, down_chunk(i, j, nv))),
    ]
    est = 2 * sb * d * 2 + sb * (d + f) * 2 + 2 * (d * 2 * tf + f * tn) * 4 + (d * 2 * tf + f * tn) * 2 \
        + f * LANES * 4 + 2 * sb * tn * 4 + 6 * (sb // 2) * 2 * tf * 4
    depth, n_exp = w_up.shape[0], w_up.shape[1]
    return pl.pallas_call(
        functools.partial(_expert_kernel, n_sub=n_sub, tf=tf, tn=tn, n_up=n_up),
        grid_spec=pltpu.PrefetchScalarGridSpec(
            num_scalar_prefetch=4,
            grid=(n_items, n_up + n_down),
            in_specs=in_specs,
            out_specs=pl.BlockSpec((sb, tn), lambda i, j, ie, ib, nv, ob: (ob[i], down_chunk(i, j, nv))),
            scratch_shapes=[pltpu.VMEM((sb, d), BF16), pltpu.VMEM((sb, f), BF16), pltpu.VMEM((d, 2 * tf), BF16),
                            pltpu.VMEM((f, LANES), F32), pltpu.VMEM((f, tn), BF16)],
        ),
        out_shape=jax.ShapeDtypeStruct((n_items * sb, d), F32),
        compiler_params=_cparams(("arbitrary", "arbitrary"), est),
        name="experts",
    )(item_e, item_b0, item_nv, item_ob, *([xs] * n_sub), w_up,
      b_up.reshape(depth, n_exp, 1, 2 * f), w_down, b_down.reshape(depth, n_exp, 1, d))


def _combine_kernel(*refs, alpha):
    y_refs = refs[:TOP_K]
    tg_ref, x_ref, g_ref, lng_ref, lnb_ref, o_ref = refs[TOP_K:]
    tg = tg_ref[...]
    moe = jnp.zeros(x_ref.shape, F32)
    for r in range(TOP_K):
        moe = moe + tg[:, r:r + 1] * y_refs[r][...]
    o_ref[...] = _layer_norm_rows(alpha * x_ref[...] + g_ref[...] * moe, lng_ref[...], lnb_ref[...])


def _combine(y4, tg, x1, mb, lng, lnb, *, npb, alpha):
    nt, d = x1.shape
    tm = TOKEN_BLOCK
    nbt = nt // tm
    row = lambda w: pl.BlockSpec((tm, w), lambda i: (i, 0))
    vec = pl.BlockSpec((1, d), lambda i: (0, 0))
    y_specs = [pl.BlockSpec((tm, d), lambda i, r=r: (r * nbt + i, 0)) for r in range(TOP_K)]
    est = 2 * tm * (TOP_K * d + LANES + 3 * d) * 4 + 6 * tm * d * 4
    return pl.pallas_call(
        functools.partial(_combine_kernel, alpha=alpha),
        grid=(nbt,),
        in_specs=y_specs + [row(LANES), row(d), _mod_spec(tm, d, 5, npb), vec, vec],
        out_specs=row(d),
        out_shape=jax.ShapeDtypeStruct((nt, d), F32),
        compiler_params=_cparams(("parallel",), est),
        name="moe_combine",
    )(*([y4] * TOP_K), tg, x1, mb, lng, lnb)


def _moe_plan(top_e, n_exp):
    nt = top_e.shape[0]
    na = nt * TOP_K
    sb = MOE_SUB * MOE_ROWS
    e_flat = top_e.reshape(-1)
    order = jnp.argsort(e_flat).astype(I32)
    e_sorted = e_flat[order]
    experts = jnp.arange(n_exp, dtype=I32)
    counts = jnp.sum((e_flat[None, :] == experts[:, None]).astype(I32), axis=1)
    starts = jnp.cumsum(counts) - counts
    padded = (counts + MOE_ROWS - 1) // MOE_ROWS * MOE_ROWS
    gend = jnp.cumsum(padded)
    gstart = gend - padded
    rank = jnp.arange(na, dtype=I32) - starts[e_sorted]
    n_slot_blocks = na // MOE_ROWS + n_exp
    slots = jnp.arange(n_slot_blocks * MOE_ROWS, dtype=I32)
    e_slot = jnp.minimum(jnp.sum((gend[None, :] <= slots[:, None]).astype(I32), axis=1), n_exp - 1)
    pos = slots - gstart[e_slot]
    src = jnp.clip(starts[e_slot] + pos, 0, na - 1)
    slot_tok = jnp.where((pos >= 0) & (pos < counts[e_slot]), order[src] // TOP_K, 0).astype(I32)
    items_e = (counts + sb - 1) // sb
    item_end = jnp.cumsum(items_e)
    item_first = item_end - items_e
    out_row_sorted = (item_first[e_sorted] + rank // sb) * sb + rank % sb
    _, out_row_of_assign = lax.sort((order, out_row_sorted), num_keys=1)
    n_items = na // sb + n_exp
    ids = jnp.arange(n_items, dtype=I32)
    n_real = item_end[-1]
    is_real = ids < n_real
    ids_c = jnp.minimum(ids, n_real - 1).astype(I32)
    e_of = jnp.minimum(jnp.sum((item_end[None, :] <= ids_c[:, None]).astype(I32), axis=1), n_exp - 1)
    s_in_e = ids_c - item_first[e_of]
    rows_left = counts[e_of] - s_in_e * sb
    nvb = jnp.where(is_real, (jnp.minimum(rows_left, sb) + MOE_ROWS - 1) // MOE_ROWS, 0).astype(I32)
    blk0 = ((gstart[e_of] + s_in_e * sb) // MOE_ROWS).astype(I32)
    return slot_tok, out_row_of_assign, (e_of, blk0, nvb, ids_c)


def _rope_tables(pos, width):
    inv = ROPE_THETA ** (-jnp.arange(0, width, 2, dtype=F32) / width)
    ang = pos.astype(F32)[:, None] * inv[None, :]
    c, s = jnp.cos(ang), jnp.sin(ang)
    reps = LANES // width
    return jnp.tile(jnp.concatenate([c, c], axis=-1), (1, reps)), jnp.tile(jnp.concatenate([-s, s], axis=-1), (1, reps))


def _pad_lanes(v, fill=0.0):
    return jnp.full((1, LANES), fill, F32).at[0, :v.shape[0]].set(v)


def kernel(x_prompt, x_sample, cache_k, cache_v, cache_kidx, state_conv, page_table, c_prompt, c_sample, ada_w, ada_b, ln_g, ln_b, attn_w_in, attn_kidx_g, attn_kidx_b, attn_w_out, conv_w_in, conv_kernel, conv_w_out, router_w, router_b, expert_w_up, expert_b_up, expert_w_down, expert_b_down):
    bp, n_prompt, d = x_prompt.shape
    b, t_new, _ = x_sample.shape
    depth = ada_w.shape[0]
    n_kv = cache_k.shape[3]
    page = cache_k.shape[2]
    n_pages = page_table.shape[1]
    past = n_pages * page
    n_exp = router_w.shape[-1]
    n_sample = b * t_new
    nt = n_prompt + n_sample
    tm = TOKEN_BLOCK
    assert bp == 1 and t_new == SUBLANES and n_prompt % tm == 0 and n_sample % tm == 0 and n_exp <= LANES
    assert d % 512 == 0 and (IDX_HEADS * IDX_DIM) % 512 == 0 and n_prompt % min(KEY_CHUNK, n_prompt) == 0
    npb = n_prompt // tm
    alpha = (2.0 * depth) ** 0.25
    kvw = n_kv * HEAD_DIM
    iw = IDX_HEADS * IDX_DIM

    x_all = jnp.concatenate([x_prompt.reshape(n_prompt, d), x_sample.reshape(n_sample, d)], axis=0)
    c_all = jnp.concatenate([c_prompt, c_sample], axis=0)
    mc = -(-c_all.shape[0] // SUBLANES) * SUBLANES
    c_all = jnp.pad(c_all, ((0, mc - c_all.shape[0]), (0, 0)))
    mod = _ada_mod(c_all, ada_w, ada_b)

    pos = jnp.concatenate([jnp.arange(n_prompt, dtype=I32), jnp.tile(past + jnp.arange(t_new, dtype=I32), b)])
    tabs = _rope_tables(pos, HEAD_DIM) + _rope_tables(pos, IDX_DIM)
    pt_flat = page_table.reshape(-1).astype(I32)
    n_pool = cache_k.shape[1]
    n_cache = cache_k.shape[0] * n_pool
    cache_k2 = cache_k.reshape(n_cache, page * n_kv, HEAD_DIM)
    cache_v2 = cache_v.reshape(n_cache, page * n_kv, HEAD_DIM)
    cache_kit = jnp.swapaxes(cache_kidx.reshape(n_cache, page, IDX_DIM), 1, 2)

    outs = dict(kp=[], vp=[], kip=[], ks=[], vs=[], kis=[], cp=[], cs=[])
    for i in range(depth):
        mb = jnp.concatenate([jnp.broadcast_to(mod[i, 0:1], (tm, 6 * d)),
                              jnp.repeat(mod[i, 1:1 + b], t_new, axis=0)], axis=0)
        if i % 2 == 0:
            a = i // 2
            w_pad = jnp.pad(attn_w_in[a], ((0, 0), (0, LANES - IDX_DIM - IDX_HEADS))).astype(BF16)
            q, k32, v32, kb, vb, qa, qb, kw, ke = _proj_attn(
                x_all, mb, w_pad, tabs, _pad_lanes(attn_kidx_g[a]), _pad_lanes(attn_kidx_b[a]), npb=npb, n_kv=n_kv)
            topk_p = min(TOPK_MAX, n_prompt // TOPK_DIV)
            o_p = _dsa_prompt(q, qa, qb, kw, ke[:n_prompt], kb[:n_prompt], vb[:n_prompt],
                              n_prompt=n_prompt, topk=topk_p, n_kv=n_kv)
            seq3 = lambda z: z[n_prompt:].astype(F32).reshape(b, t_new, z.shape[1])
            keys = _dsa_sample_scores(pt_flat, seq3(qa), seq3(qb), seq3(kw), seq3(ke),
                                      cache_kit, n_pages=n_pages, page0=a * n_pool)
            topk_s = min(TOPK_MAX, (past + t_new) // TOPK_DIV)
            thr = _thresholds(keys.reshape(n_sample, -1), topk=topk_s)
            o_s = _dsa_sample_attn(pt_flat, seq3(q), keys, thr.reshape(b, t_new, LANES), seq3(k32), seq3(v32),
                                   cache_k2, cache_v2, n_pages=n_pages, n_kv=n_kv, page0=a * n_pool)
            mix_in = jnp.concatenate([o_p, o_s.reshape(n_sample, d).astype(BF16)], axis=0)
            w_out = attn_w_out[a].astype(BF16)
            outs['kp'].append(k32[:n_prompt].reshape(1, n_prompt, n_kv, HEAD_DIM))
            outs['vp'].append(v32[:n_prompt].reshape(1, n_prompt, n_kv, HEAD_DIM))
            outs['kip'].append(kw[:n_prompt, :IDX_DIM].reshape(1, n_prompt, IDX_DIM))
            outs['ks'].append(k32[n_prompt:].reshape(b, t_new, n_kv, HEAD_DIM))
            outs['vs'].append(v32[n_prompt:].reshape(b, t_new, n_kv, HEAD_DIM))
            outs['kis'].append(kw[n_prompt:, :IDX_DIM].reshape(b, t_new, IDX_DIM))
        else:
            ci = i // 2
            bgate, v = _proj_conv(x_all, mb, conv_w_in[ci].astype(BF16), npb=npb)
            zrow = jnp.zeros((1, d), F32)
            prev1 = jnp.concatenate([zrow, v[tm - 1:n_prompt - 1:tm]], axis=0)
            prev2 = jnp.concatenate([zrow, v[tm - 2:n_prompt - 2:tm]], axis=0)
            mix_p = _conv_mix(v, bgate, jnp.repeat(prev1, SUBLANES, axis=0), jnp.repeat(prev2, SUBLANES, axis=0),
                              conv_kernel[ci], period=tm, row0_blocks=0, n_rows=n_prompt, prev_per_block=True)
            st = state_conv[ci]
            mix_s = _conv_mix(v, bgate, jnp.repeat(st[:, 1], t_new, axis=0), jnp.repeat(st[:, 0], t_new, axis=0),
                              conv_kernel[ci], period=t_new, row0_blocks=npb, n_rows=n_sample, prev_per_block=False)
            mix_in = jnp.concatenate([mix_p, mix_s], axis=0)
            w_out = conv_w_out[ci].astype(BF16)
            outs['cp'].append(v[n_prompt - (CONV_W - 1):n_prompt].reshape(1, CONV_W - 1, d))
            outs['cs'].append(v[n_prompt:].reshape(b, t_new, d)[:, t_new - (CONV_W - 1):])

        rw = jnp.pad(router_w[i], ((0, 0), (0, LANES - n_exp)))
        rwh = rw.astype(BF16)
        rwl = (rw - rwh.astype(F32)).astype(BF16)
        rb = _pad_lanes(router_b[i], fill=-1e30)
        x1, u2, ti, tg = _post_mix(mix_in, x_all, mb, w_out, ln_g[i, 0:1], ln_b[i, 0:1], rwh, rwl, rb,
                                   npb=npb, alpha=alpha)
        slot_tok, out_row_of_assign, plan = _moe_plan(ti[:, :TOP_K], n_exp)
        xs = _gather_rows(u2, slot_tok)
        ys = _experts(xs, plan, expert_w_up, expert_b_up, expert_w_down, expert_b_down, layer=i)
        y4 = _gather_rows(ys, out_row_of_assign.reshape(nt, TOP_K).T.reshape(-1))
        x_all = _combine(y4, tg, x1, mb, ln_g[i, 1:2], ln_b[i, 1:2], npb=npb, alpha=alpha)

    y_prompt = x_all[:n_prompt].reshape(1, n_prompt, d)
    y_sample = x_all[n_prompt:].reshape(b, t_new, d)
    st = lambda name: jnp.stack(outs[name])
    return (y_prompt, y_sample, st('kp'), st('vp'), st('kip'), st('ks'), st('vs'), st('kis'), st('cp'), st('cs'))
```

```python
import functools

import jax
import jax.numpy as jnp
from jax import lax
from jax.experimental import pallas as pl
from jax.experimental.pallas import tpu as pltpu
from jax.experimental.pallas import tpu_sc as plsc

F32 = jnp.float32
BF16 = jnp.bfloat16
I32 = jnp.int32

HEAD_DIM = 128
IDX_HEADS = 16
IDX_DIM = 64
TOPK_MAX = 256
TOPK_DIV = 4
CONV_W = 3
TOP_K = 4
SWIGLU_LIMIT = 7.0
SWIGLU_ALPHA = 1.702
ROPE_THETA = 10000.0
LN_EPS = 1e-5

LANES = 128
SUBLANES = 8
SC_CORES = 2
SC_SUBCORES = 16
V7X_VMEM_BYTES = 64 * 1024 * 1024
VMEM_CAP_BYTES = V7X_VMEM_BYTES - 8 * 1024 * 1024

TOKEN_BLOCK = 256
Q_TILE = 128
KEY_CHUNK = 512
SEL_ROWS = 256
MOE_ROWS = 128
MOE_SUB = 10
MOE_ROW_GROUPS = 3
MOE_FF_TILE = 256
MOE_OUT_TILE = 512
GATHER_BUFS = 4
GATHER_BUF_BYTES = 64 * 1024
ADA_N_TILE = 1024

INT_MIN = -2 ** 31
HI16_MASK = -65536
LOG2_E = 1.4426950408889634
QK_SCALE_LOG2 = (HEAD_DIM ** -0.5) * LOG2_E
KEY_NEG_INF = -2139095041
NEG_BIAS = -2e30
M_FLOOR = -1e30


def _cparams(sem, est_bytes):
    limit = int(min(max(est_bytes, 16 * 1024 * 1024), VMEM_CAP_BYTES))
    return pltpu.CompilerParams(dimension_semantics=sem, vmem_limit_bytes=limit)


def _dot(a, b):
    return jnp.dot(a, b, preferred_element_type=F32)


def _dot_nt(a, b):
    return lax.dot_general(a, b, (((1,), (1,)), ((), ())), preferred_element_type=F32)


def _rep_lanes(x, n):
    return x if n == 1 else jnp.concatenate([x] * n, axis=1)


def _layer_norm_rows(y, g, b):
    mu = jnp.mean(y, axis=-1, keepdims=True)
    d = y - mu
    var = jnp.mean(d * d, axis=-1, keepdims=True)
    return d * lax.rsqrt(var + LN_EPS) * g + b


def _sort_key(s):
    bits = lax.bitcast_convert_type(s, I32)
    return bits ^ (jnp.right_shift(bits, 31) & 0x7FFFFFFF)


def _ada_kernel(c_ref, w_ref, b_ref, o_ref):
    c = c_ref[...]
    a = (c * jax.nn.sigmoid(c)).astype(BF16)
    o_ref[0] = _dot(a, w_ref[0].astype(BF16)) + b_ref[0]


def _ada_mod(c_all, ada_w, ada_b):
    depth, d, n6 = ada_w.shape
    mc = c_all.shape[0]
    tn = ADA_N_TILE
    est = 2 * (d * tn * 4 + mc * tn * 4) + mc * d * 4 * 2 + d * tn * 2
    return pl.pallas_call(
        _ada_kernel,
        grid=(depth, n6 // tn),
        in_specs=[
            pl.BlockSpec((mc, d), lambda l, j: (0, 0)),
            pl.BlockSpec((1, d, tn), lambda l, j: (l, 0, j)),
            pl.BlockSpec((1, 1, tn), lambda l, j: (l, 0, j)),
        ],
        out_specs=pl.BlockSpec((1, mc, tn), lambda l, j: (l, 0, j)),
        out_shape=jax.ShapeDtypeStruct((depth, mc, n6), F32),
        compiler_params=_cparams(("parallel", "parallel"), est),
        name="ada_mod",
    )(c_all, ada_w, ada_b.reshape(depth, 1, n6))


def _proj_attn_kernel(x_ref, sh_ref, sc_ref, w_ref, c128_ref, s128_ref, c64_ref, s64_ref, kng_ref, knb_ref,
                      q_ref, k_ref, v_ref, kb_ref, vb_ref, qa_ref, qb_ref, kw_ref, ke_ref, *, d_model, n_kv):
    xb = (x_ref[...] * (1.0 + sc_ref[...]) + sh_ref[...]).astype(BF16)
    c128, s128 = c128_ref[...], s128_ref[...]
    c64, s64 = c64_ref[...], s64_ref[...]
    tm = xb.shape[0]
    lane = lax.broadcasted_iota(I32, (tm, LANES), 1)
    low_half = (lane % IDX_DIM) < (IDX_DIM // 2)

    def rope128(y):
        return y * c128 + pltpu.roll(y, HEAD_DIM // 2, 1) * s128

    def rope64(y):
        rot = jnp.where(low_half, pltpu.roll(y, LANES - IDX_DIM // 2, 1), pltpu.roll(y, IDX_DIM // 2, 1))
        return y * c64 + rot * s64

    kvw = n_kv * HEAD_DIM
    col = 0
    for c0 in range(0, d_model, 512):
        y = _dot(xb, w_ref[:, col + c0:col + c0 + 512])
        for t in range(4):
            r = rope128(y[:, t * LANES:(t + 1) * LANES]) * QK_SCALE_LOG2
            q_ref[:, c0 + t * LANES:c0 + (t + 1) * LANES] = r.astype(BF16)
    col += d_model
    y = _dot(xb, w_ref[:, col:col + kvw])
    for t in range(n_kv):
        r = rope128(y[:, t * LANES:(t + 1) * LANES])
        k_ref[:, t * LANES:(t + 1) * LANES] = r
        kb_ref[:, t * LANES:(t + 1) * LANES] = r.astype(BF16)
    col += kvw
    y = _dot(xb, w_ref[:, col:col + kvw])
    v_ref[...] = y
    vb_ref[...] = y.astype(BF16)
    col += kvw
    iw = IDX_HEADS * IDX_DIM
    for c0 in range(0, iw, 512):
        y = _dot(xb, w_ref[:, col + c0:col + c0 + 512])
        for t in range(4):
            r = rope64(y[:, t * LANES:(t + 1) * LANES])
            qa_ref[:, c0 + t * LANES:c0 + (t + 1) * LANES] = r.astype(BF16)
            qb_ref[:, c0 + t * LANES:c0 + (t + 1) * LANES] = pltpu.roll(r, IDX_DIM, 1).astype(BF16)
    col += iw
    y = _dot(xb, w_ref[:, col:col + LANES])
    is_key = lane < IDX_DIM
    mu = jnp.sum(jnp.where(is_key, y, 0.0), axis=-1, keepdims=True) * (1.0 / IDX_DIM)
    dlt = jnp.where(is_key, y - mu, 0.0)
    var = jnp.sum(dlt * dlt, axis=-1, keepdims=True) * (1.0 / IDX_DIM)
    kn = dlt * lax.rsqrt(var + LN_EPS) * kng_ref[...] + knb_ref[...]
    ki = rope64(kn)
    wscale = (IDX_HEADS ** -0.5) * (IDX_DIM ** -0.5)
    wh = jnp.where((lane >= IDX_DIM) & (lane < IDX_DIM + IDX_HEADS), y * wscale, 0.0)
    kw_ref[...] = ki + wh
    ke_ref[...] = ki.astype(BF16)


def _mod_spec(tm, d, col, npb):
    return pl.BlockSpec((tm, d), lambda i: (jnp.where(i < npb, 0, i - npb + 1), col))


def _proj_attn(x_all, mb, w_pad, tabs, kng, knb, *, npb, n_kv):
    nt, d = x_all.shape
    tm = TOKEN_BLOCK
    kvw = n_kv * HEAD_DIM
    iw = IDX_HEADS * IDX_DIM
    row = lambda w: pl.BlockSpec((tm, w), lambda i: (i, 0))
    tab = pl.BlockSpec((tm, LANES), lambda i: (i, 0))
    vec = pl.BlockSpec((1, LANES), lambda i: (0, 0))
    out_shapes = (
        jax.ShapeDtypeStruct((nt, d), BF16),
        jax.ShapeDtypeStruct((nt, kvw), F32),
        jax.ShapeDtypeStruct((nt, kvw), F32),
        jax.ShapeDtypeStruct((nt, kvw), BF16),
        jax.ShapeDtypeStruct((nt, kvw), BF16),
        jax.ShapeDtypeStruct((nt, iw), BF16),
        jax.ShapeDtypeStruct((nt, iw), BF16),
        jax.ShapeDtypeStruct((nt, LANES), F32),
        jax.ShapeDtypeStruct((nt, LANES), BF16),
    )
    out_specs = (row(d), row(kvw), row(kvw), row(kvw), row(kvw), row(iw), row(iw), row(LANES), row(LANES))
    est = w_pad.size * 2 + 2 * tm * (3 * d * 4 + 4 * LANES * 4) + 2 * tm * (d * 2 + kvw * 12 + iw * 4 + LANES * 6) \
        + 8 * tm * 512 * 4
    return pl.pallas_call(
        functools.partial(_proj_attn_kernel, d_model=d, n_kv=n_kv),
        grid=(nt // tm,),
        in_specs=[row(d), _mod_spec(tm, d, 0, npb), _mod_spec(tm, d, 1, npb),
                  pl.BlockSpec(memory_space=pltpu.VMEM), tab, tab, tab, tab, vec, vec],
        out_specs=out_specs,
        out_shape=out_shapes,
        compiler_params=_cparams(("parallel",), est),
        name="proj_attn",
    )(x_all, mb, mb, w_pad, *tabs, kng, knb)


def _kth_largest_key(key_ref, n_chunks, chunk, kk):
    rows = key_ref.shape[0]

    def count_ge(cand):
        candb = jnp.broadcast_to(cand, (rows, LANES))

        def body(c, acc):
            base = pl.multiple_of(c * chunk, chunk)
            for j in range(chunk // LANES):
                blk = key_ref[:, pl.ds(base + j * LANES, LANES)]
                acc = acc + jnp.where(blk >= candb, 1.0, 0.0)
            return acc

        acc = lax.fori_loop(0, n_chunks, body, jnp.zeros((rows, LANES), F32))
        return jnp.sum(acc, axis=1, keepdims=True)

    kkf = float(kk)
    c0 = count_ge(jnp.zeros((rows, 1), I32))
    prefix = jnp.where(c0 >= kkf, 0, INT_MIN).astype(I32)
    n_all = float(n_chunks * chunk) if isinstance(n_chunks, int) else (n_chunks * chunk).astype(F32)
    cnt = jnp.where(c0 >= kkf, c0, n_all)

    def unsettled(cnt):
        return jnp.max(jnp.abs(cnt - kkf)).astype(I32)

    def cond(carry):
        b, _, _, open_rows = carry
        return (b < 31) & (open_rows > 0)

    def bit_body(carry):
        b, prefix, cnt, _ = carry
        cand = prefix | jnp.left_shift(jnp.int32(1), 30 - b)
        c = count_ge(cand)
        keep = c >= kkf
        cnt = jnp.where(keep, c, cnt)
        return b + 1, jnp.where(keep, cand, prefix), cnt, unsettled(cnt)

    _, prefix, _, _ = lax.while_loop(cond, bit_body, (jnp.int32(0), prefix, cnt, unsettled(cnt)))
    return prefix


def _dsa_prompt_kernel(q_ref, qa_ref, qb_ref, kw_ref, ke_ref, k_ref, v_ref, o_ref,
                       key_ref, whb_ref, qs_ref, m_ref, l_ref, acc_ref, *, tq, tc, topk, n_kv, n_rep):
    i = pl.program_id(0)
    q0 = i * tq
    n_ch = (q0 + tq + tc - 1) // tc
    kw = kw_ref[...]
    for h in range(IDX_HEADS):
        whb_ref[h] = jnp.broadcast_to(kw[:, IDX_DIM + h:IDX_DIM + h + 1], (tq, LANES))
    qpos = q0 + lax.broadcasted_iota(I32, (tq, LANES), 0)
    lane = lax.broadcasted_iota(I32, (tq, LANES), 1)
    n_sub = tc // LANES

    def score_body(c, carry):
        base = pl.multiple_of(c * tc, tc)
        ke = ke_ref[pl.ds(base, tc), :]
        acc = [jnp.zeros((tq, LANES), F32) for _ in range(n_sub)]
        for p in range(IDX_HEADS // 2):
            d_even = _dot_nt(qa_ref[:, p * LANES:(p + 1) * LANES], ke)
            d_odd = _dot_nt(qb_ref[:, p * LANES:(p + 1) * LANES], ke)
            w_even, w_odd = whb_ref[2 * p], whb_ref[2 * p + 1]
            for j in range(n_sub):
                sl = slice(j * LANES, (j + 1) * LANES)
                acc[j] = acc[j] + jnp.maximum(d_even[:, sl], 0.0) * w_even + jnp.maximum(d_odd[:, sl], 0.0) * w_odd
        for j in range(n_sub):
            kpos = base + j * LANES + lane
            key_ref[:, pl.ds(base + j * LANES, LANES)] = jnp.where(kpos <= qpos, _sort_key(acc[j]), INT_MIN)
        return carry

    lax.fori_loop(0, n_ch, score_body, 0)

    thr = jnp.maximum(_kth_largest_key(key_ref, n_ch, tc, topk), KEY_NEG_INF + 1)
    thrb = jnp.broadcast_to(thr, (tq, LANES))
    neg_bits = lax.bitcast_convert_type(jnp.float32(NEG_BIAS), I32)

    def bias_body(c, carry):
        base = pl.multiple_of(c * tc, tc)
        for j in range(n_sub):
            sl = pl.ds(base + j * LANES, LANES)
            key_ref[:, sl] = jnp.where(key_ref[:, sl] >= thrb, 0, neg_bits)
        return carry

    lax.fori_loop(0, n_ch, bias_body, 0)

    for g in range(n_kv):
        qs_ref[g] = jnp.concatenate(
            [q_ref[:, (g * n_rep + r) * LANES:(g * n_rep + r + 1) * LANES] for r in range(n_rep)], axis=0)
    m_ref[...] = jnp.full(m_ref.shape, M_FLOOR, F32)
    l_ref[...] = jnp.zeros(l_ref.shape, F32)
    acc_ref[...] = jnp.zeros(acc_ref.shape, F32)

    def att_body(c, carry):
        base = pl.multiple_of(c * tc, tc)
        bias = lax.bitcast_convert_type(key_ref[:, pl.ds(base, tc)], F32)
        bias = jnp.concatenate([bias] * n_rep, axis=0)
        for g in range(n_kv):
            kc = k_ref[pl.ds(base, tc), g * LANES:(g + 1) * LANES]
            vc = v_ref[pl.ds(base, tc), g * LANES:(g + 1) * LANES]
            s = _dot_nt(qs_ref[g], kc) + bias
            m_old = m_ref[g]
            m_new = jnp.maximum(m_old, jnp.max(s, axis=1, keepdims=True))
            alpha = jnp.exp2(m_old - m_new)
            p = jnp.exp2(s - _rep_lanes(m_new, n_sub))
            l_ref[g] = alpha * l_ref[g] + jnp.sum(p, axis=1, keepdims=True)
            acc_ref[g] = alpha * acc_ref[g] + _dot(p.astype(BF16), vc)
            m_ref[g] = m_new
        return carry

    lax.fori_loop(0, n_ch, att_body, 0)
    for g in range(n_kv):
        o = acc_ref[g] / l_ref[g]
        for r in range(n_rep):
            h = g * n_rep + r
            o_ref[:, h * LANES:(h + 1) * LANES] = o[r * tq:(r + 1) * tq].astype(BF16)


def _dsa_prompt(q, qa, qb, kw, ke, kb, vb, *, n_prompt, topk, n_kv):
    d = q.shape[1]
    n_rep = d // HEAD_DIM // n_kv
    tq, tc = Q_TILE, min(KEY_CHUNK, n_prompt)
    iw = IDX_HEADS * IDX_DIM
    kvw = n_kv * HEAD_DIM
    whole = pl.BlockSpec(memory_space=pltpu.VMEM)
    row = lambda w: pl.BlockSpec((tq, w), lambda i: (i, 0))
    est = n_prompt * (LANES * 2 + kvw * 4) + tq * n_prompt * 4 + IDX_HEADS * tq * LANES * 4 \
        + 3 * n_rep * tq * LANES * 4 + 2 * tq * (2 * d * 2 + 2 * iw * 2 + LANES * 4) + 10 * n_rep * tq * tc * 4
    return pl.pallas_call(
        functools.partial(_dsa_prompt_kernel, tq=tq, tc=tc, topk=topk, n_kv=n_kv, n_rep=n_rep),
        grid=(n_prompt // tq,),
        in_specs=[row(d), row(iw), row(iw), row(LANES), whole, whole, whole],
        out_specs=row(d),
        out_shape=jax.ShapeDtypeStruct((n_prompt, d), BF16),
        scratch_shapes=[
            pltpu.VMEM((tq, n_prompt), I32),
            pltpu.VMEM((IDX_HEADS, tq, LANES), F32),
            pltpu.VMEM((n_kv, n_rep * tq, LANES), BF16),
            pltpu.VMEM((n_kv, n_rep * tq, LANES), F32),
            pltpu.VMEM((n_kv, n_rep * tq, LANES), F32),
            pltpu.VMEM((n_kv, n_rep * tq, LANES), F32),
        ],
        compiler_params=_cparams(("parallel",), est),
        name="dsa_prompt",
    )(q, qa, qb, kw, ke, kb, vb)


def _dsa_sample_score_kernel(pt_ref, qa_ref, qb_ref, kw_ref, ken_ref, *refs, n_pages, page, t_new):
    page_refs, key_ref = refs[:n_pages], refs[n_pages]
    qa = qa_ref[0]
    qb = qb_ref[0]
    kw = kw_ref[0]
    n_pair = IDX_HEADS // 2
    q_even = jnp.concatenate([qa[:, p * LANES:(p + 1) * LANES] for p in range(n_pair)], axis=0).astype(BF16)
    q_odd = jnp.concatenate([qb[:, p * LANES:(p + 1) * LANES] for p in range(n_pair)], axis=0).astype(BF16)
    w_even = [jnp.broadcast_to(kw[:, IDX_DIM + 2 * p:IDX_DIM + 2 * p + 1], (t_new, LANES)) for p in range(n_pair)]
    w_odd = [jnp.broadcast_to(kw[:, IDX_DIM + 2 * p + 1:IDX_DIM + 2 * p + 2], (t_new, LANES)) for p in range(n_pair)]

    def scores(d_even, d_odd):
        s = jnp.zeros((t_new, LANES), F32)
        for p in range(n_pair):
            s = s + jnp.maximum(d_even[p * t_new:(p + 1) * t_new], 0.0) * w_even[p] \
                  + jnp.maximum(d_odd[p * t_new:(p + 1) * t_new], 0.0) * w_odd[p]
        return s

    zeros = jnp.zeros((LANES - IDX_DIM, page), F32)
    for pg in range(n_pages):
        ket = jnp.concatenate([page_refs[pg][0], zeros], axis=0).astype(BF16)
        key_ref[0, :, pg * page:(pg + 1) * page] = _sort_key(scores(_dot(q_even, ket), _dot(q_odd, ket)))
    ke_new = jnp.concatenate([ken_ref[0], jnp.zeros((LANES - t_new, LANES), F32)], axis=0).astype(BF16)
    s_new = scores(_dot_nt(q_even, ke_new), _dot_nt(q_odd, ke_new))
    qi = lax.broadcasted_iota(I32, (t_new, LANES), 0)
    kj = lax.broadcasted_iota(I32, (t_new, LANES), 1)
    key_ref[0, :, n_pages * page:n_pages * page + LANES] = jnp.where(kj <= qi, _sort_key(s_new), INT_MIN)


def _dsa_sample_scores(pt_flat, qa_s, qb_s, kw_s, ke_new, cache_kit, *, n_pages, page0):
    b, t_new, iw = qa_s.shape
    page = cache_kit.shape[2]
    width = n_pages * page + LANES
    seq = lambda w: pl.BlockSpec((1, t_new, w), lambda s, pt: (s, 0, 0))
    page_specs = [pl.BlockSpec((1, IDX_DIM, page), lambda s, pt, pg=pg: (page0 + pt[s * n_pages + pg], 0, 0))
                  for pg in range(n_pages)]
    est = 2 * (n_pages * page * LANES * 4 + t_new * (2 * iw + 2 * LANES + width) * 4) + 64 * page * LANES * 4
    return pl.pallas_call(
        functools.partial(_dsa_sample_score_kernel, n_pages=n_pages, page=page, t_new=t_new),
        grid_spec=pltpu.PrefetchScalarGridSpec(
            num_scalar_prefetch=1,
            grid=(b,),
            in_specs=[seq(iw), seq(iw), seq(LANES), seq(LANES)] + page_specs,
            out_specs=seq(width),
        ),
        out_shape=jax.ShapeDtypeStruct((b, t_new, width), I32),
        compiler_params=_cparams(("parallel",), est),
        name="dsa_sample_scores",
    )(pt_flat, qa_s, qb_s, kw_s, ke_new, *([cache_kit] * n_pages))


def _threshold_kernel(key_ref, thr_ref, *, topk, n_tiles):
    thr = jnp.maximum(_kth_largest_key(key_ref, n_tiles, LANES, topk), KEY_NEG_INF + 1)
    thr_ref[...] = jnp.broadcast_to(thr, thr_ref.shape)


def _thresholds(keys, *, topk):
    n, width = keys.shape
    tr = min(SEL_ROWS, n)
    est = 2 * tr * (width + LANES) * 4 + 8 * tr * LANES * 4
    return pl.pallas_call(
        functools.partial(_threshold_kernel, topk=topk, n_tiles=width // LANES),
        grid=(n // tr,),
        in_specs=[pl.BlockSpec((tr, width), lambda i: (i, 0))],
        out_specs=pl.BlockSpec((tr, LANES), lambda i: (i, 0)),
        out_shape=jax.ShapeDtypeStruct((n, LANES), I32),
        compiler_params=_cparams(("parallel",), est),
        name="topk_threshold",
    )(keys)


def _dsa_sample_attn_kernel(pt_ref, q_ref, key_ref, thr_ref, kn_ref, vn_ref, *refs,
                            n_pages, page, t_new, n_kv, n_rep):
    k_refs, v_refs = refs[:n_pages], refs[n_pages:2 * n_pages]
    o_ref, kall_ref, vall_ref = refs[2 * n_pages:]
    n_heads = n_kv * n_rep
    kvw = n_kv * HEAD_DIM
    past = n_pages * page
    for pg in range(n_pages):
        for g in range(n_kv):
            rows_g = pl.ds(g, page, stride=n_kv)
            kall_ref[pg * page:(pg + 1) * page, g * LANES:(g + 1) * LANES] = k_refs[pg][0, rows_g, :].astype(BF16)
            vall_ref[pg * page:(pg + 1) * page, g * LANES:(g + 1) * LANES] = v_refs[pg][0, rows_g, :].astype(BF16)
    pad = jnp.zeros((LANES - t_new, kvw), F32)
    kall_ref[past:past + LANES, :] = jnp.concatenate([kn_ref[0], pad], axis=0).astype(BF16)
    vall_ref[past:past + LANES, :] = jnp.concatenate([vn_ref[0], pad], axis=0).astype(BF16)
    q = q_ref[0]
    zero = jnp.zeros((t_new, LANES), F32)
    rows = []
    for h in range(n_heads):
        g = h // n_rep
        rows.append(jnp.concatenate([q[:, h * LANES:(h + 1) * LANES] if gg == g else zero for gg in range(n_kv)], axis=1))
    qbd = jnp.concatenate(rows, axis=0).astype(BF16)
    thr = thr_ref[0]
    width = past + LANES
    sel = key_ref[0] >= _rep_lanes(thr, width // LANES)
    bias = jnp.where(sel, 0.0, NEG_BIAS)
    s = _dot_nt(qbd, kall_ref[...]) + jnp.concatenate([bias] * n_heads, axis=0)
    m = jnp.maximum(jnp.max(s, axis=1, keepdims=True), M_FLOOR)
    p = jnp.exp2(s - m)
    l = jnp.sum(p, axis=1, keepdims=True)
    o = _dot(p.astype(BF16), vall_ref[...]) / l
    for h in range(n_heads):
        g = h // n_rep
        o_ref[0, :, h * LANES:(h + 1) * LANES] = o[h * t_new:(h + 1) * t_new, g * LANES:(g + 1) * LANES]


def _dsa_sample_attn(pt_flat, q_s, keys, thr, k_new, v_new, cache_k, cache_v, *, n_pages, n_kv, page0):
    b, t_new, d = q_s.shape
    page = cache_k.shape[1] // n_kv
    kvw = n_kv * HEAD_DIM
    n_rep = d // HEAD_DIM // n_kv
    width = n_pages * page + LANES
    seq = lambda w: pl.BlockSpec((1, t_new, w), lambda s, pt: (s, 0, 0))
    page_specs = [pl.BlockSpec((1, page * n_kv, HEAD_DIM), lambda s, pt, pg=pg: (page0 + pt[s * n_pages + pg], 0, 0))
                  for pg in range(n_pages)]
    est = 2 * (2 * n_pages * page * kvw * 4 + t_new * (2 * d + width + LANES + 2 * kvw) * 4) \
        + 2 * width * kvw * 2 + 6 * (d // HEAD_DIM) * t_new * width * 4
    return pl.pallas_call(
        functools.partial(_dsa_sample_attn_kernel, n_pages=n_pages, page=page, t_new=t_new, n_kv=n_kv, n_rep=n_rep),
        grid_spec=pltpu.PrefetchScalarGridSpec(
            num_scalar_prefetch=1,
            grid=(b,),
            in_specs=[seq(d), seq(width), seq(LANES), seq(kvw), seq(kvw)] + page_specs + page_specs,
            out_specs=seq(d),
            scratch_shapes=[pltpu.VMEM((width, kvw), BF16), pltpu.VMEM((width, kvw), BF16)],
        ),
        out_shape=jax.ShapeDtypeStruct((b, t_new, d), F32),
        compiler_params=_cparams(("parallel",), est),
        name="dsa_sample_attn",
    )(pt_flat, q_s, keys, thr, k_new, v_new, *([cache_k] * n_pages), *([cache_v] * n_pages))


def _proj_conv_kernel(x_ref, sh_ref, sc_ref, w_ref, b_ref, v_ref, *, d_model):
    xb = (x_ref[...] * (1.0 + sc_ref[...]) + sh_ref[...]).astype(BF16)
    for c0 in range(0, d_model, 512):
        sl = slice(c0, c0 + 512)
        b_ref[:, sl] = _dot(xb, w_ref[:, c0:c0 + 512])
        cg = _dot(xb, w_ref[:, d_model + c0:d_model + c0 + 512])
        xi = _dot(xb, w_ref[:, 2 * d_model + c0:2 * d_model + c0 + 512])
        v_ref[:, sl] = cg * xi


def _proj_conv(x_all, mb, w_bf, *, npb):
    nt, d = x_all.shape
    tm = TOKEN_BLOCK
    row = pl.BlockSpec((tm, d), lambda i: (i, 0))
    est = w_bf.size * 2 + 2 * tm * d * 4 * 5 + 8 * tm * 512 * 4
    return pl.pallas_call(
        functools.partial(_proj_conv_kernel, d_model=d),
        grid=(nt // tm,),
        in_specs=[row, _mod_spec(tm, d, 0, npb), _mod_spec(tm, d, 1, npb), pl.BlockSpec(memory_space=pltpu.VMEM)],
        out_specs=(row, row),
        out_shape=(jax.ShapeDtypeStruct((nt, d), F32), jax.ShapeDtypeStruct((nt, d), F32)),
        compiler_params=_cparams(("parallel",), est),
        name="proj_conv",
    )(x_all, mb, mb, w_bf)


def _conv_kernel(v_ref, b_ref, p1_ref, p2_ref, k_ref, o_ref, *, period):
    v = v_ref[...]
    tm = v.shape[0]
    t = lax.broadcasted_iota(I32, v.shape, 0) % period
    s1 = jnp.where(t == 0, p1_ref[...], pltpu.roll(v, 1, 0))
    s2 = jnp.where(t == 0, p2_ref[...], jnp.where(t == 1, p1_ref[...], pltpu.roll(v, 2, 0)))
    conv = k_ref[0:1, :] * s2 + k_ref[1:2, :] * s1 + k_ref[2:3, :] * v
    o_ref[...] = (b_ref[...] * conv).astype(BF16)


def _conv_mix(v, bgate, p1, p2, kern, *, period, row0_blocks, n_rows, prev_per_block):
    d = v.shape[1]
    tm = TOKEN_BLOCK
    row = pl.BlockSpec((tm, d), lambda i: (i + row0_blocks, 0))
    if prev_per_block:
        prev = pl.BlockSpec((SUBLANES, d), lambda i: (i, 0))
    else:
        prev = pl.BlockSpec((tm, d), lambda i: (i, 0))
    kpad = jnp.zeros((SUBLANES, d), F32).at[:CONV_W].set(kern)
    est = 2 * tm * d * (4 * 4 + 2) + 8 * tm * d * 4
    return pl.pallas_call(
        functools.partial(_conv_kernel if not prev_per_block else _conv_kernel_blockprev, period=period),
        grid=(n_rows // tm,),
        in_specs=[row, row, prev, prev, pl.BlockSpec((SUBLANES, d), lambda i: (0, 0))],
        out_specs=pl.BlockSpec((tm, d), lambda i: (i, 0)),
        out_shape=jax.ShapeDtypeStruct((n_rows, d), BF16),
        compiler_params=_cparams(("parallel",), est),
        name="conv_mix",
    )(v, bgate, p1, p2, kpad)


def _conv_kernel_blockprev(v_ref, b_ref, p1_ref, p2_ref, k_ref, o_ref, *, period):
    v = v_ref[...]
    t = lax.broadcasted_iota(I32, v.shape, 0) % period
    p1 = jnp.broadcast_to(p1_ref[0:1, :], v.shape)
    p2 = jnp.broadcast_to(p2_ref[0:1, :], v.shape)
    s1 = jnp.where(t == 0, p1, pltpu.roll(v, 1, 0))
    s2 = jnp.where(t == 0, p2, jnp.where(t == 1, p1, pltpu.roll(v, 2, 0)))
    conv = k_ref[0:1, :] * s2 + k_ref[1:2, :] * s1 + k_ref[2:3, :] * v
    o_ref[...] = (b_ref[...] * conv).astype(BF16)


def _post_mix_kernel(a_ref, x_ref, g_ref, sh_ref, sc_ref, w_ref, lng_ref, lnb_ref, rwh_ref, rwl_ref, rb_ref,
                     x1_ref, u_ref, ti_ref, tg_ref, *, alpha):
    mix = _dot(a_ref[...], w_ref[...])
    x1 = _layer_norm_rows(alpha * x_ref[...] + g_ref[...] * mix, lng_ref[...], lnb_ref[...])
    x1_ref[...] = x1
    u = x1 * (1.0 + sc_ref[...]) + sh_ref[...]
    u_hi = u.astype(BF16)
    bits = lax.bitcast_convert_type(u_hi.astype(F32), I32)
    half = bits.shape[1] // 2
    u_ref[...] = (jnp.right_shift(bits[:, :half], 16) & 0xFFFF) | (bits[:, half:] & HI16_MASK)
    u_lo = (u - u_hi.astype(F32)).astype(BF16)
    logits = _dot(u_hi, rwh_ref[...]) + _dot(u_lo, rwh_ref[...]) + _dot(u_hi, rwl_ref[...]) + rb_ref[...]
    tm = logits.shape[0]
    lane = lax.broadcasted_iota(I32, (tm, LANES), 1)
    lane_f = lane.astype(F32)
    ti = jnp.zeros((tm, LANES), I32)
    tv = jnp.full((tm, LANES), -jnp.inf, F32)
    for r in range(TOP_K):
        m = jnp.max(logits, axis=-1, keepdims=True)
        idx = jnp.min(jnp.where(logits == m, lane_f, float(LANES)), axis=-1, keepdims=True)
        hit = lane_f == idx
        ti = jnp.where(lane == r, idx.astype(I32), ti)
        tv = jnp.where(lane == r, m, tv)
        logits = jnp.where(hit, -jnp.inf, logits)
    e = jnp.exp(tv - jnp.max(tv, axis=-1, keepdims=True))
    tg_ref[...] = e / jnp.sum(e, axis=-1, keepdims=True)
    ti_ref[...] = ti


def _post_mix(a, x_all, mb, w_bf, lng, lnb, rwh, rwl, rb, *, npb, alpha):
    nt, d = x_all.shape
    tm = TOKEN_BLOCK
    row = lambda w: pl.BlockSpec((tm, w), lambda i: (i, 0))
    vec = lambda w: pl.BlockSpec((1, w), lambda i: (0, 0))
    whole = pl.BlockSpec(memory_space=pltpu.VMEM)
    est = w_bf.size * 2 + 2 * d * LANES * 2 + 2 * tm * d * (2 + 4 * 4 + 4 + 2) + 4 * tm * LANES * 4 + 8 * tm * d * 4
    return pl.pallas_call(
        functools.partial(_post_mix_kernel, alpha=alpha),
        grid=(nt // tm,),
        in_specs=[row(d), row(d), _mod_spec(tm, d, 2, npb), _mod_spec(tm, d, 3, npb), _mod_spec(tm, d, 4, npb),
                  whole, vec(d), vec(d), whole, whole, vec(LANES)],
        out_specs=(row(d), row(d // 2), row(LANES), row(LANES)),
        out_shape=(jax.ShapeDtypeStruct((nt, d), F32), jax.ShapeDtypeStruct((nt, d // 2), I32),
                   jax.ShapeDtypeStruct((nt, LANES), I32), jax.ShapeDtypeStruct((nt, LANES), F32)),
        compiler_params=_cparams(("parallel",), est),
        name="post_mix",
    )(a, x_all, mb, mb, mb, w_bf, lng, lnb, rwh, rwl, rb)


def _gather_rows(src, idx):
    n, d = src.shape
    m = idx.shape[0]
    n_workers = SC_CORES * SC_SUBCORES
    per_w = m // n_workers
    rows = GATHER_BUF_BYTES // (d * src.dtype.itemsize)
    nb = GATHER_BUFS
    assert m % n_workers == 0 and per_w % (nb * rows) == 0
    mesh = plsc.VectorSubcoreMesh(core_axis_name="core", subcore_axis_name="subcore")

    @functools.partial(
        pl.kernel, out_type=jax.ShapeDtypeStruct((m, d), src.dtype), mesh=mesh, name="gather_rows",
        scratch_types=[pltpu.VMEM((per_w,), I32)] + [pltpu.VMEM((rows, d), src.dtype)] * nb
                      + [pltpu.SemaphoreType.DMA] * nb)
    def gather(src_hbm, idx_hbm, dst_hbm, idx_v, *bufs_sems):
        bufs, sems = bufs_sems[:nb], bufs_sems[nb:]
        base = (lax.axis_index("subcore") * SC_CORES + lax.axis_index("core")) * per_w
        pltpu.sync_copy(idx_hbm.at[pl.ds(base, per_w)], idx_v)

        def start_gather(off, b):
            return pltpu.async_copy(src_hbm.at[idx_v.at[pl.ds(off + b * rows, rows)]], bufs[b], sems[b])

        def start_write(off, b):
            return pltpu.async_copy(bufs[b], dst_hbm.at[pl.ds(base + off + b * rows, rows)], sems[b])

        @pl.loop(0, per_w, step=nb * rows)
        def _(off):
            gathers = [start_gather(off, 0), start_gather(off, 1)]
            writes = []
            for b in range(nb):
                gathers[b].wait()
                writes.append(start_write(off, b))
                if b + 2 < nb:
                    gathers.append(start_gather(off, b + 2))
            for w in writes:
                w.wait()

    return gather(src, idx)


def _expert_kernel(ie_ref, ib_ref, nv_ref, ob_ref, *refs, n_sub, tf, tn, n_up):
    x_refs = refs[:n_sub]
    wu_ref, bu_ref, wd_ref, bd_ref, o_ref, xb_ref, h_ref, wub_ref, wdp_ref, wdb_ref = refs[n_sub:]
    i, j = pl.program_id(0), pl.program_id(1)
    nv = nv_ref[i]
    half = LANES // 2
    d_half = xb_ref.shape[1] // 2
    n_hid = h_ref.shape[1]
    n_grp = min(MOE_ROW_GROUPS, n_sub)
    sizes = [n_sub // n_grp + (1 if g < n_sub % n_grp else 0) for g in range(n_grp)]
    firsts = [sum(sizes[:g]) for g in range(n_grp)]
    row_groups = [(f * MOE_ROWS, (f + s) * MOE_ROWS, f) for f, s in zip(firsts, sizes)]
    n_active = sum(jnp.where(nv > f, 1, 0) for f in firsts)

    @pl.when((j == 0) & (nv > 0))
    def _():
        for r in range(n_sub):
            rs = slice(r * MOE_ROWS, (r + 1) * MOE_ROWS)
            w = x_refs[r][...]
            xb_ref[rs, :d_half] = lax.bitcast_convert_type(jnp.left_shift(w, 16), F32).astype(BF16)
            xb_ref[rs, d_half:] = lax.bitcast_convert_type(w & HI16_MASK, F32).astype(BF16)

    @pl.when((j < n_up) & (nv > 0))
    def _():
        wub_ref[...] = wu_ref[0, 0].astype(BF16)
        bu = bu_ref[0, 0]
        col = pl.multiple_of(j * tf, tf)

        def up_rows(r0, r1):
            h = _dot(xb_ref[r0:r1, :], wub_ref[...]) + bu
            even = (lax.broadcasted_iota(I32, (r1 - r0, LANES), 1) % 2) == 0
            prods = []
            for t in range(2 * tf // LANES):
                capped = jnp.minimum(h[:, t * LANES:(t + 1) * LANES], SWIGLU_LIMIT)
                gate_act = capped / (1.0 + jnp.exp2(capped * (-SWIGLU_ALPHA * LOG2_E)))
                act = jnp.where(even, gate_act, jnp.maximum(capped, -SWIGLU_LIMIT) + 1.0)
                prods.append(act * pltpu.roll(act, LANES - 1, 1))
            comp = [jnp.where(even, prods[2 * t], pltpu.roll(prods[2 * t + 1], 1, 1))
                    for t in range(tf // LANES)]
            h_ref[r0:r1, pl.ds(col, tf)] = jnp.concatenate(comp, axis=1).astype(BF16)

        for k in range(1, n_grp + 1):
            @pl.when(n_active == k)
            def _(k=k):
                for r0, r1, _unused in row_groups[:k]:
                    up_rows(r0, r1)

    @pl.when((j >= n_up) & (nv > 0))
    def _():
        for c in range(tn // LANES):
            cs = slice(c * LANES, (c + 1) * LANES)
            for qd in range(n_hid // LANES):
                wdp_ref[pl.ds(qd * LANES, half, stride=2), :] = wd_ref[0, 0, qd * LANES:qd * LANES + half, cs]
                wdp_ref[pl.ds(qd * LANES + 1, half, stride=2), :] = wd_ref[0, 0, qd * LANES + half:(qd + 1) * LANES, cs]
            wdb_ref[:, cs] = wdp_ref[...].astype(BF16)
        bd = bd_ref[0, 0]
        for r0, r1, first_sub in row_groups:
            @pl.when(first_sub < nv)
            def _(r0=r0, r1=r1):
                o_ref[r0:r1, :] = _dot(h_ref[r0:r1, :], wdb_ref[...]) + bd

            @pl.when(first_sub >= nv)
            def _(r0=r0, r1=r1):
                o_ref[r0:r1, :] = jnp.zeros((r1 - r0, tn), F32)


def _experts(xs, plan, w_up, b_up, w_down, b_down, *, layer):
    item_e, item_b0, item_nv, item_ob = plan
    n_items = item_e.shape[0]
    d = w_up.shape[2]
    f = w_down.shape[2]
    tf = min(MOE_FF_TILE, f)
    tn = min(MOE_OUT_TILE, d)
    n_up, n_down = f // tf, d // tn
    n_sub = MOE_SUB
    sb = n_sub * MOE_ROWS
    assert xs.shape[1] * 2 == d and n_sub % 2 == 0

    def x_map(r):
        return lambda i, j, ie, ib, nv, ob: (jnp.where(r < nv[i], ib[i] + r, ib[i]), 0)

    def up_chunk(i, j, nv):
        return jnp.where(nv[i] > 0, jnp.minimum(j, n_up - 1), n_up - 1)

    def down_chunk(i, j, nv):
        return jnp.where(nv[i] > 0, jnp.maximum(j - n_up, 0), n_down - 1)

    in_specs = [pl.BlockSpec((MOE_ROWS, d // 2), x_map(r)) for r in range(n_sub)] + [
        pl.BlockSpec((1, 1, d, 2 * tf), lambda i, j, ie, ib, nv, ob: (layer, ie[i], 0, up_chunk(i, j, nv))),
        pl.BlockSpec((1, 1, 1, 2 * tf), lambda i, j, ie, ib, nv, ob: (layer, ie[i], 0, up_chunk(i, j, nv))),
        pl.BlockSpec((1, 1, f, tn), lambda i, j, ie, ib, nv, ob: (layer, ie[i], 0, down_chunk(i, j, nv))),
        pl.BlockSpec((1, 1, 1, tn), lambda i, j, ie, ib, nv, ob: (layer, ie[i], 0, down_chunk(i, j, nv))),
    ]
    est = 2 * sb * d * 2 + sb * (d + f) * 2 + 2 * (d * 2 * tf + f * tn) * 4 + (d * 2 * tf + f * tn) * 2 \
        + f * LANES * 4 + 2 * sb * tn * 4 + 6 * (sb // 2) * 2 * tf * 4
    depth, n_exp = w_up.shape[0], w_up.shape[1]
    return pl.pallas_call(
        functools.partial(_expert_kernel, n_sub=n_sub, tf=tf, tn=tn, n_up=n_up),
        grid_spec=pltpu.PrefetchScalarGridSpec(
            num_scalar_prefetch=4,
            grid=(n_items, n_up + n_down),
            in_specs=in_specs,
            out_specs=pl.BlockSpec((sb, tn), lambda i, j, ie, ib, nv, ob: (ob[i], down_chunk(i, j, nv))),
            scratch_shapes=[pltpu.VMEM((sb, d), BF16), pltpu.VMEM((sb, f), BF16), pltpu.VMEM((d, 2 * tf), BF16),
                            pltpu.VMEM((f, LANES), F32), pltpu.VMEM((f, tn), BF16)],
        ),
        out_shape=jax.ShapeDtypeStruct((n_items * sb, d), F32),
        compiler_params=_cparams(("arbitrary", "arbitrary"), est),
        name="experts",
    )(item_e, item_b0, item_nv, item_ob, *([xs] * n_sub), w_up,
      b_up.reshape(depth, n_exp, 1, 2 * f), w_down, b_down.reshape(depth, n_exp, 1, d))


def _combine_kernel(*refs, alpha):
    y_refs = refs[:TOP_K]
    tg_ref, x_ref, g_ref, lng_ref, lnb_ref, o_ref = refs[TOP_K:]
    tg = tg_ref[...]
    moe = jnp.zeros(x_ref.shape, F32)
    for r in range(TOP_K):
        moe = moe + tg[:, r:r + 1] * y_refs[r][...]
    o_ref[...] = _layer_norm_rows(alpha * x_ref[...] + g_ref[...] * moe, lng_ref[...], lnb_ref[...])


def _combine(y4, tg, x1, mb, lng, lnb, *, npb, alpha):
    nt, d = x1.shape
    tm = TOKEN_BLOCK
    nbt = nt // tm
    row = lambda w: pl.BlockSpec((tm, w), lambda i: (i, 0))
    vec = pl.BlockSpec((1, d), lambda i: (0, 0))
    y_specs = [pl.BlockSpec((tm, d), lambda i, r=r: (r * nbt + i, 0)) for r in range(TOP_K)]
    est = 2 * tm * (TOP_K * d + LANES + 3 * d) * 4 + 6 * tm * d * 4
    return pl.pallas_call(
        functools.partial(_combine_kernel, alpha=alpha),
        grid=(nbt,),
        in_specs=y_specs + [row(LANES), row(d), _mod_spec(tm, d, 5, npb), vec, vec],
        out_specs=row(d),
        out_shape=jax.ShapeDtypeStruct((nt, d), F32),
        compiler_params=_cparams(("parallel",), est),
        name="moe_combine",
    )(*([y4] * TOP_K), tg, x1, mb, lng, lnb)


def _moe_plan(top_e, n_exp):
    nt = top_e.shape[0]
    na = nt * TOP_K
    sb = MOE_SUB * MOE_ROWS
    e_flat = top_e.reshape(-1)
    order = jnp.argsort(e_flat).astype(I32)
    e_sorted = e_flat[order]
    experts = jnp.arange(n_exp, dtype=I32)
    counts = jnp.sum((e_flat[None, :] == experts[:, None]).astype(I32), axis=1)
    starts = jnp.cumsum(counts) - counts
    padded = (counts + MOE_ROWS - 1) // MOE_ROWS * MOE_ROWS
    gend = jnp.cumsum(padded)
    gstart = gend - padded
    rank = jnp.arange(na, dtype=I32) - starts[e_sorted]
    n_slot_blocks = na // MOE_ROWS + n_exp
    slots = jnp.arange(n_slot_blocks * MOE_ROWS, dtype=I32)
    e_slot = jnp.minimum(jnp.sum((gend[None, :] <= slots[:, None]).astype(I32), axis=1), n_exp - 1)
    pos = slots - gstart[e_slot]
    src = jnp.clip(starts[e_slot] + pos, 0, na - 1)
    slot_tok = jnp.where((pos >= 0) & (pos < counts[e_slot]), order[src] // TOP_K, 0).astype(I32)
    items_e = (counts + sb - 1) // sb
    item_end = jnp.cumsum(items_e)
    item_first = item_end - items_e
    out_row_sorted = (item_first[e_sorted] + rank // sb) * sb + rank % sb
    _, out_row_of_assign = lax.sort((order, out_row_sorted), num_keys=1)
    n_items = na // sb + n_exp
    ids = jnp.arange(n_items, dtype=I32)
    n_real = item_end[-1]
    is_real = ids < n_real
    ids_c = jnp.minimum(ids, n_real - 1).astype(I32)
    e_of = jnp.minimum(jnp.sum((item_end[None, :] <= ids_c[:, None]).astype(I32), axis=1), n_exp - 1)
    s_in_e = ids_c - item_first[e_of]
    rows_left = counts[e_of] - s_in_e * sb
    nvb = jnp.where(is_real, (jnp.minimum(rows_left, sb) + MOE_ROWS - 1) // MOE_ROWS, 0).astype(I32)
    blk0 = ((gstart[e_of] + s_in_e * sb) // MOE_ROWS).astype(I32)
    return slot_tok, out_row_of_assign, (e_of, blk0, nvb, ids_c)


def _rope_tables(pos, width):
    inv = ROPE_THETA ** (-jnp.arange(0, width, 2, dtype=F32) / width)
    ang = pos.astype(F32)[:, None] * inv[None, :]
    c, s = jnp.cos(ang), jnp.sin(ang)
    reps = LANES // width
    return jnp.tile(jnp.concatenate([c, c], axis=-1), (1, reps)), jnp.tile(jnp.concatenate([-s, s], axis=-1), (1, reps))


def _pad_lanes(v, fill=0.0):
    return jnp.full((1, LANES), fill, F32).at[0, :v.shape[0]].set(v)


def kernel(x_prompt, x_sample, cache_k, cache_v, cache_kidx, state_conv, page_table, c_prompt, c_sample, ada_w, ada_b, ln_g, ln_b, attn_w_in, attn_kidx_g, attn_kidx_b, attn_w_out, conv_w_in, conv_kernel, conv_w_out, router_w, router_b, expert_w_up, expert_b_up, expert_w_down, expert_b_down):
    bp, n_prompt, d = x_prompt.shape
    b, t_new, _ = x_sample.shape
    depth = ada_w.shape[0]
    n_kv = cache_k.shape[3]
    page = cache_k.shape[2]
    n_pages = page_table.shape[1]
    past = n_pages * page
    n_exp = router_w.shape[-1]
    n_sample = b * t_new
    nt = n_prompt + n_sample
    tm = TOKEN_BLOCK
    assert bp == 1 and t_new == SUBLANES and n_prompt % tm == 0 and n_sample % tm == 0 and n_exp <= LANES
    assert d % 512 == 0 and (IDX_HEADS * IDX_DIM) % 512 == 0 and n_prompt % min(KEY_CHUNK, n_prompt) == 0
    npb = n_prompt // tm
    alpha = (2.0 * depth) ** 0.25
    kvw = n_kv * HEAD_DIM
    iw = IDX_HEADS * IDX_DIM

    x_all = jnp.concatenate([x_prompt.reshape(n_prompt, d), x_sample.reshape(n_sample, d)], axis=0)
    c_all = jnp.concatenate([c_prompt, c_sample], axis=0)
    mc = -(-c_all.shape[0] // SUBLANES) * SUBLANES
    c_all = jnp.pad(c_all, ((0, mc - c_all.shape[0]), (0, 0)))
    mod = _ada_mod(c_all, ada_w, ada_b)

    pos = jnp.concatenate([jnp.arange(n_prompt, dtype=I32), jnp.tile(past + jnp.arange(t_new, dtype=I32), b)])
    tabs = _rope_tables(pos, HEAD_DIM) + _rope_tables(pos, IDX_DIM)
    pt_flat = page_table.reshape(-1).astype(I32)
    n_pool = cache_k.shape[1]
    n_cache = cache_k.shape[0] * n_pool
    cache_k2 = cache_k.reshape(n_cache, page * n_kv, HEAD_DIM)
    cache_v2 = cache_v.reshape(n_cache, page * n_kv, HEAD_DIM)
    cache_kit = jnp.swapaxes(cache_kidx.reshape(n_cache, page, IDX_DIM), 1, 2)

    outs = dict(kp=[], vp=[], kip=[], ks=[], vs=[], kis=[], cp=[], cs=[])
    for i in range(depth):
        mb = jnp.concatenate([jnp.broadcast_to(mod[i, 0:1], (tm, 6 * d)),
                              jnp.repeat(mod[i, 1:1 + b], t_new, axis=0)], axis=0)
        if i % 2 == 0:
            a = i // 2
            w_pad = jnp.pad(attn_w_in[a], ((0, 0), (0, LANES - IDX_DIM - IDX_HEADS))).astype(BF16)
            q, k32, v32, kb, vb, qa, qb, kw, ke = _proj_attn(
                x_all, mb, w_pad, tabs, _pad_lanes(attn_kidx_g[a]), _pad_lanes(attn_kidx_b[a]), npb=npb, n_kv=n_kv)
            topk_p = min(TOPK_MAX, n_prompt // TOPK_DIV)
            o_p = _dsa_prompt(q, qa, qb, kw, ke[:n_prompt], kb[:n_prompt], vb[:n_prompt],
                              n_prompt=n_prompt, topk=topk_p, n_kv=n_kv)
            seq3 = lambda z: z[n_prompt:].astype(F32).reshape(b, t_new, z.shape[1])
            keys = _dsa_sample_scores(pt_flat, seq3(qa), seq3(qb), seq3(kw), seq3(ke),
                                      cache_kit, n_pages=n_pages, page0=a * n_pool)
            topk_s = min(TOPK_MAX, (past + t_new) // TOPK_DIV)
            thr = _thresholds(keys.reshape(n_sample, -1), topk=topk_s)
            o_s = _dsa_sample_attn(pt_flat, seq3(q), keys, thr.reshape(b, t_new, LANES), seq3(k32), seq3(v32),
                                   cache_k2, cache_v2, n_pages=n_pages, n_kv=n_kv, page0=a * n_pool)
            mix_in = jnp.concatenate([o_p, o_s.reshape(n_sample, d).astype(BF16)], axis=0)
            w_out = attn_w_out[a].astype(BF16)
            outs['kp'].append(k32[:n_prompt].reshape(1, n_prompt, n_kv, HEAD_DIM))
            outs['vp'].append(v32[:n_prompt].reshape(1, n_prompt, n_kv, HEAD_DIM))
            outs['kip'].append(kw[:n_prompt, :IDX_DIM].reshape(1, n_prompt, IDX_DIM))
            outs['ks'].append(k32[n_prompt:].reshape(b, t_new, n_kv, HEAD_DIM))
            outs['vs'].append(v32[n_prompt:].reshape(b, t_new, n_kv, HEAD_DIM))
            outs['kis'].append(kw[n_prompt:, :IDX_DIM].reshape(b, t_new, IDX_DIM))
        else:
            ci = i // 2
            bgate, v = _proj_conv(x_all, mb, conv_w_in[ci].astype(BF16), npb=npb)
            zrow = jnp.zeros((1, d), F32)
            prev1 = jnp.concatenate([zrow, v[tm - 1:n_prompt - 1:tm]], axis=0)
            prev2 = jnp.concatenate([zrow, v[tm - 2:n_prompt - 2:tm]], axis=0)
            mix_p = _conv_mix(v, bgate, jnp.repeat(prev1, SUBLANES, axis=0), jnp.repeat(prev2, SUBLANES, axis=0),
                              conv_kernel[ci], period=tm, row0_blocks=0, n_rows=n_prompt, prev_per_block=True)
            st = state_conv[ci]
            mix_s = _conv_mix(v, bgate, jnp.repeat(st[:, 1], t_new, axis=0), jnp.repeat(st[:, 0], t_new, axis=0),
                              conv_kernel[ci], period=t_new, row0_blocks=npb, n_rows=n_sample, prev_per_block=False)
            mix_in = jnp.concatenate([mix_p, mix_s], axis=0)
            w_out = conv_w_out[ci].astype(BF16)
            outs['cp'].append(v[n_prompt - (CONV_W - 1):n_prompt].reshape(1, CONV_W - 1, d))
            outs['cs'].append(v[n_prompt:].reshape(b, t_new, d)[:, t_new - (CONV_W - 1):])

        rw = jnp.pad(router_w[i], ((0, 0), (0, LANES - n_exp)))
        rwh = rw.astype(BF16)
        rwl = (rw - rwh.astype(F32)).astype(BF16)
        rb = _pad_lanes(router_b[i], fill=-1e30)
        x1, u2, ti, tg = _post_mix(mix_in, x_all, mb, w_out, ln_g[i, 0:1], ln_b[i, 0:1], rwh, rwl, rb,
                                   npb=npb, alpha=alpha)
        slot_tok, out_row_of_assign, plan = _moe_plan(ti[:, :TOP_K], n_exp)
        xs = _gather_rows(u2, slot_tok)
        ys = _experts(xs, plan, expert_w_up, expert_b_up, expert_w_down, expert_b_down, layer=i)
        y4 = _gather_rows(ys, out_row_of_assign.reshape(nt, TOP_K).T.reshape(-1))
        x_all = _combine(y4, tg, x1, mb, ln_g[i, 1:2], ln_b[i, 1:2], npb=npb, alpha=alpha)

    y_prompt = x_all[:n_prompt].reshape(1, n_prompt, d)
    y_sample = x_all[n_prompt:].reshape(b, t_new, d)
    st = lambda name: jnp.stack(outs[name])
    return (y_prompt, y_sample, st('kp'), st('vp'), st('kip'), st('ks'), st('vs'), st('kis'), st('cp'), st('cs'))
```

```python
import functools

import jax
import jax.numpy as jnp
from jax import lax
from jax.experimental import pallas as pl
from jax.experimental.pallas import tpu as pltpu
from jax.experimental.pallas import tpu_sc as plsc

F32 = jnp.float32
BF16 = jnp.bfloat16
I32 = jnp.int32

HEAD_DIM = 128
IDX_HEADS = 16
IDX_DIM = 64
TOPK_MAX = 256
TOPK_DIV = 4
CONV_W = 3
TOP_K = 4
SWIGLU_LIMIT = 7.0
SWIGLU_ALPHA = 1.702
ROPE_THETA = 10000.0
LN_EPS = 1e-5

LANES = 128
SUBLANES = 8
SC_CORES = 2
SC_SUBCORES = 16
V7X_VMEM_BYTES = 64 * 1024 * 1024
VMEM_CAP_BYTES = V7X_VMEM_BYTES - 8 * 1024 * 1024

TOKEN_BLOCK = 256
Q_TILE = 128
KEY_CHUNK = 512
SEL_ROWS = 256
MOE_ROWS = 128
MOE_SUB = 10
MOE_ROW_GROUPS = 3
MOE_FF_TILE = 256
MOE_OUT_TILE = 512
GATHER_BUFS = 4
GATHER_BUF_BYTES = 64 * 1024
ADA_N_TILE = 1024

INT_MIN = -2 ** 31
HI16_MASK = -65536
LOG2_E = 1.4426950408889634
QK_SCALE_LOG2 = (HEAD_DIM ** -0.5) * LOG2_E
KEY_NEG_INF = -2139095041
NEG_BIAS = -2e30
M_FLOOR = -1e30


def _cparams(sem, est_bytes):
    limit = int(min(max(est_bytes, 16 * 1024 * 1024), VMEM_CAP_BYTES))
    return pltpu.CompilerParams(dimension_semantics=sem, vmem_limit_bytes=limit)


def _dot(a, b):
    return jnp.dot(a, b, preferred_element_type=F32)


def _dot_nt(a, b):
    return lax.dot_general(a, b, (((1,), (1,)), ((), ())), preferred_element_type=F32)


def _rep_lanes(x, n):
    return x if n == 1 else jnp.concatenate([x] * n, axis=1)


def _layer_norm_rows(y, g, b):
    mu = jnp.mean(y, axis=-1, keepdims=True)
    d = y - mu
    var = jnp.mean(d * d, axis=-1, keepdims=True)
    return d * lax.rsqrt(var + LN_EPS) * g + b


def _sort_key(s):
    bits = lax.bitcast_convert_type(s, I32)
    return bits ^ (jnp.right_shift(bits, 31) & 0x7FFFFFFF)


def _ada_kernel(c_ref, w_ref, b_ref, o_ref):
    c = c_ref[...]
    a = (c * jax.nn.sigmoid(c)).astype(BF16)
    o_ref[0] = _dot(a, w_ref[0].astype(BF16)) + b_ref[0]


def _ada_mod(c_all, ada_w, ada_b):
    depth, d, n6 = ada_w.shape
    mc = c_all.shape[0]
    tn = ADA_N_TILE
    est = 2 * (d * tn * 4 + mc * tn * 4) + mc * d * 4 * 2 + d * tn * 2
    return pl.pallas_call(
        _ada_kernel,
        grid=(depth, n6 // tn),
        in_specs=[
            pl.BlockSpec((mc, d), lambda l, j: (0, 0)),
            pl.BlockSpec((1, d, tn), lambda l, j: (l, 0, j)),
            pl.BlockSpec((1, 1, tn), lambda l, j: (l, 0, j)),
        ],
        out_specs=pl.BlockSpec((1, mc, tn), lambda l, j: (l, 0, j)),
        out_shape=jax.ShapeDtypeStruct((depth, mc, n6), F32),
        compiler_params=_cparams(("parallel", "parallel"), est),
        name="ada_mod",
    )(c_all, ada_w, ada_b.reshape(depth, 1, n6))


def _proj_attn_kernel(x_ref, sh_ref, sc_ref, w_ref, c128_ref, s128_ref, c64_ref, s64_ref, kng_ref, knb_ref,
                      q_ref, k_ref, v_ref, kb_ref, vb_ref, qa_ref, qb_ref, kw_ref, ke_ref, *, d_model, n_kv):
    xb = (x_ref[...] * (1.0 + sc_ref[...]) + sh_ref[...]).astype(BF16)
    c128, s128 = c128_ref[...], s128_ref[...]
    c64, s64 = c64_ref[...], s64_ref[...]
    tm = xb.shape[0]
    lane = lax.broadcasted_iota(I32, (tm, LANES), 1)
    low_half = (lane % IDX_DIM) < (IDX_DIM // 2)

    def rope128(y):
        return y * c128 + pltpu.roll(y, HEAD_DIM // 2, 1) * s128

    def rope64(y):
        rot = jnp.where(low_half, pltpu.roll(y, LANES - IDX_DIM // 2, 1), pltpu.roll(y, IDX_DIM // 2, 1))
        return y * c64 + rot * s64

    kvw = n_kv * HEAD_DIM
    col = 0
    for c0 in range(0, d_model, 512):
        y = _dot(xb, w_ref[:, col + c0:col + c0 + 512])
        for t in range(4):
            r = rope128(y[:, t * LANES:(t + 1) * LANES]) * QK_SCALE_LOG2
            q_ref[:, c0 + t * LANES:c0 + (t + 1) * LANES] = r.astype(BF16)
    col += d_model
    y = _dot(xb, w_ref[:, col:col + kvw])
    for t in range(n_kv):
        r = rope128(y[:, t * LANES:(t + 1) * LANES])
        k_ref[:, t * LANES:(t + 1) * LANES] = r
        kb_ref[:, t * LANES:(t + 1) * LANES] = r.astype(BF16)
    col += kvw
    y = _dot(xb, w_ref[:, col:col + kvw])
    v_ref[...] = y
    vb_ref[...] = y.astype(BF16)
    col += kvw
    iw = IDX_HEADS * IDX_DIM
    for c0 in range(0, iw, 512):
        y = _dot(xb, w_ref[:, col + c0:col + c0 + 512])
        for t in range(4):
            r = rope64(y[:, t * LANES:(t + 1) * LANES])
            qa_ref[:, c0 + t * LANES:c0 + (t + 1) * LANES] = r.astype(BF16)
            qb_ref[:, c0 + t * LANES:c0 + (t + 1) * LANES] = pltpu.roll(r, IDX_DIM, 1).astype(BF16)
    col += iw
    y = _dot(xb, w_ref[:, col:col + LANES])
    is_key = lane < IDX_DIM
    mu = jnp.sum(jnp.where(is_key, y, 0.0), axis=-1, keepdims=True) * (1.0 / IDX_DIM)
    dlt = jnp.where(is_key, y - mu, 0.0)
    var = jnp.sum(dlt * dlt, axis=-1, keepdims=True) * (1.0 / IDX_DIM)
    kn = dlt * lax.rsqrt(var + LN_EPS) * kng_ref[...] + knb_ref[...]
    ki = rope64(kn)
    wscale = (IDX_HEADS ** -0.5) * (IDX_DIM ** -0.5)
    wh = jnp.where((lane >= IDX_DIM) & (lane < IDX_DIM + IDX_HEADS), y * wscale, 0.0)
    kw_ref[...] = ki + wh
    ke_ref[...] = ki.astype(BF16)


def _mod_spec(tm, d, col, npb):
    return pl.BlockSpec((tm, d), lambda i: (jnp.where(i < npb, 0, i - npb + 1), col))


def _proj_attn(x_all, mb, w_pad, tabs, kng, knb, *, npb, n_kv):
    nt, d = x_all.shape
    tm = TOKEN_BLOCK
    kvw = n_kv * HEAD_DIM
    iw = IDX_HEADS * IDX_DIM
    row = lambda w: pl.BlockSpec((tm, w), lambda i: (i, 0))
    tab = pl.BlockSpec((tm, LANES), lambda i: (i, 0))
    vec = pl.BlockSpec((1, LANES), lambda i: (0, 0))
    out_shapes = (
        jax.ShapeDtypeStruct((nt, d), BF16),
        jax.ShapeDtypeStruct((nt, kvw), F32),
        jax.ShapeDtypeStruct((nt, kvw), F32),
        jax.ShapeDtypeStruct((nt, kvw), BF16),
        jax.ShapeDtypeStruct((nt, kvw), BF16),
        jax.ShapeDtypeStruct((nt, iw), BF16),
        jax.ShapeDtypeStruct((nt, iw), BF16),
        jax.ShapeDtypeStruct((nt, LANES), F32),
        jax.ShapeDtypeStruct((nt, LANES), BF16),
    )
    out_specs = (row(d), row(kvw), row(kvw), row(kvw), row(kvw), row(iw), row(iw), row(LANES), row(LANES))
    est = w_pad.size * 2 + 2 * tm * (3 * d * 4 + 4 * LANES * 4) + 2 * tm * (d * 2 + kvw * 12 + iw * 4 + LANES * 6) \
        + 8 * tm * 512 * 4
    return pl.pallas_call(
        functools.partial(_proj_attn_kernel, d_model=d, n_kv=n_kv),
        grid=(nt // tm,),
        in_specs=[row(d), _mod_spec(tm, d, 0, npb), _mod_spec(tm, d, 1, npb),
                  pl.BlockSpec(memory_space=pltpu.VMEM), tab, tab, tab, tab, vec, vec],
        out_specs=out_specs,
        out_shape=out_shapes,
        compiler_params=_cparams(("parallel",), est),
        name="proj_attn",
    )(x_all, mb, mb, w_pad, *tabs, kng, knb)


def _kth_largest_key(key_ref, n_chunks, chunk, kk):
    rows = key_ref.shape[0]

    def count_ge(cand):
        candb = jnp.broadcast_to(cand, (rows, LANES))

        def body(c, acc):
            base = pl.multiple_of(c * chunk, chunk)
            for j in range(chunk // LANES):
                blk = key_ref[:, pl.ds(base + j * LANES, LANES)]
                acc = acc + jnp.where(blk >= candb, 1.0, 0.0)
            return acc

        acc = lax.fori_loop(0, n_chunks, body, jnp.zeros((rows, LANES), F32))
        return jnp.sum(acc, axis=1, keepdims=True)

    kkf = float(kk)
    c0 = count_ge(jnp.zeros((rows, 1), I32))
    prefix = jnp.where(c0 >= kkf, 0, INT_MIN).astype(I32)
    n_all = float(n_chunks * chunk) if isinstance(n_chunks, int) else (n_chunks * chunk).astype(F32)
    cnt = jnp.where(c0 >= kkf, c0, n_all)

    def unsettled(cnt):
        return jnp.max(jnp.abs(cnt - kkf)).astype(I32)

    def cond(carry):
        b, _, _, open_rows = carry
        return (b < 31) & (open_rows > 0)

    def bit_body(carry):
        b, prefix, cnt, _ = carry
        cand = prefix | jnp.left_shift(jnp.int32(1), 30 - b)
        c = count_ge(cand)
        keep = c >= kkf
        cnt = jnp.where(keep, c, cnt)
        return b + 1, jnp.where(keep, cand, prefix), cnt, unsettled(cnt)

    _, prefix, _, _ = lax.while_loop(cond, bit_body, (jnp.int32(0), prefix, cnt, unsettled(cnt)))
    return prefix


def _dsa_prompt_kernel(q_ref, qa_ref, qb_ref, kw_ref, ke_ref, k_ref, v_ref, o_ref,
                       key_ref, whb_ref, qs_ref, m_ref, l_ref, acc_ref, *, tq, tc, topk, n_kv, n_rep):
    i = pl.program_id(0)
    q0 = i * tq
    n_ch = (q0 + tq + tc - 1) // tc
    kw = kw_ref[...]
    for h in range(IDX_HEADS):
        whb_ref[h] = jnp.broadcast_to(kw[:, IDX_DIM + h:IDX_DIM + h + 1], (tq, LANES))
    qpos = q0 + lax.broadcasted_iota(I32, (tq, LANES), 0)
    lane = lax.broadcasted_iota(I32, (tq, LANES), 1)
    n_sub = tc // LANES

    def score_body(c, carry):
        base = pl.multiple_of(c * tc, tc)
        ke = ke_ref[pl.ds(base, tc), :]
        acc = [jnp.zeros((tq, LANES), F32) for _ in range(n_sub)]
        for p in range(IDX_HEADS // 2):
            d_even = _dot_nt(qa_ref[:, p * LANES:(p + 1) * LANES], ke)
            d_odd = _dot_nt(qb_ref[:, p * LANES:(p + 1) * LANES], ke)
            w_even, w_odd = whb_ref[2 * p], whb_ref[2 * p + 1]
            for j in range(n_sub):
                sl = slice(j * LANES, (j + 1) * LANES)
                acc[j] = acc[j] + jnp.maximum(d_even[:, sl], 0.0) * w_even + jnp.maximum(d_odd[:, sl], 0.0) * w_odd
        for j in range(n_sub):
            kpos = base + j * LANES + lane
            key_ref[:, pl.ds(base + j * LANES, LANES)] = jnp.where(kpos <= qpos, _sort_key(acc[j]), INT_MIN)
        return carry

    lax.fori_loop(0, n_ch, score_body, 0)

    thr = jnp.maximum(_kth_largest_key(key_ref, n_ch, tc, topk), KEY_NEG_INF + 1)
    thrb = jnp.broadcast_to(thr, (tq, LANES))
    neg_bits = lax.bitcast_convert_type(jnp.float32(NEG_BIAS), I32)

    def bias_body(c, carry):
        base = pl.multiple_of(c * tc, tc)
        for j in range(n_sub):
            sl = pl.ds(base + j * LANES, LANES)
            key_ref[:, sl] = jnp.where(key_ref[:, sl] >= thrb, 0, neg_bits)
        return carry

    lax.fori_loop(0, n_ch, bias_body, 0)

    for g in range(n_kv):
        qs_ref[g] = jnp.concatenate(
            [q_ref[:, (g * n_rep + r) * LANES:(g * n_rep + r + 1) * LANES] for r in range(n_rep)], axis=0)
    m_ref[...] = jnp.full(m_ref.shape, M_FLOOR, F32)
    l_ref[...] = jnp.zeros(l_ref.shape, F32)
    acc_ref[...] = jnp.zeros(acc_ref.shape, F32)

    def att_body(c, carry):
        base = pl.multiple_of(c * tc, tc)
        bias = lax.bitcast_convert_type(key_ref[:, pl.ds(base, tc)], F32)
        bias = jnp.concatenate([bias] * n_rep, axis=0)
        for g in range(n_kv):
            kc = k_ref[pl.ds(base, tc), g * LANES:(g + 1) * LANES]
            vc = v_ref[pl.ds(base, tc), g * LANES:(g + 1) * LANES]
            s = _dot_nt(qs_ref[g], kc) + bias
            m_old = m_ref[g]
            m_new = jnp.maximum(m_old, jnp.max(s, axis=1, keepdims=True))
            alpha = jnp.exp2(m_old - m_new)
            p = jnp.exp2(s - _rep_lanes(m_new, n_sub))
            l_ref[g] = alpha * l_ref[g] + jnp.sum(p, axis=1, keepdims=True)
            acc_ref[g] = alpha * acc_ref[g] + _dot(p.astype(BF16), vc)
            m_ref[g] = m_new
        return carry

    lax.fori_loop(0, n_ch, att_body, 0)
    for g in range(n_kv):
        o = acc_ref[g] / l_ref[g]
        for r in range(n_rep):
            h = g * n_rep + r
            o_ref[:, h * LANES:(h + 1) * LANES] = o[r * tq:(r + 1) * tq].astype(BF16)


def _dsa_prompt(q, qa, qb, kw, ke, kb, vb, *, n_prompt, topk, n_kv):
    d = q.shape[1]
    n_rep = d // HEAD_DIM // n_kv
    tq, tc = Q_TILE, min(KEY_CHUNK, n_prompt)
    iw = IDX_HEADS * IDX_DIM
    kvw = n_kv * HEAD_DIM
    whole = pl.BlockSpec(memory_space=pltpu.VMEM)
    row = lambda w: pl.BlockSpec((tq, w), lambda i: (i, 0))
    est = n_prompt * (LANES * 2 + kvw * 4) + tq * n_prompt * 4 + IDX_HEADS * tq * LANES * 4 \
        + 3 * n_rep * tq * LANES * 4 + 2 * tq * (2 * d * 2 + 2 * iw * 2 + LANES * 4) + 10 * n_rep * tq * tc * 4
    return pl.pallas_call(
        functools.partial(_dsa_prompt_kernel, tq=tq, tc=tc, topk=topk, n_kv=n_kv, n_rep=n_rep),
        grid=(n_prompt // tq,),
        in_specs=[row(d), row(iw), row(iw), row(LANES), whole, whole, whole],
        out_specs=row(d),
        out_shape=jax.ShapeDtypeStruct((n_prompt, d), BF16),
        scratch_shapes=[
            pltpu.VMEM((tq, n_prompt), I32),
            pltpu.VMEM((IDX_HEADS, tq, LANES), F32),
            pltpu.VMEM((n_kv, n_rep * tq, LANES), BF16),
            pltpu.VMEM((n_kv, n_rep * tq, LANES), F32),
            pltpu.VMEM((n_kv, n_rep * tq, LANES), F32),
            pltpu.VMEM((n_kv, n_rep * tq, LANES), F32),
        ],
        compiler_params=_cparams(("parallel",), est),
        name="dsa_prompt",
    )(q, qa, qb, kw, ke, kb, vb)


def _dsa_sample_score_kernel(pt_ref, qa_ref, qb_ref, kw_ref, ken_ref, *refs, n_pages, page, t_new):
    page_refs, key_ref = refs[:n_pages], refs[n_pages]
    qa = qa_ref[0]
    qb = qb_ref[0]
    kw = kw_ref[0]
    n_pair = IDX_HEADS // 2
    q_even = jnp.concatenate([qa[:, p * LANES:(p + 1) * LANES] for p in range(n_pair)], axis=0).astype(BF16)
    q_odd = jnp.concatenate([qb[:, p * LANES:(p + 1) * LANES] for p in range(n_pair)], axis=0).astype(BF16)
    w_even = [jnp.broadcast_to(kw[:, IDX_DIM + 2 * p:IDX_DIM + 2 * p + 1], (t_new, LANES)) for p in range(n_pair)]
    w_odd = [jnp.broadcast_to(kw[:, IDX_DIM + 2 * p + 1:IDX_DIM + 2 * p + 2], (t_new, LANES)) for p in range(n_pair)]

    def scores(d_even, d_odd):
        s = jnp.zeros((t_new, LANES), F32)
        for p in range(n_pair):
            s = s + jnp.maximum(d_even[p * t_new:(p + 1) * t_new], 0.0) * w_even[p] \
                  + jnp.maximum(d_odd[p * t_new:(p + 1) * t_new], 0.0) * w_odd[p]
        return s

    zeros = jnp.zeros((LANES - IDX_DIM, page), F32)
    for pg in range(n_pages):
        ket = jnp.concatenate([page_refs[pg][0], zeros], axis=0).astype(BF16)
        key_ref[0, :, pg * page:(pg + 1) * page] = _sort_key(scores(_dot(q_even, ket), _dot(q_odd, ket)))
    ke_new = jnp.concatenate([ken_ref[0], jnp.zeros((LANES - t_new, LANES), F32)], axis=0).astype(BF16)
    s_new = scores(_dot_nt(q_even, ke_new), _dot_nt(q_odd, ke_new))
    qi = lax.broadcasted_iota(I32, (t_new, LANES), 0)
    kj = lax.broadcasted_iota(I32, (t_new, LANES), 1)
    key_ref[0, :, n_pages * page:n_pages * page + LANES] = jnp.where(kj <= qi, _sort_key(s_new), INT_MIN)


def _dsa_sample_scores(pt_flat, qa_s, qb_s, kw_s, ke_new, cache_kit, *, n_pages, page0):
    b, t_new, iw = qa_s.shape
    page = cache_kit.shape[2]
    width = n_pages * page + LANES
    seq = lambda w: pl.BlockSpec((1, t_new, w), lambda s, pt: (s, 0, 0))
    page_specs = [pl.BlockSpec((1, IDX_DIM, page), lambda s, pt, pg=pg: (page0 + pt[s * n_pages + pg], 0, 0))
                  for pg in range(n_pages)]
    est = 2 * (n_pages * page * LANES * 4 + t_new * (2 * iw + 2 * LANES + width) * 4) + 64 * page * LANES * 4
    return pl.pallas_call(
        functools.partial(_dsa_sample_score_kernel, n_pages=n_pages, page=page, t_new=t_new),
        grid_spec=pltpu.PrefetchScalarGridSpec(
            num_scalar_prefetch=1,
            grid=(b,),
            in_specs=[seq(iw), seq(iw), seq(LANES), seq(LANES)] + page_specs,
            out_specs=seq(width),
        ),
        out_shape=jax.ShapeDtypeStruct((b, t_new, width), I32),
        compiler_params=_cparams(("parallel",), est),
        name="dsa_sample_scores",
    )(pt_flat, qa_s, qb_s, kw_s, ke_new, *([cache_kit] * n_pages))


def _threshold_kernel(key_ref, thr_ref, *, topk, n_tiles):
    thr = jnp.maximum(_kth_largest_key(key_ref, n_tiles, LANES, topk), KEY_NEG_INF + 1)
    thr_ref[...] = jnp.broadcast_to(thr, thr_ref.shape)


def _thresholds(keys, *, topk):
    n, width = keys.shape
    tr = min(SEL_ROWS, n)
    est = 2 * tr * (width + LANES) * 4 + 8 * tr * LANES * 4
    return pl.pallas_call(
        functools.partial(_threshold_kernel, topk=topk, n_tiles=width // LANES),
        grid=(n // tr,),
        in_specs=[pl.BlockSpec((tr, width), lambda i: (i, 0))],
        out_specs=pl.BlockSpec((tr, LANES), lambda i: (i, 0)),
        out_shape=jax.ShapeDtypeStruct((n, LANES), I32),
        compiler_params=_cparams(("parallel",), est),
        name="topk_threshold",
    )(keys)


def _dsa_sample_attn_kernel(pt_ref, q_ref, key_ref, thr_ref, kn_ref, vn_ref, *refs,
                            n_pages, page, t_new, n_kv, n_rep):
    k_refs, v_refs = refs[:n_pages], refs[n_pages:2 * n_pages]
    o_ref, kall_ref, vall_ref = refs[2 * n_pages:]
    n_heads = n_kv * n_rep
    kvw = n_kv * HEAD_DIM
    past = n_pages * page
    for pg in range(n_pages):
        for g in range(n_kv):
            rows_g = pl.ds(g, page, stride=n_kv)
            kall_ref[pg * page:(pg + 1) * page, g * LANES:(g + 1) * LANES] = k_refs[pg][0, rows_g, :].astype(BF16)
            vall_ref[pg * page:(pg + 1) * page, g * LANES:(g + 1) * LANES] = v_refs[pg][0, rows_g, :].astype(BF16)
    pad = jnp.zeros((LANES - t_new, kvw), F32)
    kall_ref[past:past + LANES, :] = jnp.concatenate([kn_ref[0], pad], axis=0).astype(BF16)
    vall_ref[past:past + LANES, :] = jnp.concatenate([vn_ref[0], pad], axis=0).astype(BF16)
    q = q_ref[0]
    zero = jnp.zeros((t_new, LANES), F32)
    rows = []
    for h in range(n_heads):
        g = h // n_rep
        rows.append(jnp.concatenate([q[:, h * LANES:(h + 1) * LANES] if gg == g else zero for gg in range(n_kv)], axis=1))
    qbd = jnp.concatenate(rows, axis=0).astype(BF16)
    thr = thr_ref[0]
    width = past + LANES
    sel = key_ref[0] >= _rep_lanes(thr, width // LANES)
    bias = jnp.where(sel, 0.0, NEG_BIAS)
    s = _dot_nt(qbd, kall_ref[...]) + jnp.concatenate([bias] * n_heads, axis=0)
    m = jnp.maximum(jnp.max(s, axis=1, keepdims=True), M_FLOOR)
    p = jnp.exp2(s - m)
    l = jnp.sum(p, axis=1, keepdims=True)
    o = _dot(p.astype(BF16), vall_ref[...]) / l
    for h in range(n_heads):
        g = h // n_rep
        o_ref[0, :, h * LANES:(h + 1) * LANES] = o[h * t_new:(h + 1) * t_new, g * LANES:(g + 1) * LANES]


def _dsa_sample_attn(pt_flat, q_s, keys, thr, k_new, v_new, cache_k, cache_v, *, n_pages, n_kv, page0):
    b, t_new, d = q_s.shape
    page = cache_k.shape[1] // n_kv
    kvw = n_kv * HEAD_DIM
    n_rep = d // HEAD_DIM // n_kv
    width = n_pages * page + LANES
    seq = lambda w: pl.BlockSpec((1, t_new, w), lambda s, pt: (s, 0, 0))
    page_specs = [pl.BlockSpec((1, page * n_kv, HEAD_DIM), lambda s, pt, pg=pg: (page0 + pt[s * n_pages + pg], 0, 0))
                  for pg in range(n_pages)]
    est = 2 * (2 * n_pages * page * kvw * 4 + t_new * (2 * d + width + LANES + 2 * kvw) * 4) \
        + 2 * width * kvw * 2 + 6 * (d // HEAD_DIM) * t_new * width * 4
    return pl.pallas_call(
        functools.partial(_dsa_sample_attn_kernel, n_pages=n_pages, page=page, t_new=t_new, n_kv=n_kv, n_rep=n_rep),
        grid_spec=pltpu.PrefetchScalarGridSpec(
            num_scalar_prefetch=1,
            grid=(b,),
            in_specs=[seq(d), seq(width), seq(LANES), seq(kvw), seq(kvw)] + page_specs + page_specs,
            out_specs=seq(d),
            scratch_shapes=[pltpu.VMEM((width, kvw), BF16), pltpu.VMEM((width, kvw), BF16)],
        ),
        out_shape=jax.ShapeDtypeStruct((b, t_new, d), F32),
        compiler_params=_cparams(("parallel",), est),
        name="dsa_sample_attn",
    )(pt_flat, q_s, keys, thr, k_new, v_new, *([cache_k] * n_pages), *([cache_v] * n_pages))


def _proj_conv_kernel(x_ref, sh_ref, sc_ref, w_ref, b_ref, v_ref, *, d_model):
    xb = (x_ref[...] * (1.0 + sc_ref[...]) + sh_ref[...]).astype(BF16)
    for c0 in range(0, d_model, 512):
        sl = slice(c0, c0 + 512)
        b_ref[:, sl] = _dot(xb, w_ref[:, c0:c0 + 512])
        cg = _dot(xb, w_ref[:, d_model + c0:d_model + c0 + 512])
        xi = _dot(xb, w_ref[:, 2 * d_model + c0:2 * d_model + c0 + 512])
        v_ref[:, sl] = cg * xi


def _proj_conv(x_all, mb, w_bf, *, npb):
    nt, d = x_all.shape
    tm = TOKEN_BLOCK
    row = pl.BlockSpec((tm, d), lambda i: (i, 0))
    est = w_bf.size * 2 + 2 * tm * d * 4 * 5 + 8 * tm * 512 * 4
    return pl.pallas_call(
        functools.partial(_proj_conv_kernel, d_model=d),
        grid=(nt // tm,),
        in_specs=[row, _mod_spec(tm, d, 0, npb), _mod_spec(tm, d, 1, npb), pl.BlockSpec(memory_space=pltpu.VMEM)],
        out_specs=(row, row),
        out_shape=(jax.ShapeDtypeStruct((nt, d), F32), jax.ShapeDtypeStruct((nt, d), F32)),
        compiler_params=_cparams(("parallel",), est),
        name="proj_conv",
    )(x_all, mb, mb, w_bf)


def _conv_kernel(v_ref, b_ref, p1_ref, p2_ref, k_ref, o_ref, *, period):
    v = v_ref[...]
    tm = v.shape[0]
    t = lax.broadcasted_iota(I32, v.shape, 0) % period
    s1 = jnp.where(t == 0, p1_ref[...], pltpu.roll(v, 1, 0))
    s2 = jnp.where(t == 0, p2_ref[...], jnp.where(t == 1, p1_ref[...], pltpu.roll(v, 2, 0)))
    conv = k_ref[0:1, :] * s2 + k_ref[1:2, :] * s1 + k_ref[2:3, :] * v
    o_ref[...] = (b_ref[...] * conv).astype(BF16)


def _conv_mix(v, bgate, p1, p2, kern, *, period, row0_blocks, n_rows, prev_per_block):
    d = v.shape[1]
    tm = TOKEN_BLOCK
    row = pl.BlockSpec((tm, d), lambda i: (i + row0_blocks, 0))
    if prev_per_block:
        prev = pl.BlockSpec((SUBLANES, d), lambda i: (i, 0))
    else:
        prev = pl.BlockSpec((tm, d), lambda i: (i, 0))
    kpad = jnp.zeros((SUBLANES, d), F32).at[:CONV_W].set(kern)
    est = 2 * tm * d * (4 * 4 + 2) + 8 * tm * d * 4
    return pl.pallas_call(
        functools.partial(_conv_kernel if not prev_per_block else _conv_kernel_blockprev, period=period),
        grid=(n_rows // tm,),
        in_specs=[row, row, prev, prev, pl.BlockSpec((SUBLANES, d), lambda i: (0, 0))],
        out_specs=pl.BlockSpec((tm, d), lambda i: (i, 0)),
        out_shape=jax.ShapeDtypeStruct((n_rows, d), BF16),
        compiler_params=_cparams(("parallel",), est),
        name="conv_mix",
    )(v, bgate, p1, p2, kpad)


def _conv_kernel_blockprev(v_ref, b_ref, p1_ref, p2_ref, k_ref, o_ref, *, period):
    v = v_ref[...]
    t = lax.broadcasted_iota(I32, v.shape, 0) % period
    p1 = jnp.broadcast_to(p1_ref[0:1, :], v.shape)
    p2 = jnp.broadcast_to(p2_ref[0:1, :], v.shape)
    s1 = jnp.where(t == 0, p1, pltpu.roll(v, 1, 0))
    s2 = jnp.where(t == 0, p2, jnp.where(t == 1, p1, pltpu.roll(v, 2, 0)))
    conv = k_ref[0:1, :] * s2 + k_ref[1:2, :] * s1 + k_ref[2:3, :] * v
    o_ref[...] = (b_ref[...] * conv).astype(BF16)


def _post_mix_kernel(a_ref, x_ref, g_ref, sh_ref, sc_ref, w_ref, lng_ref, lnb_ref, rwh_ref, rwl_ref, rb_ref,
                     x1_ref, u_ref, ti_ref, tg_ref, *, alpha):
    mix = _dot(a_ref[...], w_ref[...])
    x1 = _layer_norm_rows(alpha * x_ref[...] + g_ref[...] * mix, lng_ref[...], lnb_ref[...])
    x1_ref[...] = x1
    u = x1 * (1.0 + sc_ref[...]) + sh_ref[...]
    u_hi = u.astype(BF16)
    bits = lax.bitcast_convert_type(u_hi.astype(F32), I32)
    half = bits.shape[1] // 2
    u_ref[...] = (jnp.right_shift(bits[:, :half], 16) & 0xFFFF) | (bits[:, half:] & HI16_MASK)
    u_lo = (u - u_hi.astype(F32)).astype(BF16)
    logits = _dot(u_hi, rwh_ref[...]) + _dot(u_lo, rwh_ref[...]) + _dot(u_hi, rwl_ref[...]) + rb_ref[...]
    tm = logits.shape[0]
    lane = lax.broadcasted_iota(I32, (tm, LANES), 1)
    lane_f = lane.astype(F32)
    ti = jnp.zeros((tm, LANES), I32)
    tv = jnp.full((tm, LANES), -jnp.inf, F32)
    for r in range(TOP_K):
        m = jnp.max(logits, axis=-1, keepdims=True)
        idx = jnp.min(jnp.where(logits == m, lane_f, float(LANES)), axis=-1, keepdims=True)
        hit = lane_f == idx
        ti = jnp.where(lane == r, idx.astype(I32), ti)
        tv = jnp.where(lane == r, m, tv)
        logits = jnp.where(hit, -jnp.inf, logits)
    e = jnp.exp(tv - jnp.max(tv, axis=-1, keepdims=True))
    tg_ref[...] = e / jnp.sum(e, axis=-1, keepdims=True)
    ti_ref[...] = ti


def _post_mix(a, x_all, mb, w_bf, lng, lnb, rwh, rwl, rb, *, npb, alpha):
    nt, d = x_all.shape
    tm = TOKEN_BLOCK
    row = lambda w: pl.BlockSpec((tm, w), lambda i: (i, 0))
    vec = lambda w: pl.BlockSpec((1, w), lambda i: (0, 0))
    whole = pl.BlockSpec(memory_space=pltpu.VMEM)
    est = w_bf.size * 2 + 2 * d * LANES * 2 + 2 * tm * d * (2 + 4 * 4 + 4 + 2) + 4 * tm * LANES * 4 + 8 * tm * d * 4
    return pl.pallas_call(
        functools.partial(_post_mix_kernel, alpha=alpha),
        grid=(nt // tm,),
        in_specs=[row(d), row(d), _mod_spec(tm, d, 2, npb), _mod_spec(tm, d, 3, npb), _mod_spec(tm, d, 4, npb),
                  whole, vec(d), vec(d), whole, whole, vec(LANES)],
        out_specs=(row(d), row(d // 2), row(LANES), row(LANES)),
        out_shape=(jax.ShapeDtypeStruct((nt, d), F32), jax.ShapeDtypeStruct((nt, d // 2), I32),
                   jax.ShapeDtypeStruct((nt, LANES), I32), jax.ShapeDtypeStruct((nt, LANES), F32)),
        compiler_params=_cparams(("parallel",), est),
        name="post_mix",
    )(a, x_all, mb, mb, mb, w_bf, lng, lnb, rwh, rwl, rb)


def _gather_rows(src, idx):
    n, d = src.shape
    m = idx.shape[0]
    n_workers = SC_CORES * SC_SUBCORES
    per_w = m // n_workers
    rows = GATHER_BUF_BYTES // (d * src.dtype.itemsize)
    nb = GATHER_BUFS
    assert m % n_workers == 0 and per_w % (nb * rows) == 0
    mesh = plsc.VectorSubcoreMesh(core_axis_name="core", subcore_axis_name="subcore")

    @functools.partial(
        pl.kernel, out_type=jax.ShapeDtypeStruct((m, d), src.dtype), mesh=mesh, name="gather_rows",
        scratch_types=[pltpu.VMEM((per_w,), I32)] + [pltpu.VMEM((rows, d), src.dtype)] * nb
                      + [pltpu.SemaphoreType.DMA] * nb)
    def gather(src_hbm, idx_hbm, dst_hbm, idx_v, *bufs_sems):
        bufs, sems = bufs_sems[:nb], bufs_sems[nb:]
        base = (lax.axis_index("subcore") * SC_CORES + lax.axis_index("core")) * per_w
        pltpu.sync_copy(idx_hbm.at[pl.ds(base, per_w)], idx_v)

        def start_gather(off, b):
            return pltpu.async_copy(src_hbm.at[idx_v.at[pl.ds(off + b * rows, rows)]], bufs[b], sems[b])

        def start_write(off, b):
            return pltpu.async_copy(bufs[b], dst_hbm.at[pl.ds(base + off + b * rows, rows)], sems[b])

        @pl.loop(0, per_w, step=nb * rows)
        def _(off):
            gathers = [start_gather(off, 0), start_gather(off, 1)]
            writes = []
            for b in range(nb):
                gathers[b].wait()
                writes.append(start_write(off, b))
                if b + 2 < nb:
                    gathers.append(start_gather(off, b + 2))
            for w in writes:
                w.wait()

    return gather(src, idx)


def _expert_kernel(ie_ref, ib_ref, nv_ref, ob_ref, *refs, n_sub, tf, tn, n_up):
    x_refs = refs[:n_sub]
    wu_ref, bu_ref, wd_ref, bd_ref, o_ref, xb_ref, h_ref, wub_ref, wdp_ref, wdb_ref = refs[n_sub:]
    i, j = pl.program_id(0), pl.program_id(1)
    nv = nv_ref[i]
    half = LANES // 2
    d_half = xb_ref.shape[1] // 2
    n_hid = h_ref.shape[1]
    n_grp = min(MOE_ROW_GROUPS, n_sub)
    sizes = [n_sub // n_grp + (1 if g < n_sub % n_grp else 0) for g in range(n_grp)]
    firsts = [sum(sizes[:g]) for g in range(n_grp)]
    row_groups = [(f * MOE_ROWS, (f + s) * MOE_ROWS, f) for f, s in zip(firsts, sizes)]
    n_active = sum(jnp.where(nv > f, 1, 0) for f in firsts)

    @pl.when((j == 0) & (nv > 0))
    def _():
        for r in range(n_sub):
            rs = slice(r * MOE_ROWS, (r + 1) * MOE_ROWS)
            w = x_refs[r][...]
            xb_ref[rs, :d_half] = lax.bitcast_convert_type(jnp.left_shift(w, 16), F32).astype(BF16)
            xb_ref[rs, d_half:] = lax.bitcast_convert_type(w & HI16_MASK, F32).astype(BF16)

    @pl.when((j < n_up) & (nv > 0))
    def _():
        wub_ref[...] = wu_ref[0, 0].astype(BF16)
        bu = bu_ref[0, 0]
        col = pl.multiple_of(j * tf, tf)

        def up_rows(r0, r1):
            h = _dot(xb_ref[r0:r1, :], wub_ref[...]) + bu
            even = (lax.broadcasted_iota(I32, (r1 - r0, LANES), 1) % 2) == 0
            prods = []
            for t in range(2 * tf // LANES):
                capped = jnp.minimum(h[:, t * LANES:(t + 1) * LANES], SWIGLU_LIMIT)
                gate_act = capped / (1.0 + jnp.exp2(capped * (-SWIGLU_ALPHA * LOG2_E)))
                act = jnp.where(even, gate_act, jnp.maximum(capped, -SWIGLU_LIMIT) + 1.0)
                prods.append(act * pltpu.roll(act, LANES - 1, 1))
            comp = [jnp.where(even, prods[2 * t], pltpu.roll(prods[2 * t + 1], 1, 1))
                    for t in range(tf // LANES)]
            h_ref[r0:r1, pl.ds(col, tf)] = jnp.concatenate(comp, axis=1).astype(BF16)

        for k in range(1, n_grp + 1):
            @pl.when(n_active == k)
            def _(k=k):
                for r0, r1, _unused in row_groups[:k]:
                    up_rows(r0, r1)

    @pl.when((j >= n_up) & (nv > 0))
    def _():
        for c in range(tn // LANES):
            cs = slice(c * LANES, (c + 1) * LANES)
            for qd in range(n_hid // LANES):
                wdp_ref[pl.ds(qd * LANES, half, stride=2), :] = wd_ref[0, 0, qd * LANES:qd * LANES + half, cs]
                wdp_ref[pl.ds(qd * LANES + 1, half, stride=2), :] = wd_ref[0, 0, qd * LANES + half:(qd + 1) * LANES, cs]
            wdb_ref[:, cs] = wdp_ref[...].astype(BF16)
        bd = bd_ref[0, 0]
        for r0, r1, first_sub in row_groups:
            @pl.when(first_sub < nv)
            def _(r0=r0, r1=r1):
                o_ref[r0:r1, :] = _dot(h_ref[r0:r1, :], wdb_ref[...]) + bd

            @pl.when(first_sub >= nv)
            def _(r0=r0, r1=r1):
                o_ref[r0:r1, :] = jnp.zeros((r1 - r0, tn), F32)


def _experts(xs, plan, w_up, b_up, w_down, b_down, *, layer):
    item_e, item_b0, item_nv, item_ob, n_real = plan
    n_items = item_e.shape[0]
    d = w_up.shape[2]
    f = w_down.shape[2]
    tf = min(MOE_FF_TILE, f)
    tn = min(MOE_OUT_TILE, d)
    n_up, n_down = f // tf, d // tn
    n_sub = MOE_SUB
    sb = n_sub * MOE_ROWS
    assert xs.shape[1] * 2 == d and n_sub % 2 == 0

    def x_map(r):
        return lambda i, j, ie, ib, nv, ob: (jnp.where(r < nv[i], ib[i] + r, ib[i]), 0)

    def up_chunk(i, j, nv):
        return jnp.where(nv[i] > 0, jnp.minimum(j, n_up - 1), n_up - 1)

    def down_chunk(i, j, nv):
        return jnp.where(nv[i] > 0, jnp.maximum(j - n_up, 0), n_down - 1)

    in_specs = [pl.BlockSpec((MOE_ROWS, d // 2), x_map(r)) for r in range(n_sub)] + [
        pl.BlockSpec((1, 1, d, 2 * tf), lambda i, j, ie, ib, nv, ob: (layer, ie[i], 0, up_chunk(i, j, nv))),
        pl.BlockSpec((1, 1, 1, 2 * tf), lambda i, j, ie, ib, nv, ob: (layer, ie[i], 0, up_chunk(i, j, nv))),
        pl.BlockSpec((1, 1, f, tn), lambda i, j, ie, ib, nv, ob: (layer, ie[i], 0, down_chunk(i, j, nv))),
        pl.BlockSpec((1, 1, 1, tn), lambda i, j, ie, ib, nv, ob: (layer, ie[i], 0, down_chunk(i, j, nv))),
    ]
    est = 2 * sb * d * 2 + sb * (d + f) * 2 + 2 * (d * 2 * tf + f * tn) * 4 + (d * 2 * tf + f * tn) * 2 \
        + f * LANES * 4 + 2 * sb * tn * 4 + 6 * (sb // 2) * 2 * tf * 4
    depth, n_exp = w_up.shape[0], w_up.shape[1]
    return pl.pallas_call(
        functools.partial(_expert_kernel, n_sub=n_sub, tf=tf, tn=tn, n_up=n_up),
        grid_spec=pltpu.PrefetchScalarGridSpec(
            num_scalar_prefetch=4,
            grid=(n_real, n_up + n_down),
            in_specs=in_specs,
            out_specs=pl.BlockSpec((sb, tn), lambda i, j, ie, ib, nv, ob: (ob[i], down_chunk(i, j, nv))),
            scratch_shapes=[pltpu.VMEM((sb, d), BF16), pltpu.VMEM((sb, f), BF16), pltpu.VMEM((d, 2 * tf), BF16),
                            pltpu.VMEM((f, LANES), F32), pltpu.VMEM((f, tn), BF16)],
        ),
        out_shape=jax.ShapeDtypeStruct((n_items * sb, d), F32),
        compiler_params=_cparams(("arbitrary", "arbitrary"), est),
        name="experts",
    )(item_e, item_b0, item_nv, item_ob, *([xs] * n_sub), w_up,
      b_up.reshape(depth, n_exp, 1, 2 * f), w_down, b_down.reshape(depth, n_exp, 1, d))


def _combine_kernel(*refs, alpha):
    y_refs = refs[:TOP_K]
    tg_ref, x_ref, g_ref, lng_ref, lnb_ref, o_ref = refs[TOP_K:]
    tg = tg_ref[...]
    moe = jnp.zeros(x_ref.shape, F32)
    for r in range(TOP_K):
        moe = moe + tg[:, r:r + 1] * y_refs[r][...]
    o_ref[...] = _layer_norm_rows(alpha * x_ref[...] + g_ref[...] * moe, lng_ref[...], lnb_ref[...])


def _combine(y4, tg, x1, mb, lng, lnb, *, npb, alpha):
    nt, d = x1.shape
    tm = TOKEN_BLOCK
    nbt = nt // tm
    row = lambda w: pl.BlockSpec((tm, w), lambda i: (i, 0))
    vec = pl.BlockSpec((1, d), lambda i: (0, 0))
    y_specs = [pl.BlockSpec((tm, d), lambda i, r=r: (r * nbt + i, 0)) for r in range(TOP_K)]
    est = 2 * tm * (TOP_K * d + LANES + 3 * d) * 4 + 6 * tm * d * 4
    return pl.pallas_call(
        functools.partial(_combine_kernel, alpha=alpha),
        grid=(nbt,),
        in_specs=y_specs + [row(LANES), row(d), _mod_spec(tm, d, 5, npb), vec, vec],
        out_specs=row(d),
        out_shape=jax.ShapeDtypeStruct((nt, d), F32),
        compiler_params=_cparams(("parallel",), est),
        name="moe_combine",
    )(*([y4] * TOP_K), tg, x1, mb, lng, lnb)


def _moe_plan(top_e, n_exp):
    nt = top_e.shape[0]
    na = nt * TOP_K
    sb = MOE_SUB * MOE_ROWS
    e_flat = top_e.reshape(-1)
    order = jnp.argsort(e_flat).astype(I32)
    e_sorted = e_flat[order]
    experts = jnp.arange(n_exp, dtype=I32)
    counts = jnp.sum((e_flat[None, :] == experts[:, None]).astype(I32), axis=1)
    starts = jnp.cumsum(counts) - counts
    padded = (counts + MOE_ROWS - 1) // MOE_ROWS * MOE_ROWS
    gend = jnp.cumsum(padded)
    gstart = gend - padded
    rank = jnp.arange(na, dtype=I32) - starts[e_sorted]
    n_slot_blocks = na // MOE_ROWS + n_exp
    slots = jnp.arange(n_slot_blocks * MOE_ROWS, dtype=I32)
    e_slot = jnp.minimum(jnp.sum((gend[None, :] <= slots[:, None]).astype(I32), axis=1), n_exp - 1)
    pos = slots - gstart[e_slot]
    src = jnp.clip(starts[e_slot] + pos, 0, na - 1)
    slot_tok = jnp.where((pos >= 0) & (pos < counts[e_slot]), order[src] // TOP_K, 0).astype(I32)
    items_e = (counts + sb - 1) // sb
    item_end = jnp.cumsum(items_e)
    item_first = item_end - items_e
    out_row_sorted = (item_first[e_sorted] + rank // sb) * sb + rank % sb
    _, out_row_of_assign = lax.sort((order, out_row_sorted), num_keys=1)
    n_items = na // sb + n_exp
    ids = jnp.arange(n_items, dtype=I32)
    n_real = item_end[-1]
    is_real = ids < n_real
    ids_c = jnp.minimum(ids, n_real - 1).astype(I32)
    e_of = jnp.minimum(jnp.sum((item_end[None, :] <= ids_c[:, None]).astype(I32), axis=1), n_exp - 1)
    s_in_e = ids_c - item_first[e_of]
    rows_left = counts[e_of] - s_in_e * sb
    nvb = jnp.where(is_real, (jnp.minimum(rows_left, sb) + MOE_ROWS - 1) // MOE_ROWS, 0).astype(I32)
    blk0 = ((gstart[e_of] + s_in_e * sb) // MOE_ROWS).astype(I32)
    return slot_tok, out_row_of_assign, (e_of, blk0, nvb, ids_c, n_real.astype(I32))


def _rope_tables(pos, width):
    inv = ROPE_THETA ** (-jnp.arange(0, width, 2, dtype=F32) / width)
    ang = pos.astype(F32)[:, None] * inv[None, :]
    c, s = jnp.cos(ang), jnp.sin(ang)
    reps = LANES // width
    return jnp.tile(jnp.concatenate([c, c], axis=-1), (1, reps)), jnp.tile(jnp.concatenate([-s, s], axis=-1), (1, reps))


def _pad_lanes(v, fill=0.0):
    return jnp.full((1, LANES), fill, F32).at[0, :v.shape[0]].set(v)


def kernel(x_prompt, x_sample, cache_k, cache_v, cache_kidx, state_conv, page_table, c_prompt, c_sample, ada_w, ada_b, ln_g, ln_b, attn_w_in, attn_kidx_g, attn_kidx_b, attn_w_out, conv_w_in, conv_kernel, conv_w_out, router_w, router_b, expert_w_up, expert_b_up, expert_w_down, expert_b_down):
    bp, n_prompt, d = x_prompt.shape
    b, t_new, _ = x_sample.shape
    depth = ada_w.shape[0]
    n_kv = cache_k.shape[3]
    page = cache_k.shape[2]
    n_pages = page_table.shape[1]
    past = n_pages * page
    n_exp = router_w.shape[-1]
    n_sample = b * t_new
    nt = n_prompt + n_sample
    tm = TOKEN_BLOCK
    assert bp == 1 and t_new == SUBLANES and n_prompt % tm == 0 and n_sample % tm == 0 and n_exp <= LANES
    assert d % 512 == 0 and (IDX_HEADS * IDX_DIM) % 512 == 0 and n_prompt % min(KEY_CHUNK, n_prompt) == 0
    npb = n_prompt // tm
    alpha = (2.0 * depth) ** 0.25
    kvw = n_kv * HEAD_DIM
    iw = IDX_HEADS * IDX_DIM

    x_all = jnp.concatenate([x_prompt.reshape(n_prompt, d), x_sample.reshape(n_sample, d)], axis=0)
    c_all = jnp.concatenate([c_prompt, c_sample], axis=0)
    mc = -(-c_all.shape[0] // SUBLANES) * SUBLANES
    c_all = jnp.pad(c_all, ((0, mc - c_all.shape[0]), (0, 0)))
    mod = _ada_mod(c_all, ada_w, ada_b)

    pos = jnp.concatenate([jnp.arange(n_prompt, dtype=I32), jnp.tile(past + jnp.arange(t_new, dtype=I32), b)])
    tabs = _rope_tables(pos, HEAD_DIM) + _rope_tables(pos, IDX_DIM)
    pt_flat = page_table.reshape(-1).astype(I32)
    n_pool = cache_k.shape[1]
    n_cache = cache_k.shape[0] * n_pool
    cache_k2 = cache_k.reshape(n_cache, page * n_kv, HEAD_DIM)
    cache_v2 = cache_v.reshape(n_cache, page * n_kv, HEAD_DIM)
    cache_kit = jnp.swapaxes(cache_kidx.reshape(n_cache, page, IDX_DIM), 1, 2)

    outs = dict(kp=[], vp=[], kip=[], ks=[], vs=[], kis=[], cp=[], cs=[])
    for i in range(depth):
        mb = jnp.concatenate([jnp.broadcast_to(mod[i, 0:1], (tm, 6 * d)),
                              jnp.repeat(mod[i, 1:1 + b], t_new, axis=0)], axis=0)
        if i % 2 == 0:
            a = i // 2
            w_pad = jnp.pad(attn_w_in[a], ((0, 0), (0, LANES - IDX_DIM - IDX_HEADS))).astype(BF16)
            q, k32, v32, kb, vb, qa, qb, kw, ke = _proj_attn(
                x_all, mb, w_pad, tabs, _pad_lanes(attn_kidx_g[a]), _pad_lanes(attn_kidx_b[a]), npb=npb, n_kv=n_kv)
            topk_p = min(TOPK_MAX, n_prompt // TOPK_DIV)
            o_p = _dsa_prompt(q, qa, qb, kw, ke[:n_prompt], kb[:n_prompt], vb[:n_prompt],
                              n_prompt=n_prompt, topk=topk_p, n_kv=n_kv)
            seq3 = lambda z: z[n_prompt:].astype(F32).reshape(b, t_new, z.shape[1])
            keys = _dsa_sample_scores(pt_flat, seq3(qa), seq3(qb), seq3(kw), seq3(ke),
                                      cache_kit, n_pages=n_pages, page0=a * n_pool)
            topk_s = min(TOPK_MAX, (past + t_new) // TOPK_DIV)
            thr = _thresholds(keys.reshape(n_sample, -1), topk=topk_s)
            o_s = _dsa_sample_attn(pt_flat, seq3(q), keys, thr.reshape(b, t_new, LANES), seq3(k32), seq3(v32),
                                   cache_k2, cache_v2, n_pages=n_pages, n_kv=n_kv, page0=a * n_pool)
            mix_in = jnp.concatenate([o_p, o_s.reshape(n_sample, d).astype(BF16)], axis=0)
            w_out = attn_w_out[a].astype(BF16)
            outs['kp'].append(k32[:n_prompt].reshape(1, n_prompt, n_kv, HEAD_DIM))
            outs['vp'].append(v32[:n_prompt].reshape(1, n_prompt, n_kv, HEAD_DIM))
            outs['kip'].append(kw[:n_prompt, :IDX_DIM].reshape(1, n_prompt, IDX_DIM))
            outs['ks'].append(k32[n_prompt:].reshape(b, t_new, n_kv, HEAD_DIM))
            outs['vs'].append(v32[n_prompt:].reshape(b, t_new, n_kv, HEAD_DIM))
            outs['kis'].append(kw[n_prompt:, :IDX_DIM].reshape(b, t_new, IDX_DIM))
        else:
            ci = i // 2
            bgate, v = _proj_conv(x_all, mb, conv_w_in[ci].astype(BF16), npb=npb)
            zrow = jnp.zeros((1, d), F32)
            prev1 = jnp.concatenate([zrow, v[tm - 1:n_prompt - 1:tm]], axis=0)
            prev2 = jnp.concatenate([zrow, v[tm - 2:n_prompt - 2:tm]], axis=0)
            mix_p = _conv_mix(v, bgate, jnp.repeat(prev1, SUBLANES, axis=0), jnp.repeat(prev2, SUBLANES, axis=0),
                              conv_kernel[ci], period=tm, row0_blocks=0, n_rows=n_prompt, prev_per_block=True)
            st = state_conv[ci]
            mix_s = _conv_mix(v, bgate, jnp.repeat(st[:, 1], t_new, axis=0), jnp.repeat(st[:, 0], t_new, axis=0),
                              conv_kernel[ci], period=t_new, row0_blocks=npb, n_rows=n_sample, prev_per_block=False)
            mix_in = jnp.concatenate([mix_p, mix_s], axis=0)
            w_out = conv_w_out[ci].astype(BF16)
            outs['cp'].append(v[n_prompt - (CONV_W - 1):n_prompt].reshape(1, CONV_W - 1, d))
            outs['cs'].append(v[n_prompt:].reshape(b, t_new, d)[:, t_new - (CONV_W - 1):])

        rw = jnp.pad(router_w[i], ((0, 0), (0, LANES - n_exp)))
        rwh = rw.astype(BF16)
        rwl = (rw - rwh.astype(F32)).astype(BF16)
        rb = _pad_lanes(router_b[i], fill=-1e30)
        x1, u2, ti, tg = _post_mix(mix_in, x_all, mb, w_out, ln_g[i, 0:1], ln_b[i, 0:1], rwh, rwl, rb,
                                   npb=npb, alpha=alpha)
        slot_tok, out_row_of_assign, plan = _moe_plan(ti[:, :TOP_K], n_exp)
        xs = _gather_rows(u2, slot_tok)
        ys = _experts(xs, plan, expert_w_up, expert_b_up, expert_w_down, expert_b_down, layer=i)
        y4 = _gather_rows(ys, out_row_of_assign.reshape(nt, TOP_K).T.reshape(-1))
        x_all = _combine(y4, tg, x1, mb, ln_g[i, 1:2], ln_b[i, 1:2], npb=npb, alpha=alpha)

    y_prompt = x_all[:n_prompt].reshape(1, n_prompt, d)
    y_sample = x_all[n_prompt:].reshape(b, t_new, d)
    st = lambda name: jnp.stack(outs[name])
    return (y_prompt, y_sample, st('kp'), st('vp'), st('kip'), st('ks'), st('vs'), st('kis'), st('cp'), st('cs'))
```

```python
import functools

import jax
import jax.numpy as jnp
from jax import lax
from jax.experimental import pallas as pl
from jax.experimental.pallas import tpu as pltpu
from jax.experimental.pallas import tpu_sc as plsc

F32 = jnp.float32
BF16 = jnp.bfloat16
I32 = jnp.int32

HEAD_DIM = 128
IDX_HEADS = 16
IDX_DIM = 64
TOPK_MAX = 256
TOPK_DIV = 4
CONV_W = 3
TOP_K = 4
SWIGLU_LIMIT = 7.0
SWIGLU_ALPHA = 1.702
ROPE_THETA = 10000.0
LN_EPS = 1e-5

LANES = 128
SUBLANES = 8
SC_CORES = 2
SC_SUBCORES = 16
V7X_VMEM_BYTES = 64 * 1024 * 1024
VMEM_CAP_BYTES = V7X_VMEM_BYTES - 8 * 1024 * 1024

TOKEN_BLOCK = 256
Q_TILE = 128
KEY_CHUNK = 512
ATT_CHUNK = 256
SEL_ROWS = 256
MOE_ROWS = 128
MOE_SUB = 10
MOE_ROW_GROUPS = 3
MOE_FF_TILE = 256
MOE_OUT_TILE = 512
GATHER_BUFS = 4
GATHER_BUF_BYTES = 64 * 1024
ADA_N_TILE = 1024

INT_MIN = -2 ** 31
HI16_MASK = -65536
LOG2_E = 1.4426950408889634
QK_SCALE_LOG2 = (HEAD_DIM ** -0.5) * LOG2_E
KEY_NEG_INF = -2139095041
NEG_BIAS = -2e30
M_FLOOR = -1e30


def _cparams(sem, est_bytes):
    limit = int(min(max(est_bytes, 16 * 1024 * 1024), VMEM_CAP_BYTES))
    return pltpu.CompilerParams(dimension_semantics=sem, vmem_limit_bytes=limit)


def _dot(a, b):
    return jnp.dot(a, b, preferred_element_type=F32)


def _dot_nt(a, b):
    return lax.dot_general(a, b, (((1,), (1,)), ((), ())), preferred_element_type=F32)


def _rep_lanes(x, n):
    return x if n == 1 else jnp.concatenate([x] * n, axis=1)


def _layer_norm_rows(y, g, b):
    mu = jnp.mean(y, axis=-1, keepdims=True)
    d = y - mu
    var = jnp.mean(d * d, axis=-1, keepdims=True)
    return d * lax.rsqrt(var + LN_EPS) * g + b


def _sort_key(s):
    bits = lax.bitcast_convert_type(s, I32)
    return bits ^ (jnp.right_shift(bits, 31) & 0x7FFFFFFF)


def _ada_kernel(c_ref, w_ref, b_ref, o_ref):
    c = c_ref[...]
    a = (c * jax.nn.sigmoid(c)).astype(BF16)
    o_ref[0] = _dot(a, w_ref[0].astype(BF16)) + b_ref[0]


def _ada_mod(c_all, ada_w, ada_b):
    depth, d, n6 = ada_w.shape
    mc = c_all.shape[0]
    tn = ADA_N_TILE
    est = 2 * (d * tn * 4 + mc * tn * 4) + mc * d * 4 * 2 + d * tn * 2
    return pl.pallas_call(
        _ada_kernel,
        grid=(depth, n6 // tn),
        in_specs=[
            pl.BlockSpec((mc, d), lambda l, j: (0, 0)),
            pl.BlockSpec((1, d, tn), lambda l, j: (l, 0, j)),
            pl.BlockSpec((1, 1, tn), lambda l, j: (l, 0, j)),
        ],
        out_specs=pl.BlockSpec((1, mc, tn), lambda l, j: (l, 0, j)),
        out_shape=jax.ShapeDtypeStruct((depth, mc, n6), F32),
        compiler_params=_cparams(("parallel", "parallel"), est),
        name="ada_mod",
    )(c_all, ada_w, ada_b.reshape(depth, 1, n6))


def _proj_attn_kernel(x_ref, sh_ref, sc_ref, w_ref, c128_ref, s128_ref, c64_ref, s64_ref, kng_ref, knb_ref,
                      q_ref, k_ref, v_ref, kb_ref, vb_ref, qa_ref, qb_ref, kw_ref, ke_ref, *, d_model, n_kv):
    xb = (x_ref[...] * (1.0 + sc_ref[...]) + sh_ref[...]).astype(BF16)
    c128, s128 = c128_ref[...], s128_ref[...]
    c64, s64 = c64_ref[...], s64_ref[...]
    tm = xb.shape[0]
    lane = lax.broadcasted_iota(I32, (tm, LANES), 1)
    low_half = (lane % IDX_DIM) < (IDX_DIM // 2)

    def rope128(y):
        return y * c128 + pltpu.roll(y, HEAD_DIM // 2, 1) * s128

    def rope64(y):
        rot = jnp.where(low_half, pltpu.roll(y, LANES - IDX_DIM // 2, 1), pltpu.roll(y, IDX_DIM // 2, 1))
        return y * c64 + rot * s64

    kvw = n_kv * HEAD_DIM
    col = 0
    for c0 in range(0, d_model, 512):
        y = _dot(xb, w_ref[:, col + c0:col + c0 + 512])
        for t in range(4):
            r = rope128(y[:, t * LANES:(t + 1) * LANES]) * QK_SCALE_LOG2
            q_ref[:, c0 + t * LANES:c0 + (t + 1) * LANES] = r.astype(BF16)
    col += d_model
    y = _dot(xb, w_ref[:, col:col + kvw])
    for t in range(n_kv):
        r = rope128(y[:, t * LANES:(t + 1) * LANES])
        k_ref[:, t * LANES:(t + 1) * LANES] = r
        kb_ref[:, t * LANES:(t + 1) * LANES] = r.astype(BF16)
    col += kvw
    y = _dot(xb, w_ref[:, col:col + kvw])
    v_ref[...] = y
    vb_ref[...] = y.astype(BF16)
    col += kvw
    iw = IDX_HEADS * IDX_DIM
    for c0 in range(0, iw, 512):
        y = _dot(xb, w_ref[:, col + c0:col + c0 + 512])
        for t in range(4):
            r = rope64(y[:, t * LANES:(t + 1) * LANES])
            qa_ref[:, c0 + t * LANES:c0 + (t + 1) * LANES] = r.astype(BF16)
            qb_ref[:, c0 + t * LANES:c0 + (t + 1) * LANES] = pltpu.roll(r, IDX_DIM, 1).astype(BF16)
    col += iw
    y = _dot(xb, w_ref[:, col:col + LANES])
    is_key = lane < IDX_DIM
    mu = jnp.sum(jnp.where(is_key, y, 0.0), axis=-1, keepdims=True) * (1.0 / IDX_DIM)
    dlt = jnp.where(is_key, y - mu, 0.0)
    var = jnp.sum(dlt * dlt, axis=-1, keepdims=True) * (1.0 / IDX_DIM)
    kn = dlt * lax.rsqrt(var + LN_EPS) * kng_ref[...] + knb_ref[...]
    ki = rope64(kn)
    wscale = (IDX_HEADS ** -0.5) * (IDX_DIM ** -0.5)
    wh = jnp.where((lane >= IDX_DIM) & (lane < IDX_DIM + IDX_HEADS), y * wscale, 0.0)
    kw_ref[...] = ki + wh
    ke_ref[...] = ki.astype(BF16)


def _mod_spec(tm, d, col, npb):
    return pl.BlockSpec((tm, d), lambda i: (jnp.where(i < npb, 0, i - npb + 1), col))


def _proj_attn(x_all, mb, w_pad, tabs, kng, knb, *, npb, n_kv):
    nt, d = x_all.shape
    tm = TOKEN_BLOCK
    kvw = n_kv * HEAD_DIM
    iw = IDX_HEADS * IDX_DIM
    row = lambda w: pl.BlockSpec((tm, w), lambda i: (i, 0))
    tab = pl.BlockSpec((tm, LANES), lambda i: (i, 0))
    vec = pl.BlockSpec((1, LANES), lambda i: (0, 0))
    out_shapes = (
        jax.ShapeDtypeStruct((nt, d), BF16),
        jax.ShapeDtypeStruct((nt, kvw), F32),
        jax.ShapeDtypeStruct((nt, kvw), F32),
        jax.ShapeDtypeStruct((nt, kvw), BF16),
        jax.ShapeDtypeStruct((nt, kvw), BF16),
        jax.ShapeDtypeStruct((nt, iw), BF16),
        jax.ShapeDtypeStruct((nt, iw), BF16),
        jax.ShapeDtypeStruct((nt, LANES), F32),
        jax.ShapeDtypeStruct((nt, LANES), BF16),
    )
    out_specs = (row(d), row(kvw), row(kvw), row(kvw), row(kvw), row(iw), row(iw), row(LANES), row(LANES))
    est = w_pad.size * 2 + 2 * tm * (3 * d * 4 + 4 * LANES * 4) + 2 * tm * (d * 2 + kvw * 12 + iw * 4 + LANES * 6) \
        + 8 * tm * 512 * 4
    return pl.pallas_call(
        functools.partial(_proj_attn_kernel, d_model=d, n_kv=n_kv),
        grid=(nt // tm,),
        in_specs=[row(d), _mod_spec(tm, d, 0, npb), _mod_spec(tm, d, 1, npb),
                  pl.BlockSpec(memory_space=pltpu.VMEM), tab, tab, tab, tab, vec, vec],
        out_specs=out_specs,
        out_shape=out_shapes,
        compiler_params=_cparams(("parallel",), est),
        name="proj_attn",
    )(x_all, mb, mb, w_pad, *tabs, kng, knb)


def _kth_largest_key(key_ref, n_chunks, chunk, kk):
    rows = key_ref.shape[0]

    def count_ge(cand):
        candb = jnp.broadcast_to(cand, (rows, LANES))

        def body(c, acc):
            base = pl.multiple_of(c * chunk, chunk)
            for j in range(chunk // LANES):
                blk = key_ref[:, pl.ds(base + j * LANES, LANES)]
                acc = acc + jnp.where(blk >= candb, 1.0, 0.0)
            return acc

        acc = lax.fori_loop(0, n_chunks, body, jnp.zeros((rows, LANES), F32))
        return jnp.sum(acc, axis=1, keepdims=True)

    kkf = float(kk)
    c0 = count_ge(jnp.zeros((rows, 1), I32))
    prefix = jnp.where(c0 >= kkf, 0, INT_MIN).astype(I32)
    n_all = float(n_chunks * chunk) if isinstance(n_chunks, int) else (n_chunks * chunk).astype(F32)
    cnt = jnp.where(c0 >= kkf, c0, n_all)

    def unsettled(cnt):
        return jnp.max(jnp.abs(cnt - kkf)).astype(I32)

    def cond(carry):
        b, _, _, open_rows = carry
        return (b < 31) & (open_rows > 0)

    def bit_body(carry):
        b, prefix, cnt, _ = carry
        cand = prefix | jnp.left_shift(jnp.int32(1), 30 - b)
        c = count_ge(cand)
        keep = c >= kkf
        cnt = jnp.where(keep, c, cnt)
        return b + 1, jnp.where(keep, cand, prefix), cnt, unsettled(cnt)

    _, prefix, _, _ = lax.while_loop(cond, bit_body, (jnp.int32(0), prefix, cnt, unsettled(cnt)))
    return prefix


def _dsa_prompt_kernel(q_ref, qa_ref, qb_ref, kw_ref, ke_ref, k_ref, v_ref, o_ref,
                       key_ref, whb_ref, qs_ref, m_ref, l_ref, acc_ref, *, tq, tc, topk, n_kv, n_rep):
    i = pl.program_id(0)
    q0 = i * tq
    n_ch = (q0 + tq + tc - 1) // tc
    kw = kw_ref[...]
    for h in range(IDX_HEADS):
        whb_ref[h] = jnp.broadcast_to(kw[:, IDX_DIM + h:IDX_DIM + h + 1], (tq, LANES))
    qpos = q0 + lax.broadcasted_iota(I32, (tq, LANES), 0)
    lane = lax.broadcasted_iota(I32, (tq, LANES), 1)
    n_sub = tc // LANES

    def score_body(c, carry):
        base = pl.multiple_of(c * tc, tc)
        ke = ke_ref[pl.ds(base, tc), :]
        acc = [jnp.zeros((tq, LANES), F32) for _ in range(n_sub)]
        for p in range(IDX_HEADS // 2):
            d_even = _dot_nt(qa_ref[:, p * LANES:(p + 1) * LANES], ke)
            d_odd = _dot_nt(qb_ref[:, p * LANES:(p + 1) * LANES], ke)
            w_even, w_odd = whb_ref[2 * p], whb_ref[2 * p + 1]
            for j in range(n_sub):
                sl = slice(j * LANES, (j + 1) * LANES)
                acc[j] = acc[j] + jnp.maximum(d_even[:, sl], 0.0) * w_even + jnp.maximum(d_odd[:, sl], 0.0) * w_odd
        for j in range(n_sub):
            kpos = base + j * LANES + lane
            key_ref[:, pl.ds(base + j * LANES, LANES)] = jnp.where(kpos <= qpos, _sort_key(acc[j]), INT_MIN)
        return carry

    lax.fori_loop(0, n_ch, score_body, 0)

    thr = jnp.maximum(_kth_largest_key(key_ref, n_ch, tc, topk), KEY_NEG_INF + 1)
    thrb = jnp.broadcast_to(thr, (tq, LANES))
    neg_bits = lax.bitcast_convert_type(jnp.float32(NEG_BIAS), I32)

    def bias_body(c, carry):
        base = pl.multiple_of(c * tc, tc)
        for j in range(n_sub):
            sl = pl.ds(base + j * LANES, LANES)
            key_ref[:, sl] = jnp.where(key_ref[:, sl] >= thrb, 0, neg_bits)
        return carry

    lax.fori_loop(0, n_ch, bias_body, 0)

    for g in range(n_kv):
        qs_ref[g] = jnp.concatenate(
            [q_ref[:, (g * n_rep + r) * LANES:(g * n_rep + r + 1) * LANES] for r in range(n_rep)], axis=0)
    m_ref[...] = jnp.full(m_ref.shape, M_FLOOR, F32)
    l_ref[...] = jnp.zeros(l_ref.shape, F32)
    acc_ref[...] = jnp.zeros(acc_ref.shape, F32)

    ta = min(ATT_CHUNK, tc)
    n_sub_a = ta // LANES

    def att_body(c, carry):
        base = pl.multiple_of(c * ta, ta)
        bias = lax.bitcast_convert_type(key_ref[:, pl.ds(base, ta)], F32)
        bias = jnp.concatenate([bias] * n_rep, axis=0)
        for g in range(n_kv):
            kc = k_ref[pl.ds(base, ta), g * LANES:(g + 1) * LANES]
            vc = v_ref[pl.ds(base, ta), g * LANES:(g + 1) * LANES]
            s = _dot_nt(qs_ref[g], kc) + bias
            m_old = m_ref[g]
            m_new = jnp.maximum(m_old, jnp.max(s, axis=1, keepdims=True))
            alpha = jnp.exp2(m_old - m_new)
            p = jnp.exp2(s - _rep_lanes(m_new, n_sub_a))
            l_ref[g] = alpha * l_ref[g] + jnp.sum(p, axis=1, keepdims=True)
            acc_ref[g] = alpha * acc_ref[g] + _dot(p.astype(BF16), vc)
            m_ref[g] = m_new
        return carry

    lax.fori_loop(0, n_ch * (tc // ta), att_body, 0)
    for g in range(n_kv):
        o = acc_ref[g] / l_ref[g]
        for r in range(n_rep):
            h = g * n_rep + r
            o_ref[:, h * LANES:(h + 1) * LANES] = o[r * tq:(r + 1) * tq].astype(BF16)


def _dsa_prompt(q, qa, qb, kw, ke, kb, vb, *, n_prompt, topk, n_kv):
    d = q.shape[1]
    n_rep = d // HEAD_DIM // n_kv
    tq, tc = Q_TILE, min(KEY_CHUNK, n_prompt)
    iw = IDX_HEADS * IDX_DIM
    kvw = n_kv * HEAD_DIM
    whole = pl.BlockSpec(memory_space=pltpu.VMEM)
    row = lambda w: pl.BlockSpec((tq, w), lambda i: (i, 0))
    est = n_prompt * (LANES * 2 + kvw * 4) + tq * n_prompt * 4 + IDX_HEADS * tq * LANES * 4 \
        + 3 * n_rep * tq * LANES * 4 + 2 * tq * (2 * d * 2 + 2 * iw * 2 + LANES * 4) + 10 * n_rep * tq * tc * 4
    return pl.pallas_call(
        functools.partial(_dsa_prompt_kernel, tq=tq, tc=tc, topk=topk, n_kv=n_kv, n_rep=n_rep),
        grid=(n_prompt // tq,),
        in_specs=[row(d), row(iw), row(iw), row(LANES), whole, whole, whole],
        out_specs=row(d),
        out_shape=jax.ShapeDtypeStruct((n_prompt, d), BF16),
        scratch_shapes=[
            pltpu.VMEM((tq, n_prompt), I32),
            pltpu.VMEM((IDX_HEADS, tq, LANES), F32),
            pltpu.VMEM((n_kv, n_rep * tq, LANES), BF16),
            pltpu.VMEM((n_kv, n_rep * tq, LANES), F32),
            pltpu.VMEM((n_kv, n_rep * tq, LANES), F32),
            pltpu.VMEM((n_kv, n_rep * tq, LANES), F32),
        ],
        compiler_params=_cparams(("parallel",), est),
        name="dsa_prompt",
    )(q, qa, qb, kw, ke, kb, vb)


def _dsa_sample_score_kernel(pt_ref, qa_ref, qb_ref, kw_ref, ken_ref, *refs, n_pages, page, t_new):
    page_refs, key_ref = refs[:n_pages], refs[n_pages]
    qa = qa_ref[0]
    qb = qb_ref[0]
    kw = kw_ref[0]
    n_pair = IDX_HEADS // 2
    q_even = jnp.concatenate([qa[:, p * LANES:(p + 1) * LANES] for p in range(n_pair)], axis=0).astype(BF16)
    q_odd = jnp.concatenate([qb[:, p * LANES:(p + 1) * LANES] for p in range(n_pair)], axis=0).astype(BF16)
    w_even = [jnp.broadcast_to(kw[:, IDX_DIM + 2 * p:IDX_DIM + 2 * p + 1], (t_new, LANES)) for p in range(n_pair)]
    w_odd = [jnp.broadcast_to(kw[:, IDX_DIM + 2 * p + 1:IDX_DIM + 2 * p + 2], (t_new, LANES)) for p in range(n_pair)]

    def scores(d_even, d_odd):
        s = jnp.zeros((t_new, LANES), F32)
        for p in range(n_pair):
            s = s + jnp.maximum(d_even[p * t_new:(p + 1) * t_new], 0.0) * w_even[p] \
                  + jnp.maximum(d_odd[p * t_new:(p + 1) * t_new], 0.0) * w_odd[p]
        return s

    zeros = jnp.zeros((LANES - IDX_DIM, page), F32)
    for pg in range(n_pages):
        ket = jnp.concatenate([page_refs[pg][0], zeros], axis=0).astype(BF16)
        key_ref[0, :, pg * page:(pg + 1) * page] = _sort_key(scores(_dot(q_even, ket), _dot(q_odd, ket)))
    ke_new = jnp.concatenate([ken_ref[0], jnp.zeros((LANES - t_new, LANES), F32)], axis=0).astype(BF16)
    s_new = scores(_dot_nt(q_even, ke_new), _dot_nt(q_odd, ke_new))
    qi = lax.broadcasted_iota(I32, (t_new, LANES), 0)
    kj = lax.broadcasted_iota(I32, (t_new, LANES), 1)
    key_ref[0, :, n_pages * page:n_pages * page + LANES] = jnp.where(kj <= qi, _sort_key(s_new), INT_MIN)


def _dsa_sample_scores(pt_flat, qa_s, qb_s, kw_s, ke_new, cache_kit, *, n_pages, page0):
    b, t_new, iw = qa_s.shape
    page = cache_kit.shape[2]
    width = n_pages * page + LANES
    seq = lambda w: pl.BlockSpec((1, t_new, w), lambda s, pt: (s, 0, 0))
    page_specs = [pl.BlockSpec((1, IDX_DIM, page), lambda s, pt, pg=pg: (page0 + pt[s * n_pages + pg], 0, 0))
                  for pg in range(n_pages)]
    est = 2 * (n_pages * page * LANES * 4 + t_new * (2 * iw + 2 * LANES + width) * 4) + 64 * page * LANES * 4
    return pl.pallas_call(
        functools.partial(_dsa_sample_score_kernel, n_pages=n_pages, page=page, t_new=t_new),
        grid_spec=pltpu.PrefetchScalarGridSpec(
            num_scalar_prefetch=1,
            grid=(b,),
            in_specs=[seq(iw), seq(iw), seq(LANES), seq(LANES)] + page_specs,
            out_specs=seq(width),
        ),
        out_shape=jax.ShapeDtypeStruct((b, t_new, width), I32),
        compiler_params=_cparams(("parallel",), est),
        name="dsa_sample_scores",
    )(pt_flat, qa_s, qb_s, kw_s, ke_new, *([cache_kit] * n_pages))


def _threshold_kernel(key_ref, thr_ref, *, topk, n_tiles):
    thr = jnp.maximum(_kth_largest_key(key_ref, n_tiles, LANES, topk), KEY_NEG_INF + 1)
    thr_ref[...] = jnp.broadcast_to(thr, thr_ref.shape)


def _thresholds(keys, *, topk):
    n, width = keys.shape
    tr = min(SEL_ROWS, n)
    est = 2 * tr * (width + LANES) * 4 + 8 * tr * LANES * 4
    return pl.pallas_call(
        functools.partial(_threshold_kernel, topk=topk, n_tiles=width // LANES),
        grid=(n // tr,),
        in_specs=[pl.BlockSpec((tr, width), lambda i: (i, 0))],
        out_specs=pl.BlockSpec((tr, LANES), lambda i: (i, 0)),
        out_shape=jax.ShapeDtypeStruct((n, LANES), I32),
        compiler_params=_cparams(("parallel",), est),
        name="topk_threshold",
    )(keys)


def _dsa_sample_attn_kernel(pt_ref, q_ref, key_ref, thr_ref, kn_ref, vn_ref, *refs,
                            n_pages, page, t_new, n_kv, n_rep):
    k_refs, v_refs = refs[:n_pages], refs[n_pages:2 * n_pages]
    o_ref, kall_ref, vall_ref = refs[2 * n_pages:]
    n_heads = n_kv * n_rep
    kvw = n_kv * HEAD_DIM
    past = n_pages * page
    for pg in range(n_pages):
        for g in range(n_kv):
            rows_g = pl.ds(g, page, stride=n_kv)
            kall_ref[pg * page:(pg + 1) * page, g * LANES:(g + 1) * LANES] = k_refs[pg][0, rows_g, :].astype(BF16)
            vall_ref[pg * page:(pg + 1) * page, g * LANES:(g + 1) * LANES] = v_refs[pg][0, rows_g, :].astype(BF16)
    pad = jnp.zeros((LANES - t_new, kvw), F32)
    kall_ref[past:past + LANES, :] = jnp.concatenate([kn_ref[0], pad], axis=0).astype(BF16)
    vall_ref[past:past + LANES, :] = jnp.concatenate([vn_ref[0], pad], axis=0).astype(BF16)
    q = q_ref[0]
    zero = jnp.zeros((t_new, LANES), F32)
    rows = []
    for h in range(n_heads):
        g = h // n_rep
        rows.append(jnp.concatenate([q[:, h * LANES:(h + 1) * LANES] if gg == g else zero for gg in range(n_kv)], axis=1))
    qbd = jnp.concatenate(rows, axis=0).astype(BF16)
    thr = thr_ref[0]
    width = past + LANES
    sel = key_ref[0] >= _rep_lanes(thr, width // LANES)
    bias = jnp.where(sel, 0.0, NEG_BIAS)
    s = _dot_nt(qbd, kall_ref[...]) + jnp.concatenate([bias] * n_heads, axis=0)
    m = jnp.maximum(jnp.max(s, axis=1, keepdims=True), M_FLOOR)
    p = jnp.exp2(s - m)
    l = jnp.sum(p, axis=1, keepdims=True)
    o = _dot(p.astype(BF16), vall_ref[...]) / l
    for h in range(n_heads):
        g = h // n_rep
        o_ref[0, :, h * LANES:(h + 1) * LANES] = o[h * t_new:(h + 1) * t_new, g * LANES:(g + 1) * LANES]


def _dsa_sample_attn(pt_flat, q_s, keys, thr, k_new, v_new, cache_k, cache_v, *, n_pages, n_kv, page0):
    b, t_new, d = q_s.shape
    page = cache_k.shape[1] // n_kv
    kvw = n_kv * HEAD_DIM
    n_rep = d // HEAD_DIM // n_kv
    width = n_pages * page + LANES
    seq = lambda w: pl.BlockSpec((1, t_new, w), lambda s, pt: (s, 0, 0))
    page_specs = [pl.BlockSpec((1, page * n_kv, HEAD_DIM), lambda s, pt, pg=pg: (page0 + pt[s * n_pages + pg], 0, 0))
                  for pg in range(n_pages)]
    est = 2 * (2 * n_pages * page * kvw * 4 + t_new * (2 * d + width + LANES + 2 * kvw) * 4) \
        + 2 * width * kvw * 2 + 6 * (d // HEAD_DIM) * t_new * width * 4
    return pl.pallas_call(
        functools.partial(_dsa_sample_attn_kernel, n_pages=n_pages, page=page, t_new=t_new, n_kv=n_kv, n_rep=n_rep),
        grid_spec=pltpu.PrefetchScalarGridSpec(
            num_scalar_prefetch=1,
            grid=(b,),
            in_specs=[seq(d), seq(width), seq(LANES), seq(kvw), seq(kvw)] + page_specs + page_specs,
            out_specs=seq(d),
            scratch_shapes=[pltpu.VMEM((width, kvw), BF16), pltpu.VMEM((width, kvw), BF16)],
        ),
        out_shape=jax.ShapeDtypeStruct((b, t_new, d), F32),
        compiler_params=_cparams(("parallel",), est),
        name="dsa_sample_attn",
    )(pt_flat, q_s, keys, thr, k_new, v_new, *([cache_k] * n_pages), *([cache_v] * n_pages))


def _proj_conv_kernel(x_ref, sh_ref, sc_ref, w_ref, b_ref, v_ref, *, d_model):
    xb = (x_ref[...] * (1.0 + sc_ref[...]) + sh_ref[...]).astype(BF16)
    for c0 in range(0, d_model, 512):
        sl = slice(c0, c0 + 512)
        b_ref[:, sl] = _dot(xb, w_ref[:, c0:c0 + 512])
        cg = _dot(xb, w_ref[:, d_model + c0:d_model + c0 + 512])
        xi = _dot(xb, w_ref[:, 2 * d_model + c0:2 * d_model + c0 + 512])
        v_ref[:, sl] = cg * xi


def _proj_conv(x_all, mb, w_bf, *, npb):
    nt, d = x_all.shape
    tm = TOKEN_BLOCK
    row = pl.BlockSpec((tm, d), lambda i: (i, 0))
    est = w_bf.size * 2 + 2 * tm * d * 4 * 5 + 8 * tm * 512 * 4
    return pl.pallas_call(
        functools.partial(_proj_conv_kernel, d_model=d),
        grid=(nt // tm,),
        in_specs=[row, _mod_spec(tm, d, 0, npb), _mod_spec(tm, d, 1, npb), pl.BlockSpec(memory_space=pltpu.VMEM)],
        out_specs=(row, row),
        out_shape=(jax.ShapeDtypeStruct((nt, d), F32), jax.ShapeDtypeStruct((nt, d), F32)),
        compiler_params=_cparams(("parallel",), est),
        name="proj_conv",
    )(x_all, mb, mb, w_bf)


def _conv_kernel(v_ref, b_ref, p1_ref, p2_ref, k_ref, o_ref, *, period):
    v = v_ref[...]
    tm = v.shape[0]
    t = lax.broadcasted_iota(I32, v.shape, 0) % period
    s1 = jnp.where(t == 0, p1_ref[...], pltpu.roll(v, 1, 0))
    s2 = jnp.where(t == 0, p2_ref[...], jnp.where(t == 1, p1_ref[...], pltpu.roll(v, 2, 0)))
    conv = k_ref[0:1, :] * s2 + k_ref[1:2, :] * s1 + k_ref[2:3, :] * v
    o_ref[...] = (b_ref[...] * conv).astype(BF16)


def _conv_mix(v, bgate, p1, p2, kern, *, period, row0_blocks, n_rows, prev_per_block):
    d = v.shape[1]
    tm = TOKEN_BLOCK
    row = pl.BlockSpec((tm, d), lambda i: (i + row0_blocks, 0))
    if prev_per_block:
        prev = pl.BlockSpec((SUBLANES, d), lambda i: (i, 0))
    else:
        prev = pl.BlockSpec((tm, d), lambda i: (i, 0))
    kpad = jnp.zeros((SUBLANES, d), F32).at[:CONV_W].set(kern)
    est = 2 * tm * d * (4 * 4 + 2) + 8 * tm * d * 4
    return pl.pallas_call(
        functools.partial(_conv_kernel if not prev_per_block else _conv_kernel_blockprev, period=period),
        grid=(n_rows // tm,),
        in_specs=[row, row, prev, prev, pl.BlockSpec((SUBLANES, d), lambda i: (0, 0))],
        out_specs=pl.BlockSpec((tm, d), lambda i: (i, 0)),
        out_shape=jax.ShapeDtypeStruct((n_rows, d), BF16),
        compiler_params=_cparams(("parallel",), est),
        name="conv_mix",
    )(v, bgate, p1, p2, kpad)


def _conv_kernel_blockprev(v_ref, b_ref, p1_ref, p2_ref, k_ref, o_ref, *, period):
    v = v_ref[...]
    t = lax.broadcasted_iota(I32, v.shape, 0) % period
    p1 = jnp.broadcast_to(p1_ref[0:1, :], v.shape)
    p2 = jnp.broadcast_to(p2_ref[0:1, :], v.shape)
    s1 = jnp.where(t == 0, p1, pltpu.roll(v, 1, 0))
    s2 = jnp.where(t == 0, p2, jnp.where(t == 1, p1, pltpu.roll(v, 2, 0)))
    conv = k_ref[0:1, :] * s2 + k_ref[1:2, :] * s1 + k_ref[2:3, :] * v
    o_ref[...] = (b_ref[...] * conv).astype(BF16)


def _post_mix_kernel(a_ref, x_ref, g_ref, sh_ref, sc_ref, w_ref, lng_ref, lnb_ref, rwh_ref, rwl_ref, rb_ref,
                     x1_ref, u_ref, ti_ref, tg_ref, *, alpha):
    mix = _dot(a_ref[...], w_ref[...])
    x1 = _layer_norm_rows(alpha * x_ref[...] + g_ref[...] * mix, lng_ref[...], lnb_ref[...])
    x1_ref[...] = x1
    u = x1 * (1.0 + sc_ref[...]) + sh_ref[...]
    u_hi = u.astype(BF16)
    bits = lax.bitcast_convert_type(u_hi.astype(F32), I32)
    half = bits.shape[1] // 2
    u_ref[...] = (jnp.right_shift(bits[:, :half], 16) & 0xFFFF) | (bits[:, half:] & HI16_MASK)
    u_lo = (u - u_hi.astype(F32)).astype(BF16)
    logits = _dot(u_hi, rwh_ref[...]) + _dot(u_lo, rwh_ref[...]) + _dot(u_hi, rwl_ref[...]) + rb_ref[...]
    tm = logits.shape[0]
    lane = lax.broadcasted_iota(I32, (tm, LANES), 1)
    lane_f = lane.astype(F32)
    ti = jnp.zeros((tm, LANES), I32)
    tv = jnp.full((tm, LANES), -jnp.inf, F32)
    for r in range(TOP_K):
        m = jnp.max(logits, axis=-1, keepdims=True)
        idx = jnp.min(jnp.where(logits == m, lane_f, float(LANES)), axis=-1, keepdims=True)
        hit = lane_f == idx
        ti = jnp.where(lane == r, idx.astype(I32), ti)
        tv = jnp.where(lane == r, m, tv)
        logits = jnp.where(hit, -jnp.inf, logits)
    e = jnp.exp(tv - jnp.max(tv, axis=-1, keepdims=True))
    tg_ref[...] = e / jnp.sum(e, axis=-1, keepdims=True)
    ti_ref[...] = ti


def _post_mix(a, x_all, mb, w_bf, lng, lnb, rwh, rwl, rb, *, npb, alpha):
    nt, d = x_all.shape
    tm = TOKEN_BLOCK
    row = lambda w: pl.BlockSpec((tm, w), lambda i: (i, 0))
    vec = lambda w: pl.BlockSpec((1, w), lambda i: (0, 0))
    whole = pl.BlockSpec(memory_space=pltpu.VMEM)
    est = w_bf.size * 2 + 2 * d * LANES * 2 + 2 * tm * d * (2 + 4 * 4 + 4 + 2) + 4 * tm * LANES * 4 + 8 * tm * d * 4
    return pl.pallas_call(
        functools.partial(_post_mix_kernel, alpha=alpha),
        grid=(nt // tm,),
        in_specs=[row(d), row(d), _mod_spec(tm, d, 2, npb), _mod_spec(tm, d, 3, npb), _mod_spec(tm, d, 4, npb),
                  whole, vec(d), vec(d), whole, whole, vec(LANES)],
        out_specs=(row(d), row(d // 2), row(LANES), row(LANES)),
        out_shape=(jax.ShapeDtypeStruct((nt, d), F32), jax.ShapeDtypeStruct((nt, d // 2), I32),
                   jax.ShapeDtypeStruct((nt, LANES), I32), jax.ShapeDtypeStruct((nt, LANES), F32)),
        compiler_params=_cparams(("parallel",), est),
        name="post_mix",
    )(a, x_all, mb, mb, mb, w_bf, lng, lnb, rwh, rwl, rb)


def _gather_rows(src, idx):
    n, d = src.shape
    m = idx.shape[0]
    n_workers = SC_CORES * SC_SUBCORES
    per_w = m // n_workers
    rows = GATHER_BUF_BYTES // (d * src.dtype.itemsize)
    nb = GATHER_BUFS
    assert m % n_workers == 0 and per_w % (nb * rows) == 0
    mesh = plsc.VectorSubcoreMesh(core_axis_name="core", subcore_axis_name="subcore")

    @functools.partial(
        pl.kernel, out_type=jax.ShapeDtypeStruct((m, d), src.dtype), mesh=mesh, name="gather_rows",
        scratch_types=[pltpu.VMEM((per_w,), I32)] + [pltpu.VMEM((rows, d), src.dtype)] * nb
                      + [pltpu.SemaphoreType.DMA] * nb)
    def gather(src_hbm, idx_hbm, dst_hbm, idx_v, *bufs_sems):
        bufs, sems = bufs_sems[:nb], bufs_sems[nb:]
        base = (lax.axis_index("subcore") * SC_CORES + lax.axis_index("core")) * per_w
        pltpu.sync_copy(idx_hbm.at[pl.ds(base, per_w)], idx_v)

        def start_gather(off, b):
            return pltpu.async_copy(src_hbm.at[idx_v.at[pl.ds(off + b * rows, rows)]], bufs[b], sems[b])

        def start_write(off, b):
            return pltpu.async_copy(bufs[b], dst_hbm.at[pl.ds(base + off + b * rows, rows)], sems[b])

        @pl.loop(0, per_w, step=nb * rows)
        def _(off):
            gathers = [start_gather(off, 0), start_gather(off, 1)]
            writes = []
            for b in range(nb):
                gathers[b].wait()
                writes.append(start_write(off, b))
                if b + 2 < nb:
                    gathers.append(start_gather(off, b + 2))
            for w in writes:
                w.wait()

    return gather(src, idx)


def _expert_kernel(ie_ref, ib_ref, nv_ref, ob_ref, *refs, n_sub, tf, tn, n_up):
    x_refs = refs[:n_sub]
    wu_ref, bu_ref, wd_ref, bd_ref, o_ref, xb_ref, h_ref, wub_ref, wdp_ref, wdb_ref = refs[n_sub:]
    i, j = pl.program_id(0), pl.program_id(1)
    nv = nv_ref[i]
    half = LANES // 2
    d_half = xb_ref.shape[1] // 2
    n_hid = h_ref.shape[1]
    n_grp = min(MOE_ROW_GROUPS, n_sub)
    sizes = [n_sub // n_grp + (1 if g < n_sub % n_grp else 0) for g in range(n_grp)]
    firsts = [sum(sizes[:g]) for g in range(n_grp)]
    row_groups = [(f * MOE_ROWS, (f + s) * MOE_ROWS, f) for f, s in zip(firsts, sizes)]
    n_active = sum(jnp.where(nv > f, 1, 0) for f in firsts)

    @pl.when((j == 0) & (nv > 0))
    def _():
        for r in range(n_sub):
            rs = slice(r * MOE_ROWS, (r + 1) * MOE_ROWS)
            w = x_refs[r][...]
            xb_ref[rs, :d_half] = lax.bitcast_convert_type(jnp.left_shift(w, 16), F32).astype(BF16)
            xb_ref[rs, d_half:] = lax.bitcast_convert_type(w & HI16_MASK, F32).astype(BF16)

    @pl.when((j < n_up) & (nv > 0))
    def _():
        wub_ref[...] = wu_ref[0, 0].astype(BF16)
        bu = bu_ref[0, 0]
        col = pl.multiple_of(j * tf, tf)

        def up_rows(r0, r1):
            h = _dot(xb_ref[r0:r1, :], wub_ref[...]) + bu
            even = (lax.broadcasted_iota(I32, (r1 - r0, LANES), 1) % 2) == 0
            prods = []
            for t in range(2 * tf // LANES):
                capped = jnp.minimum(h[:, t * LANES:(t + 1) * LANES], SWIGLU_LIMIT)
                gate_act = capped / (1.0 + jnp.exp2(capped * (-SWIGLU_ALPHA * LOG2_E)))
                act = jnp.where(even, gate_act, jnp.maximum(capped, -SWIGLU_LIMIT) + 1.0)
                prods.append(act * pltpu.roll(act, LANES - 1, 1))
            comp = [jnp.where(even, prods[2 * t], pltpu.roll(prods[2 * t + 1], 1, 1))
                    for t in range(tf // LANES)]
            h_ref[r0:r1, pl.ds(col, tf)] = jnp.concatenate(comp, axis=1).astype(BF16)

        for k in range(1, n_grp + 1):
            @pl.when(n_active == k)
            def _(k=k):
                for r0, r1, _unused in row_groups[:k]:
                    up_rows(r0, r1)

    @pl.when((j >= n_up) & (nv > 0))
    def _():
        for c in range(tn // LANES):
            cs = slice(c * LANES, (c + 1) * LANES)
            for qd in range(n_hid // LANES):
                wdp_ref[pl.ds(qd * LANES, half, stride=2), :] = wd_ref[0, 0, qd * LANES:qd * LANES + half, cs]
                wdp_ref[pl.ds(qd * LANES + 1, half, stride=2), :] = wd_ref[0, 0, qd * LANES + half:(qd + 1) * LANES, cs]
            wdb_ref[:, cs] = wdp_ref[...].astype(BF16)
        bd = bd_ref[0, 0]
        for r0, r1, first_sub in row_groups:
            @pl.when(first_sub < nv)
            def _(r0=r0, r1=r1):
                o_ref[r0:r1, :] = _dot(h_ref[r0:r1, :], wdb_ref[...]) + bd

            @pl.when(first_sub >= nv)
            def _(r0=r0, r1=r1):
                o_ref[r0:r1, :] = jnp.zeros((r1 - r0, tn), F32)


def _experts(xs, plan, w_up, b_up, w_down, b_down, *, layer):
    item_e, item_b0, item_nv, item_ob, n_real = plan
    n_items = item_e.shape[0]
    d = w_up.shape[2]
    f = w_down.shape[2]
    tf = min(MOE_FF_TILE, f)
    tn = min(MOE_OUT_TILE, d)
    n_up, n_down = f // tf, d // tn
    n_sub = MOE_SUB
    sb = n_sub * MOE_ROWS
    assert xs.shape[1] * 2 == d and n_sub % 2 == 0

    def x_map(r):
        return lambda i, j, ie, ib, nv, ob: (jnp.where(r < nv[i], ib[i] + r, ib[i]), 0)

    def up_chunk(i, j, nv):
        return jnp.where(nv[i] > 0, jnp.minimum(j, n_up - 1), n_up - 1)

    def down_chunk(i, j, nv):
        return jnp.where(nv[i] > 0, jnp.maximum(j - n_up, 0), n_down - 1)

    in_specs = [pl.BlockSpec((MOE_ROWS, d // 2), x_map(r)) for r in range(n_sub)] + [
        pl.BlockSpec((1, 1, d, 2 * tf), lambda i, j, ie, ib, nv, ob: (layer, ie[i], 0, up_chunk(i, j, nv))),
        pl.BlockSpec((1, 1, 1, 2 * tf), lambda i, j, ie, ib, nv, ob: (layer, ie[i], 0, up_chunk(i, j, nv))),
        pl.BlockSpec((1, 1, f, tn), lambda i, j, ie, ib, nv, ob: (layer, ie[i], 0, down_chunk(i, j, nv))),
        pl.BlockSpec((1, 1, 1, tn), lambda i, j, ie, ib, nv, ob: (layer, ie[i], 0, down_chunk(i, j, nv))),
    ]
    est = 2 * sb * d * 2 + sb * (d + f) * 2 + 2 * (d * 2 * tf + f * tn) * 4 + (d * 2 * tf + f * tn) * 2 \
        + f * LANES * 4 + 2 * sb * tn * 4 + 6 * (sb // 2) * 2 * tf * 4
    depth, n_exp = w_up.shape[0], w_up.shape[1]
    return pl.pallas_call(
        functools.partial(_expert_kernel, n_sub=n_sub, tf=tf, tn=tn, n_up=n_up),
        grid_spec=pltpu.PrefetchScalarGridSpec(
            num_scalar_prefetch=4,
            grid=(n_real, n_up + n_down),
            in_specs=in_specs,
            out_specs=pl.BlockSpec((sb, tn), lambda i, j, ie, ib, nv, ob: (ob[i], down_chunk(i, j, nv))),
            scratch_shapes=[pltpu.VMEM((sb, d), BF16), pltpu.VMEM((sb, f), BF16), pltpu.VMEM((d, 2 * tf), BF16),
                            pltpu.VMEM((f, LANES), F32), pltpu.VMEM((f, tn), BF16)],
        ),
        out_shape=jax.ShapeDtypeStruct((n_items * sb, d), F32),
        compiler_params=_cparams(("arbitrary", "arbitrary"), est),
        name="experts",
    )(item_e, item_b0, item_nv, item_ob, *([xs] * n_sub), w_up,
      b_up.reshape(depth, n_exp, 1, 2 * f), w_down, b_down.reshape(depth, n_exp, 1, d))


def _combine_kernel(*refs, alpha):
    y_refs = refs[:TOP_K]
    tg_ref, x_ref, g_ref, lng_ref, lnb_ref, o_ref = refs[TOP_K:]
    tg = tg_ref[...]
    moe = jnp.zeros(x_ref.shape, F32)
    for r in range(TOP_K):
        moe = moe + tg[:, r:r + 1] * y_refs[r][...]
    o_ref[...] = _layer_norm_rows(alpha * x_ref[...] + g_ref[...] * moe, lng_ref[...], lnb_ref[...])


def _combine(y4, tg, x1, mb, lng, lnb, *, npb, alpha):
    nt, d = x1.shape
    tm = TOKEN_BLOCK
    nbt = nt // tm
    row = lambda w: pl.BlockSpec((tm, w), lambda i: (i, 0))
    vec = pl.BlockSpec((1, d), lambda i: (0, 0))
    y_specs = [pl.BlockSpec((tm, d), lambda i, r=r: (r * nbt + i, 0)) for r in range(TOP_K)]
    est = 2 * tm * (TOP_K * d + LANES + 3 * d) * 4 + 6 * tm * d * 4
    return pl.pallas_call(
        functools.partial(_combine_kernel, alpha=alpha),
        grid=(nbt,),
        in_specs=y_specs + [row(LANES), row(d), _mod_spec(tm, d, 5, npb), vec, vec],
        out_specs=row(d),
        out_shape=jax.ShapeDtypeStruct((nt, d), F32),
        compiler_params=_cparams(("parallel",), est),
        name="moe_combine",
    )(*([y4] * TOP_K), tg, x1, mb, lng, lnb)


def _moe_plan(top_e, n_exp):
    nt = top_e.shape[0]
    na = nt * TOP_K
    sb = MOE_SUB * MOE_ROWS
    e_flat = top_e.reshape(-1)
    order = jnp.argsort(e_flat).astype(I32)
    e_sorted = e_flat[order]
    experts = jnp.arange(n_exp, dtype=I32)
    counts = jnp.sum((e_flat[None, :] == experts[:, None]).astype(I32), axis=1)
    starts = jnp.cumsum(counts) - counts
    padded = (counts + MOE_ROWS - 1) // MOE_ROWS * MOE_ROWS
    gend = jnp.cumsum(padded)
    gstart = gend - padded
    rank = jnp.arange(na, dtype=I32) - starts[e_sorted]
    n_slot_blocks = na // MOE_ROWS + n_exp
    slots = jnp.arange(n_slot_blocks * MOE_ROWS, dtype=I32)
    e_slot = jnp.minimum(jnp.sum((gend[None, :] <= slots[:, None]).astype(I32), axis=1), n_exp - 1)
    pos = slots - gstart[e_slot]
    src = jnp.clip(starts[e_slot] + pos, 0, na - 1)
    slot_tok = jnp.where((pos >= 0) & (pos < counts[e_slot]), order[src] // TOP_K, 0).astype(I32)
    items_e = (counts + sb - 1) // sb
    item_end = jnp.cumsum(items_e)
    item_first = item_end - items_e
    out_row_sorted = (item_first[e_sorted] + rank // sb) * sb + rank % sb
    _, out_row_of_assign = lax.sort((order, out_row_sorted), num_keys=1)
    n_items = na // sb + n_exp
    ids = jnp.arange(n_items, dtype=I32)
    n_real = item_end[-1]
    is_real = ids < n_real
    ids_c = jnp.minimum(ids, n_real - 1).astype(I32)
    e_of = jnp.minimum(jnp.sum((item_end[None, :] <= ids_c[:, None]).astype(I32), axis=1), n_exp - 1)
    s_in_e = ids_c - item_first[e_of]
    rows_left = counts[e_of] - s_in_e * sb
    nvb = jnp.where(is_real, (jnp.minimum(rows_left, sb) + MOE_ROWS - 1) // MOE_ROWS, 0).astype(I32)
    blk0 = ((gstart[e_of] + s_in_e * sb) // MOE_ROWS).astype(I32)
    return slot_tok, out_row_of_assign, (e_of, blk0, nvb, ids_c, n_real.astype(I32))


def _rope_tables(pos, width):
    inv = ROPE_THETA ** (-jnp.arange(0, width, 2, dtype=F32) / width)
    ang = pos.astype(F32)[:, None] * inv[None, :]
    c, s = jnp.cos(ang), jnp.sin(ang)
    reps = LANES // width
    return jnp.tile(jnp.concatenate([c, c], axis=-1), (1, reps)), jnp.tile(jnp.concatenate([-s, s], axis=-1), (1, reps))


def _pad_lanes(v, fill=0.0):
    return jnp.full((1, LANES), fill, F32).at[0, :v.shape[0]].set(v)


def kernel(x_prompt, x_sample, cache_k, cache_v, cache_kidx, state_conv, page_table, c_prompt, c_sample, ada_w, ada_b, ln_g, ln_b, attn_w_in, attn_kidx_g, attn_kidx_b, attn_w_out, conv_w_in, conv_kernel, conv_w_out, router_w, router_b, expert_w_up, expert_b_up, expert_w_down, expert_b_down):
    bp, n_prompt, d = x_prompt.shape
    b, t_new, _ = x_sample.shape
    depth = ada_w.shape[0]
    n_kv = cache_k.shape[3]
    page = cache_k.shape[2]
    n_pages = page_table.shape[1]
    past = n_pages * page
    n_exp = router_w.shape[-1]
    n_sample = b * t_new
    nt = n_prompt + n_sample
    tm = TOKEN_BLOCK
    assert bp == 1 and t_new == SUBLANES and n_prompt % tm == 0 and n_sample % tm == 0 and n_exp <= LANES
    assert d % 512 == 0 and (IDX_HEADS * IDX_DIM) % 512 == 0 and n_prompt % min(KEY_CHUNK, n_prompt) == 0
    npb = n_prompt // tm
    alpha = (2.0 * depth) ** 0.25
    kvw = n_kv * HEAD_DIM
    iw = IDX_HEADS * IDX_DIM

    x_all = jnp.concatenate([x_prompt.reshape(n_prompt, d), x_sample.reshape(n_sample, d)], axis=0)
    c_all = jnp.concatenate([c_prompt, c_sample], axis=0)
    mc = -(-c_all.shape[0] // SUBLANES) * SUBLANES
    c_all = jnp.pad(c_all, ((0, mc - c_all.shape[0]), (0, 0)))
    mod = _ada_mod(c_all, ada_w, ada_b)

    pos = jnp.concatenate([jnp.arange(n_prompt, dtype=I32), jnp.tile(past + jnp.arange(t_new, dtype=I32), b)])
    tabs = _rope_tables(pos, HEAD_DIM) + _rope_tables(pos, IDX_DIM)
    pt_flat = page_table.reshape(-1).astype(I32)
    n_pool = cache_k.shape[1]
    n_cache = cache_k.shape[0] * n_pool
    cache_k2 = cache_k.reshape(n_cache, page * n_kv, HEAD_DIM)
    cache_v2 = cache_v.reshape(n_cache, page * n_kv, HEAD_DIM)
    cache_kit = jnp.swapaxes(cache_kidx.reshape(n_cache, page, IDX_DIM), 1, 2)

    outs = dict(kp=[], vp=[], kip=[], ks=[], vs=[], kis=[], cp=[], cs=[])
    for i in range(depth):
        mb = jnp.concatenate([jnp.broadcast_to(mod[i, 0:1], (tm, 6 * d)),
                              jnp.repeat(mod[i, 1:1 + b], t_new, axis=0)], axis=0)
        if i % 2 == 0:
            a = i // 2
            w_pad = jnp.pad(attn_w_in[a], ((0, 0), (0, LANES - IDX_DIM - IDX_HEADS))).astype(BF16)
            q, k32, v32, kb, vb, qa, qb, kw, ke = _proj_attn(
                x_all, mb, w_pad, tabs, _pad_lanes(attn_kidx_g[a]), _pad_lanes(attn_kidx_b[a]), npb=npb, n_kv=n_kv)
            topk_p = min(TOPK_MAX, n_prompt // TOPK_DIV)
            o_p = _dsa_prompt(q, qa, qb, kw, ke[:n_prompt], kb[:n_prompt], vb[:n_prompt],
                              n_prompt=n_prompt, topk=topk_p, n_kv=n_kv)
            seq3 = lambda z: z[n_prompt:].astype(F32).reshape(b, t_new, z.shape[1])
            keys = _dsa_sample_scores(pt_flat, seq3(qa), seq3(qb), seq3(kw), seq3(ke),
                                      cache_kit, n_pages=n_pages, page0=a * n_pool)
            topk_s = min(TOPK_MAX, (past + t_new) // TOPK_DIV)
            thr = _thresholds(keys.reshape(n_sample, -1), topk=topk_s)
            o_s = _dsa_sample_attn(pt_flat, seq3(q), keys, thr.reshape(b, t_new, LANES), seq3(k32), seq3(v32),
                                   cache_k2, cache_v2, n_pages=n_pages, n_kv=n_kv, page0=a * n_pool)
            mix_in = jnp.concatenate([o_p, o_s.reshape(n_sample, d).astype(BF16)], axis=0)
            w_out = attn_w_out[a].astype(BF16)
            outs['kp'].append(k32[:n_prompt].reshape(1, n_prompt, n_kv, HEAD_DIM))
            outs['vp'].append(v32[:n_prompt].reshape(1, n_prompt, n_kv, HEAD_DIM))
            outs['kip'].append(kw[:n_prompt, :IDX_DIM].reshape(1, n_prompt, IDX_DIM))
            outs['ks'].append(k32[n_prompt:].reshape(b, t_new, n_kv, HEAD_DIM))
            outs['vs'].append(v32[n_prompt:].reshape(b, t_new, n_kv, HEAD_DIM))
            outs['kis'].append(kw[n_prompt:, :IDX_DIM].reshape(b, t_new, IDX_DIM))
        else:
            ci = i // 2
            bgate, v = _proj_conv(x_all, mb, conv_w_in[ci].astype(BF16), npb=npb)
            zrow = jnp.zeros((1, d), F32)
            prev1 = jnp.concatenate([zrow, v[tm - 1:n_prompt - 1:tm]], axis=0)
            prev2 = jnp.concatenate([zrow, v[tm - 2:n_prompt - 2:tm]], axis=0)
            mix_p = _conv_mix(v, bgate, jnp.repeat(prev1, SUBLANES, axis=0), jnp.repeat(prev2, SUBLANES, axis=0),
                              conv_kernel[ci], period=tm, row0_blocks=0, n_rows=n_prompt, prev_per_block=True)
            st = state_conv[ci]
            mix_s = _conv_mix(v, bgate, jnp.repeat(st[:, 1], t_new, axis=0), jnp.repeat(st[:, 0], t_new, axis=0),
                              conv_kernel[ci], period=t_new, row0_blocks=npb, n_rows=n_sample, prev_per_block=False)
            mix_in = jnp.concatenate([mix_p, mix_s], axis=0)
            w_out = conv_w_out[ci].astype(BF16)
            outs['cp'].append(v[n_prompt - (CONV_W - 1):n_prompt].reshape(1, CONV_W - 1, d))
            outs['cs'].append(v[n_prompt:].reshape(b, t_new, d)[:, t_new - (CONV_W - 1):])

        rw = jnp.pad(router_w[i], ((0, 0), (0, LANES - n_exp)))
        rwh = rw.astype(BF16)
        rwl = (rw - rwh.astype(F32)).astype(BF16)
        rb = _pad_lanes(router_b[i], fill=-1e30)
        x1, u2, ti, tg = _post_mix(mix_in, x_all, mb, w_out, ln_g[i, 0:1], ln_b[i, 0:1], rwh, rwl, rb,
                                   npb=npb, alpha=alpha)
        slot_tok, out_row_of_assign, plan = _moe_plan(ti[:, :TOP_K], n_exp)
        xs = _gather_rows(u2, slot_tok)
        ys = _experts(xs, plan, expert_w_up, expert_b_up, expert_w_down, expert_b_down, layer=i)
        y4 = _gather_rows(ys, out_row_of_assign.reshape(nt, TOP_K).T.reshape(-1))
        x_all = _combine(y4, tg, x1, mb, ln_g[i, 1:2], ln_b[i, 1:2], npb=npb, alpha=alpha)

    y_prompt = x_all[:n_prompt].reshape(1, n_prompt, d)
    y_sample = x_all[n_prompt:].reshape(b, t_new, d)
    st = lambda name: jnp.stack(outs[name])
    return (y_prompt, y_sample, st('kp'), st('vp'), st('kip'), st('ks'), st('vs'), st('kis'), st('cp'), st('cs'))
```
